```python
import jax, jax.numpy as jnp
from jax import lax
import numpy as np

D_MODEL = 1024
BATCH = 16
SEQ = 256
DEPTH = 1
DEC_BATCH = 4
DEC_SEQ = 1024
PAST_LEN = 512

GRID_W = 64
N_HEADS = 8
KV_HEADS = 2
Q_PER_KV = N_HEADS // KV_HEADS
HEAD_DIM = 64
ATTN_WIDTH = N_HEADS * HEAD_DIM
KV_WIDTH = KV_HEADS * HEAD_DIM
Q_BLOCK = 128
ROPE_THETA = 10000.0
ROPE_PAIRS = HEAD_DIM // 4
R_HEADS = 8
R_HEAD_DIM = 64
R_WIDTH = R_HEADS * R_HEAD_DIM
DECAY_RANK = 64
ICLR_RANK = 64
GATE_RANK = 128
R_IN_WIDTH = 3 * R_WIDTH + DECAY_RANK + ICLR_RANK + GATE_RANK
IN_WIDTH = ATTN_WIDTH + 2 * KV_WIDTH + R_IN_WIDTH + 2 * D_MODEL
D_FF = 2816
N_MOD = 6
NORM_EPS = 1e-6
GN_EPS = 64e-5
DECAY_OFFSET = 0.5

kernel_name = "hybrid_dit_attn_rwkv7_prefix_step"


def _rmsnorm(x, g):
    xf = x.astype(jnp.float32)
    y = xf * lax.rsqrt(jnp.mean(xf * xf, axis=-1, keepdims=True) + NORM_EPS)
    return (y * g.astype(jnp.float32)).astype(x.dtype)


def _modulate(h, shift, scale):
    return h * (1 + scale[:, None, :]) + shift[:, None, :]


def _dwconv3(x, w):
    xp = jnp.pad(x, ((0, 0), (1, 1), (0, 0)))
    return xp[:, :-2] * w[0] + xp[:, 1:-1] * w[1] + xp[:, 2:] * w[2]


def _axial_rope_tables(rows):
    row = jnp.repeat(jnp.arange(rows), GRID_W).astype(jnp.float32)
    col = jnp.tile(jnp.arange(GRID_W), rows).astype(jnp.float32)
    inv = 1.0 / (ROPE_THETA ** (jnp.arange(ROPE_PAIRS, dtype=jnp.float32) / ROPE_PAIRS))
    ang = jnp.concatenate([row[:, None] * inv, col[:, None] * inv], axis=-1)
    return jnp.cos(ang), jnp.sin(ang)


def _apply_rope(x, cos, sin):
    x1, x2 = jnp.split(x, 2, axis=-1)
    cos = cos.astype(x.dtype)
    sin = sin.astype(x.dtype)
    return jnp.concatenate([x1 * cos - x2 * sin, x2 * cos + x1 * sin], axis=-1)


def _block_attention(q, k, v):
    b, h, t, d = q.shape
    nb = t // Q_BLOCK
    qb = jnp.moveaxis(q.reshape(b, KV_HEADS, Q_PER_KV, nb, Q_BLOCK, d), 3, 0)
    scale = d ** -0.5

    def one_block(qblk):
        s = jnp.einsum("bkgqd,bkld->bkgql", qblk, k).astype(jnp.float32) * scale
        p = jax.nn.softmax(s, axis=-1).astype(v.dtype)
        return jnp.einsum("bkgql,bkld->bkgqd", p, v)

    o = lax.map(one_block, qb)
    o = jnp.moveaxis(o, 0, 3).reshape(b, h, t, d)
    return o.transpose(0, 2, 1, 3).reshape(b, t, h * d)


def _wkv7_scan(r, w, k, v, kk, a, s0, reverse):
    def step(s, inp):
        r_t, w_t, k_t, v_t, kk_t, a_t = inp
        sa = jnp.einsum("bhvk,bhk->bhv", s, -kk_t)
        s = (s * w_t[:, :, None, :] + sa[..., None] * (kk_t * a_t)[:, :, None, :]
             + v_t[..., None] * k_t[:, :, None, :])
        return s, jnp.einsum("bhvk,bhk->bhv", s, r_t)

    xs = tuple(jnp.moveaxis(z, 1, 0) for z in (r, w, k, v, kk, a))
    s_final, ys = lax.scan(step, s0, xs, reverse=reverse)
    return jnp.moveaxis(ys, 0, 1), s_final


def _rwkv7_bidir(u, p, s0):
    b, t, _ = u.shape
    f32 = jnp.float32
    u = _dwconv3(u, p["rwkv_conv"])
    splits = np.cumsum([R_WIDTH, R_WIDTH, R_WIDTH, DECAY_RANK, ICLR_RANK]).tolist()
    r, k, v, wd, ad, gd = jnp.split(u, splits, axis=-1)

    def heads(z):
        return z.reshape(b, t, R_HEADS, R_HEAD_DIM).astype(f32)

    rh, kh, vh = heads(r), heads(k), heads(v)
    kk = heads(k * p["rwkv_k_k"])
    kk = kk * lax.rsqrt(jnp.sum(kk * kk, axis=-1, keepdims=True) + 1e-12)
    ka = p["rwkv_k_a"].reshape(R_HEADS, R_HEAD_DIM).astype(f32)
    r_k = p["rwkv_r_k"].astype(f32)
    wd_t = jnp.tanh(wd)
    ys, bonus, finals = [], [], []
    for d, reverse in enumerate((False, True)):
        logw = -jax.nn.softplus(-(p["rwkv_w0"][d] + wd_t @ p["rwkv_w2"][d])) - DECAY_OFFSET
        w = jnp.exp(-jnp.exp(heads(logw)))
        a = jax.nn.sigmoid(heads(p["rwkv_a0"][d] + ad @ p["rwkv_a2"][d]))
        kd = kh * (1 + (a - 1) * ka)
        y, s_fin = _wkv7_scan(rh, w, kd, vh, kk, a, s0[:, d].astype(f32), reverse)
        ys.append(y)
        bonus.append(jnp.sum(rh * kd * r_k, axis=-1, keepdims=True) * vh)
        finals.append(s_fin)
    y = ys[0] + ys[1]
    mu = jnp.mean(y, axis=-1, keepdims=True)
    var = jnp.mean(jnp.square(y - mu), axis=-1, keepdims=True)
    y = ((y - mu) * lax.rsqrt(var + GN_EPS)).reshape(b, t, R_WIDTH)
    y = (y * p["rwkv_ln_g"].astype(f32) + p["rwkv_ln_b"].astype(f32)
         + (bonus[0] + bonus[1]).reshape(b, t, R_WIDTH))
    g = jax.nn.sigmoid(gd) @ p["rwkv_g2"]
    return y.astype(u.dtype) * g, jnp.stack(finals, axis=1)


def _layer(x, mod, p, rope, ctx_k, ctx_v, s0):
    b, t, _ = x.shape
    shift1, scale1, gate1, shift2, scale2, gate2 = jnp.split(mod, N_MOD, axis=-1)
    h = _modulate(_rmsnorm(x, p["norm1_g"]), shift1, scale1)
    proj = h @ p["w_in"]
    splits = np.cumsum([ATTN_WIDTH, KV_WIDTH, KV_WIDTH, R_IN_WIDTH, D_MODEL]).tolist()
    q, k, v, u, ga, gr = jnp.split(proj, splits, axis=-1)
    q = _rmsnorm(q.reshape(b, t, N_HEADS, HEAD_DIM), p["q_norm_g"]).transpose(0, 2, 1, 3)
    k = _rmsnorm(k.reshape(b, t, KV_HEADS, HEAD_DIM), p["k_norm_g"]).transpose(0, 2, 1, 3)
    v = v.reshape(b, t, KV_HEADS, HEAD_DIM).transpose(0, 2, 1, 3)
    if rope is None:
        keys, vals = k, v
    else:
        cos, sin = rope
        q = _apply_rope(q, cos, sin)
        keys = jnp.concatenate([_apply_rope(k, cos, sin), ctx_k.astype(k.dtype)], axis=2)
        vals = jnp.concatenate([v, ctx_v.astype(v.dtype)], axis=2)
    attn_o = _block_attention(q, keys, vals)
    rwkv_o, s_final = _rwkv7_bidir(u, p, s0)
    merged = (jax.nn.sigmoid(ga) * (attn_o @ p["w_attn_br"])
              + jax.nn.sigmoid(gr) * (rwkv_o @ p["w_rwkv_br"]))
    x = x + gate1[:, None, :] * (merged @ p["w_out"])
    h2 = _modulate(_rmsnorm(x, p["norm2_g"]), shift2, scale2)
    up = _dwconv3(h2 @ p["ffn_up"], p["ffn_conv"])
    up_u, up_g = jnp.split(up, 2, axis=-1)
    x = x + gate2[:, None, :] * ((jax.nn.silu(up_g) * up_u) @ p["ffn_down"])
    return x, k, v, s_final


def setup_inputs(seed: int = 0) -> dict:
    key = jax.random.key(seed)
    ks = jax.random.split(key, 40)
    nrm = jax.random.normal
    f32 = jnp.float32
    L = DEPTH
    conv_base = jnp.array([0.3, 1.0, 0.3], dtype=f32)[:, None]
    return {
        "x_prompt": nrm(ks[0], (BATCH, SEQ, D_MODEL), f32),
        "x_sample": nrm(ks[1], (DEC_BATCH, DEC_SEQ, D_MODEL), f32),
        "cache_k": nrm(ks[2], (DEC_BATCH, L, KV_HEADS, PAST_LEN, HEAD_DIM), f32),
        "cache_v": nrm(ks[3], (DEC_BATCH, L, KV_HEADS, PAST_LEN, HEAD_DIM), f32),
        "state_rwkv": 0.5 * nrm(ks[4], (DEC_BATCH, L, 2, R_HEADS, R_HEAD_DIM, R_HEAD_DIM), f32),
        "c": nrm(ks[5], (DEC_BATCH, D_MODEL), f32),
        "c_ctx": nrm(ks[6], (D_MODEL,), f32),
        "w_ada": 0.5 * D_MODEL ** -0.5 * nrm(ks[7], (L, D_MODEL, N_MOD * D_MODEL), f32),
        "b_ada": 0.02 * nrm(ks[8], (L, N_MOD * D_MODEL), f32),
        "norm1_g": 1.0 + 0.02 * nrm(ks[9], (L, D_MODEL), f32),
        "w_in": D_MODEL ** -0.5 * nrm(ks[10], (L, D_MODEL, IN_WIDTH), f32),
        "q_norm_g": 1.0 + 0.02 * nrm(ks[11], (L, HEAD_DIM), f32),
        "k_norm_g": 1.0 + 0.02 * nrm(ks[12], (L, HEAD_DIM), f32),
        "rwkv_conv": conv_base + 0.1 * nrm(ks[13], (L, 3, R_IN_WIDTH), f32),
        "rwkv_w0": 1.5 * nrm(ks[14], (L, 2, R_WIDTH), f32) - 1.5,
        "rwkv_w2": 0.5 * DECAY_RANK ** -0.5 * nrm(ks[15], (L, 2, DECAY_RANK, R_WIDTH), f32),
        "rwkv_a0": 0.5 * nrm(ks[16], (L, 2, R_WIDTH), f32),
        "rwkv_a2": 0.5 * ICLR_RANK ** -0.5 * nrm(ks[17], (L, 2, ICLR_RANK, R_WIDTH), f32),
        "rwkv_g2": GATE_RANK ** -0.5 * nrm(ks[18], (L, GATE_RANK, R_WIDTH), f32),
        "rwkv_k_k": 1.0 + 0.1 * nrm(ks[19], (L, R_WIDTH), f32),
        "rwkv_k_a": 1.0 + 0.1 * nrm(ks[20], (L, R_WIDTH), f32),
        "rwkv_r_k": 0.1 * nrm(ks[21], (L, R_HEADS, R_HEAD_DIM), f32),
        "rwkv_ln_g": 1.0 + 0.02 * nrm(ks[22], (L, R_WIDTH), f32),
        "rwkv_ln_b": 0.02 * nrm(ks[23], (L, R_WIDTH), f32),
        "w_attn_br": ATTN_WIDTH ** -0.5 * nrm(ks[24], (L, ATTN_WIDTH, D_MODEL), f32),
        "w_rwkv_br": R_WIDTH ** -0.5 * nrm(ks[25], (L, R_WIDTH, D_MODEL), f32),
        "w_out": D_MODEL ** -0.5 * nrm(ks[26], (L, D_MODEL, D_MODEL), f32),
        "norm2_g": 1.0 + 0.02 * nrm(ks[27], (L, D_MODEL), f32),
        "ffn_up": D_MODEL ** -0.5 * nrm(ks[28], (L, D_MODEL, 2 * D_FF), f32),
        "ffn_conv": conv_base + 0.1 * nrm(ks[29], (L, 3, 2 * D_FF), f32),
        "ffn_down": D_FF ** -0.5 * nrm(ks[30], (L, D_FF, D_MODEL), f32),
        "final_norm_g": 1.0 + 0.02 * nrm(ks[31], (D_MODEL,), f32),
    }


def reference(x_prompt, x_sample, cache_k, cache_v, state_rwkv, c, c_ctx,
              w_ada, b_ada, norm1_g, w_in, q_norm_g, k_norm_g, rwkv_conv,
              rwkv_w0, rwkv_w2, rwkv_a0, rwkv_a2, rwkv_g2, rwkv_k_k, rwkv_k_a,
              rwkv_r_k, rwkv_ln_g, rwkv_ln_b, w_attn_br, w_rwkv_br, w_out,
              norm2_g, ffn_up, ffn_conv, ffn_down, final_norm_g):
    rows = x_sample.shape[1] // GRID_W
    rope = _axial_rope_tables(rows)
    s_zero = jnp.zeros((x_prompt.shape[0], 2, R_HEADS, R_HEAD_DIM, R_HEAD_DIM), jnp.float32)
    xp, xs = x_prompt, x_sample
    new_k, new_v, new_s = [], [], []
    for l in range(DEPTH):
        p = {
            "norm1_g": norm1_g[l], "w_in": w_in[l], "q_norm_g": q_norm_g[l],
            "k_norm_g": k_norm_g[l], "rwkv_conv": rwkv_conv[l], "rwkv_w0": rwkv_w0[l],
            "rwkv_w2": rwkv_w2[l], "rwkv_a0": rwkv_a0[l], "rwkv_a2": rwkv_a2[l],
            "rwkv_g2": rwkv_g2[l], "rwkv_k_k": rwkv_k_k[l], "rwkv_k_a": rwkv_k_a[l],
            "rwkv_r_k": rwkv_r_k[l], "rwkv_ln_g": rwkv_ln_g[l], "rwkv_ln_b": rwkv_ln_b[l],
            "w_attn_br": w_attn_br[l], "w_rwkv_br": w_rwkv_br[l], "w_out": w_out[l],
            "norm2_g": norm2_g[l], "ffn_up": ffn_up[l], "ffn_conv": ffn_conv[l],
            "ffn_down": ffn_down[l],
        }
        mod_ctx = (jax.nn.silu(c_ctx) @ w_ada[l] + b_ada[l])[None, :]
        mod_lat = jax.nn.silu(c) @ w_ada[l] + b_ada[l]
        xp, k_ctx, v_ctx, s_ctx = _layer(xp, mod_ctx, p, None, None, None, s_zero)
        new_k.append(k_ctx)
        new_v.append(v_ctx)
        new_s.append(s_ctx)
        xs, _, _, _ = _layer(xs, mod_lat, p, rope, cache_k[:, l], cache_v[:, l], state_rwkv[:, l])
    y_prompt = _rmsnorm(xp, final_norm_g)
    y_sample = _rmsnorm(xs, final_norm_g)
    new_cache_k = jnp.stack(new_k, axis=1)
    new_cache_v = jnp.stack(new_v, axis=1)
    new_state_rwkv = jnp.stack(new_s, axis=1)
    return (y_prompt, y_sample, new_cache_k, new_cache_v, new_state_rwkv)
```

```python
import functools

import jax
import jax.numpy as jnp
import numpy as np
from jax import lax
from jax.experimental import pallas as pl
from jax.experimental.pallas import tpu as pltpu

D_MODEL = 1024
GRID_W = 64
N_HEADS = 8
KV_HEADS = 2
Q_PER_KV = N_HEADS // KV_HEADS
HEAD_DIM = 64
ATTN_WIDTH = N_HEADS * HEAD_DIM
KV_WIDTH = KV_HEADS * HEAD_DIM
ROPE_THETA = 10000.0
ROPE_PAIRS = HEAD_DIM // 4
R_HEADS = 8
R_HEAD_DIM = 64
R_WIDTH = R_HEADS * R_HEAD_DIM
DECAY_RANK = 64
ICLR_RANK = 64
GATE_RANK = 128
LORA_WIDTH = DECAY_RANK + ICLR_RANK + GATE_RANK
R_IN_WIDTH = 3 * R_WIDTH + LORA_WIDTH
D_FF = 2816
N_MOD = 6
NORM_EPS = 1e-6
GN_EPS = 64e-5
DECAY_SCALE = float(np.exp(-0.5))

F32 = jnp.float32
BF16 = jnp.bfloat16

VMEM_LIMIT_BYTES = 56 * 1024 * 1024
PROJ_ROWS = 256
FFN_ROWS = 1024
FFN_TILE = 256
Q_ROWS = 128
GROUP = 256
CHUNK = 64

_NN = (((1,), (0,)), ((), ()))
_NT = (((1,), (1,)), ((), ()))
_TN = (((0,), (0,)), ((), ()))

P_LORA = 1
P_AMAT = 1
P_INV = 1
P_AV = 1
P_SERIAL = 1


def _dg(a, b, dims):
    return lax.dot_general(a, b, dims, preferred_element_type=F32)


def _dot(a, b):
    return _dg(a, b, _NN)


def _split2(x):
    hi = x.astype(BF16)
    return hi, (x - hi.astype(F32)).astype(BF16)


def _mm(a, b, dims=_NN, passes=1):
    if passes == 1:
        return _dg(a.astype(BF16), b.astype(BF16), dims)
    a_hi, a_lo = _split2(a)
    b_hi, b_lo = _split2(b)
    return _dg(a_hi, b_hi, dims) + (_dg(a_hi, b_lo, dims) + _dg(a_lo, b_hi, dims))


def _mm_exact_lhs(m, x):
    hi = x.astype(BF16)
    r1 = x - hi.astype(F32)
    mid = r1.astype(BF16)
    lo = (r1 - mid.astype(F32)).astype(BF16)
    return _dot(m, hi) + (_dot(m, mid) + _dot(m, lo))


def _sigmoid(x):
    return 1.0 / (1.0 + jnp.exp(-x))


def _rms(x):
    return x * lax.rsqrt(jnp.mean(x * x, axis=-1, keepdims=True) + NORM_EPS)


def _params(*semantics):
    return pltpu.CompilerParams(dimension_semantics=semantics,
                                vmem_limit_bytes=VMEM_LIMIT_BYTES)


def _mod_kernel(c_ref, w_ref, b_ref, o_ref):
    c = c_ref[...]
    s = (c * _sigmoid(c)).astype(BF16)
    o_ref[...] = _dot(s, w_ref[...].astype(BF16)) + b_ref[...]


def _modulation(cc, w_ada, b_ada):
    rows = cc.shape[0]
    n = w_ada.shape[1]
    tile = 1024
    return pl.pallas_call(
        _mod_kernel,
        out_shape=jax.ShapeDtypeStruct((rows, n), F32),
        grid=(n // tile,),
        in_specs=[
            pl.BlockSpec((rows, D_MODEL), lambda j: (0, 0)),
            pl.BlockSpec((D_MODEL, tile), lambda j: (0, j)),
            pl.BlockSpec((1, tile), lambda j: (0, j)),
        ],
        out_specs=pl.BlockSpec((rows, tile), lambda j: (0, j)),
        compiler_params=_params("arbitrary"),
        name="adaln_mod",
    )(cc, w_ada, b_ada)


def _inproj_kernel(x_ref, mod_ref, g_ref, w_ref, qkv_ref, rkv_ref, lora_ref, sga_ref, sgr_ref):
    x = x_ref[...]
    h = _rms(x) * g_ref[...]
    h = h * (1.0 + mod_ref[1:2, :]) + mod_ref[0:1, :]
    hb = h.astype(BF16)
    c0 = ATTN_WIDTH + 2 * KV_WIDTH
    c1 = c0 + 3 * R_WIDTH
    c2 = c1 + LORA_WIDTH
    c3 = c2 + D_MODEL
    qkv_ref[...] = _dot(hb, w_ref[:, 0:c0])
    rkv_ref[...] = _dot(hb, w_ref[:, c0:c1])
    lora_ref[...] = _dot(hb, w_ref[:, c1:c2])
    sga_ref[...] = _sigmoid(_dot(hb, w_ref[:, c2:c3]))
    sgr_ref[...] = _sigmoid(_dot(hb, w_ref[:, c3:c3 + D_MODEL]))


def _in_projection(x2d, mod, norm1_g, w_in_bf16, seq_len):
    n_tok = x2d.shape[0]
    tm = PROJ_ROWS
    in_width = w_in_bf16.shape[1]
    widths = (ATTN_WIDTH + 2 * KV_WIDTH, 3 * R_WIDTH, LORA_WIDTH, D_MODEL, D_MODEL)
    row = lambda i: (i, 0)
    return pl.pallas_call(
        _inproj_kernel,
        out_shape=tuple(jax.ShapeDtypeStruct((n_tok, w), F32) for w in widths),
        grid=(n_tok // tm,),
        in_specs=[
            pl.BlockSpec((tm, D_MODEL), row),
            pl.BlockSpec((None, N_MOD, D_MODEL), lambda i: ((i * tm) // seq_len, 0, 0)),
            pl.BlockSpec((1, D_MODEL), lambda i: (0, 0)),
            pl.BlockSpec((D_MODEL, in_width), lambda i: (0, 0)),
        ],
        out_specs=tuple(pl.BlockSpec((tm, w), row) for w in widths),
        compiler_params=_params("arbitrary"),
        name="in_proj",
    )(x2d, mod, norm1_g, w_in_bf16)


def _rope(x, cos2, sin2):
    half = HEAD_DIM // 2
    partner = jnp.concatenate([x[:, half:], x[:, :half]], axis=-1)
    return x * cos2 + partner * sin2


def _attn_kernel(*refs, seq_len, past_len, latent):
    if latent:
        (q_ref, k_ref, v_ref, ck_ref, cv_ref, cos_ref, sin_ref, qg_ref, kg_ref,
         o_ref, qs_ref, ks_ref, vs_ref) = refs
    else:
        (q_ref, k_ref, v_ref, qg_ref, kg_ref, o_ref, kn_ref, qs_ref, ks_ref, vs_ref) = refs
    t = seq_len
    kn = _rms(k_ref[...]) * kg_ref[...]
    if latent:
        cos2 = cos_ref[...]
        sin2 = sin_ref[...]
        ks_ref[0:t, :] = _rope(kn, cos2, sin2).astype(BF16)
        ks_ref[t:t + past_len, :] = ck_ref[...].astype(BF16)
        vs_ref[0:t, :] = v_ref[...].astype(BF16)
        vs_ref[t:t + past_len, :] = cv_ref[...].astype(BF16)
    else:
        kn_ref[...] = kn
        ks_ref[...] = kn.astype(BF16)
        vs_ref[...] = v_ref[...].astype(BF16)
    for hq in range(Q_PER_KV):
        qn = _rms(q_ref[hq]) * qg_ref[...]
        if latent:
            qn = _rope(qn, cos2, sin2)
        qs_ref[hq] = qn.astype(BF16)
    scale = HEAD_DIM ** -0.5

    def body(i, carry):
        hq = i // (t // Q_ROWS)
        qb = i % (t // Q_ROWS)
        r0 = pl.multiple_of(qb * Q_ROWS, Q_ROWS)
        q = qs_ref[hq, pl.ds(r0, Q_ROWS), :]
        s = _dg(q, ks_ref[...], _NT) * scale
        m = jnp.max(s, axis=-1, keepdims=True)
        p = jnp.exp(s - m)
        l = jnp.sum(p, axis=-1, keepdims=True)
        o = _dot(p.astype(BF16), vs_ref[...])
        o_ref[hq, pl.ds(r0, Q_ROWS), :] = o / l
        return carry

    lax.fori_loop(0, Q_PER_KV * (t // Q_ROWS), body, 0)


def _attention(q, k, v, qg, kg, cache=None, rope=None):
    b, _, t, d = q.shape
    latent = cache is not None
    past_len = cache[0].shape[2] if latent else 0
    total = t + past_len
    kv_spec = pl.BlockSpec((None, None, t, d), lambda i, g: (i, g, 0, 0))
    q_spec = pl.BlockSpec((None, Q_PER_KV, t, d), lambda i, g: (i, g, 0, 0))
    vec_spec = pl.BlockSpec((1, d), lambda i, g: (0, 0))
    scratch = [pltpu.VMEM((Q_PER_KV, t, d), BF16), pltpu.VMEM((total, d), BF16),
               pltpu.VMEM((total, d), BF16)]
    kern = functools.partial(_attn_kernel, seq_len=t, past_len=past_len, latent=latent)
    if latent:
        ck, cv = cache
        cos2, sin2 = rope
        c_spec = pl.BlockSpec((None, None, past_len, d), lambda i, g: (i, g, 0, 0))
        tab_spec = pl.BlockSpec((t, d), lambda i, g: (0, 0))
        return pl.pallas_call(
            kern,
            out_shape=jax.ShapeDtypeStruct(q.shape, F32),
            grid=(b, KV_HEADS),
            in_specs=[q_spec, kv_spec, kv_spec, c_spec, c_spec, tab_spec, tab_spec,
                      vec_spec, vec_spec],
            out_specs=q_spec,
            scratch_shapes=scratch,
            compiler_params=_params("arbitrary", "arbitrary"),
            name="attn_latent",
        )(q, k, v, ck, cv, cos2, sin2, qg, kg)
    return pl.pallas_call(
        kern,
        out_shape=(jax.ShapeDtypeStruct(q.shape, F32), jax.ShapeDtypeStruct(k.shape, F32)),
        grid=(b, KV_HEADS),
        in_specs=[q_spec, kv_spec, kv_spec, vec_spec, vec_spec],
        out_specs=(q_spec, kv_spec),
        scratch_shapes=scratch,
        compiler_params=_params("arbitrary", "arbitrary"),
        name="attn_context",
    )(q, k, v, qg, kg)


def _conv_rows(ref_rows, taps, start, n, total):
    cur = ref_rows(start, n)
    width = cur.shape[-1]
    zero = jnp.zeros((1, width), F32)
    prev_row = ref_rows(start - 1, 1) if start > 0 else zero
    next_row = ref_rows(start + n, 1) if start + n < total else zero
    ridx = lax.broadcasted_iota(jnp.int32, (n, 1), 0)
    prev = jnp.where(ridx == 0, prev_row, pltpu.roll(cur, 1, 0))
    nxt = jnp.where(ridx == n - 1, next_row, pltpu.roll(cur, n - 1, 0))
    return prev * taps[0] + cur * taps[1] + nxt * taps[2]


def _rwkv_kernel(*refs, seq_len, has_state):
    n_in = 15 if has_state else 14
    (rkv_ref, lora_ref, cwr_ref, cwl_ref, w0_ref, w2_ref, a0_ref, a2_ref, g2_ref,
     kk_ref, ka_ref, rk_ref, lng_ref, lnb_ref) = refs[:14]
    s0_ref = refs[14] if has_state else None
    (y_ref, sfin_ref, rs_ref, ks_ref, vs_ref, ls_ref, yacc_ref, bacc_ref,
     at_ref, rt_ref, bh_ref, kh_ref, akv_ref, rkv2_ref, pc_ref, tm_ref, arb_ref,
     xpad_ref, upad_ref) = refs[n_in:]
    t = seq_len
    n_groups = t // GROUP
    n_chunks = GROUP // CHUNK
    n = R_HEAD_DIM

    for g in range(n_groups):
        start = g * GROUP
        rows = slice(start, start + GROUP)
        for which, dst in ((0, rs_ref), (1, ks_ref), (2, vs_ref)):
            taps = [cwr_ref[j, which] for j in range(3)]
            dst[rows, :] = _conv_rows(lambda s, m, w=which: rkv_ref[w, s:s + m, :],
                                      taps, start, GROUP, t)
        taps = [cwl_ref[j:j + 1, :] for j in range(3)]
        ls_ref[rows, :] = _conv_rows(lambda s, m: lora_ref[s:s + m, :], taps, start, GROUP, t)
    yacc_ref[...] = jnp.zeros((t, n), F32)
    bacc_ref[...] = jnp.zeros((t, n), F32)
    xpad_ref[...] = jnp.zeros((GROUP, n), F32)
    upad_ref[...] = jnp.zeros((GROUP, n), F32)

    ri = lax.broadcasted_iota(jnp.int32, (GROUP, GROUP), 0)
    ci = lax.broadcasted_iota(jnp.int32, (GROUP, GROUP), 1)
    same = (ri // CHUNK) == (ci // CHUNK)
    eye = (ri == ci).astype(F32)
    blk = same.astype(BF16)
    zrows = jnp.zeros((CHUNK, n), F32)

    def group_body(i, state):
        d = i // n_groups
        j = i - d * n_groups
        rev = d == 1
        g = jnp.where(rev, n_groups - 1 - j, j)
        row0 = pl.multiple_of(g * GROUP, GROUP)
        rows = pl.ds(row0, GROUP)
        order = (ci - ri) * jnp.where(rev, -1, 1)
        strict = same & (order < 0)
        incl = same & (order <= 0)
        tri = incl.astype(BF16)
        if has_state:
            state = jnp.where(j == 0, s0_ref[d], state)
        else:
            state = jnp.where(j == 0, 0.0, state)

        r = rs_ref[rows, :]
        k = ks_ref[rows, :]
        v = vs_ref[rows, :]
        lo = ls_ref[rows, :]
        wd_t = jnp.tanh(lo[:, 0:DECAY_RANK])
        ad = lo[:, DECAY_RANK:DECAY_RANK + ICLR_RANK]
        z = w0_ref[d] + _mm(wd_t, w2_ref[d], _NN, P_LORA)
        lw = -DECAY_SCALE * _sigmoid(z)
        a = _sigmoid(a0_ref[d] + _mm(ad, a2_ref[d], _NN, P_LORA))
        kd = k * (1.0 + (a - 1.0) * ka_ref[...])
        kk = k * kk_ref[...]
        kk = kk * lax.rsqrt(jnp.sum(kk * kk, axis=-1, keepdims=True) + 1e-12)
        b = kk * a
        bacc_ref[rows, :] = bacc_ref[rows, :] + (
            jnp.sum(r * kd * rk_ref[...], axis=-1, keepdims=True) * v)
        cum = _mm_exact_lhs(tri, lw)
        cum_c = _mm_exact_lhs(blk, lw)
        a_t = -kk * jnp.exp(cum - lw)
        r_t = r * jnp.exp(cum)
        inv = jnp.exp(-cum)
        b_t = b * inv
        k_t = kd * inv
        tail = jnp.exp(cum_c - cum)
        at_ref[...] = a_t
        rt_ref[...] = r_t
        bh_ref[...] = b * tail
        kh_ref[...] = kd * tail
        pc_ref[...] = jnp.exp(cum_c)
        a_ab = jnp.where(strict, _mm(a_t, b_t, _NT, P_AMAT), 0.0)
        a_ak = jnp.where(strict, _mm(a_t, k_t, _NT, P_AMAT), 0.0)
        arb_ref[...] = jnp.where(incl, _mm(r_t, b_t, _NT, P_AMAT), 0.0)
        a_rk = jnp.where(incl, _mm(r_t, k_t, _NT, P_AMAT), 0.0)
        tm = eye + a_ab
        pw = a_ab
        for _ in range(int(np.log2(CHUNK)) - 1):
            pw = _mm(pw, pw, _NN, P_INV)
            tm = tm + _mm(tm, pw, _NN, P_INV)
        tm_ref[...] = tm
        akv_ref[...] = _mm(a_ak, v, _NN, P_AV)
        rkv2_ref[...] = _mm(a_rk, v, _NN, P_AV)

        for jj in range(n_chunks):
            c = jnp.where(rev, n_chunks - 1 - jj, jj)
            c0 = pl.multiple_of(c * CHUNK, CHUNK)
            cr = pl.ds(c0, CHUNK)
            yr = pl.ds(pl.multiple_of(row0 + c0, CHUNK), CHUNK)
            x = _mm(at_ref[cr, :], state, _NT, P_SERIAL) + akv_ref[cr, :]
            xpad_ref[cr, :] = x
            u = _mm(tm_ref[cr, :], xpad_ref[...], _NN, P_SERIAL)
            xpad_ref[cr, :] = zrows
            upad_ref[cr, :] = u
            yc = (_mm(rt_ref[cr, :], state, _NT, P_SERIAL)
                  + _mm(arb_ref[cr, :], upad_ref[...], _NN, P_SERIAL) + rkv2_ref[cr, :])
            upad_ref[cr, :] = zrows
            yacc_ref[yr, :] = yacc_ref[yr, :] + yc
            state = (state * pc_ref[pl.ds(c0, 1), :]
                     + _mm(u, bh_ref[cr, :], _TN, P_SERIAL)
                     + _mm(vs_ref[yr, :], kh_ref[cr, :], _TN, P_SERIAL))
        sfin_ref[d] = state
        return state

    lax.fori_loop(0, 2 * n_groups, group_body, jnp.zeros((n, n), F32))

    for g in range(n_groups):
        rows = slice(g * GROUP, (g + 1) * GROUP)
        y = yacc_ref[rows, :]
        mu = jnp.mean(y, axis=-1, keepdims=True)
        yc = y - mu
        var = jnp.mean(yc * yc, axis=-1, keepdims=True)
        yn = yc * lax.rsqrt(var + GN_EPS)
        out = yn * lng_ref[...] + lnb_ref[...] + bacc_ref[rows, :]
        gd = ls_ref[rows, DECAY_RANK + ICLR_RANK:LORA_WIDTH]
        gate = _dot(_sigmoid(gd).astype(BF16), g2_ref[...].astype(BF16))
        y_ref[rows, :] = out * gate


def _rwkv(rkv_hm, lora, p, s0):
    n_seq, _, h, t, n = rkv_hm.shape
    has_state = s0 is not None
    hv = lambda s, i: (i, 0, 0)
    in_specs = [
        pl.BlockSpec((None, 3, None, t, n), lambda s, i: (s, 0, i, 0, 0)),
        pl.BlockSpec((None, t, LORA_WIDTH), lambda s, i: (s, 0, 0)),
        pl.BlockSpec((3, 3, None, 1, n), lambda s, i: (0, 0, i, 0, 0)),
        pl.BlockSpec((3, LORA_WIDTH), lambda s, i: (0, 0)),
        pl.BlockSpec((2, None, 1, n), lambda s, i: (0, i, 0, 0)),
        pl.BlockSpec((2, None, DECAY_RANK, n), lambda s, i: (0, i, 0, 0)),
        pl.BlockSpec((2, None, 1, n), lambda s, i: (0, i, 0, 0)),
        pl.BlockSpec((2, None, ICLR_RANK, n), lambda s, i: (0, i, 0, 0)),
        pl.BlockSpec((None, GATE_RANK, n), hv),
        pl.BlockSpec((None, 1, n), hv),
        pl.BlockSpec((None, 1, n), hv),
        pl.BlockSpec((None, 1, n), hv),
        pl.BlockSpec((None, 1, n), hv),
        pl.BlockSpec((None, 1, n), hv),
    ]
    args = [rkv_hm, lora, p["cw_rkv"], p["cw_lora"], p["w0"], p["w2"], p["a0"], p["a2"],
            p["g2"], p["k_k"], p["k_a"], p["r_k"], p["ln_g"], p["ln_b"]]
    state_spec = pl.BlockSpec((None, 2, None, n, n), lambda s, i: (s, 0, i, 0, 0))
    if has_state:
        in_specs.append(state_spec)
        args.append(s0)
    seq_buf = pltpu.VMEM((t, n), F32)
    grp_buf = pltpu.VMEM((GROUP, n), F32)
    grp_sq = pltpu.VMEM((GROUP, GROUP), F32)
    return pl.pallas_call(
        functools.partial(_rwkv_kernel, seq_len=t, has_state=has_state),
        out_shape=(jax.ShapeDtypeStruct((n_seq, h, t, n), F32),
                   jax.ShapeDtypeStruct((n_seq, 2, h, n, n), F32)),
        grid=(n_seq, h),
        in_specs=in_specs,
        out_specs=(pl.BlockSpec((None, None, t, n), lambda s, i: (s, i, 0, 0)), state_spec),
        scratch_shapes=[seq_buf, seq_buf, seq_buf, pltpu.VMEM((t, LORA_WIDTH), F32), seq_buf, seq_buf,
                        grp_buf, grp_buf, grp_buf, grp_buf, grp_buf, grp_buf, grp_buf,
                        grp_sq, grp_sq, grp_buf, grp_buf],
        compiler_params=_params("arbitrary", "arbitrary"),
        name="rwkv7_latent" if has_state else "rwkv7_context",
    )(*args)


def _merge_kernel(x_ref, ao_ref, ro_ref, sga_ref, sgr_ref, mod_ref, g_ref, wab_ref, wrb_ref,
                  wo_ref, x1_ref, h2_ref):
    merged = (sga_ref[...] * _dot(ao_ref[...].astype(BF16), wab_ref[...])
              + sgr_ref[...] * _dot(ro_ref[...].astype(BF16), wrb_ref[...]))
    x1 = x_ref[...] + mod_ref[2:3, :] * _dot(merged.astype(BF16), wo_ref[...])
    x1_ref[...] = x1
    h2 = _rms(x1) * g_ref[...]
    h2_ref[...] = (h2 * (1.0 + mod_ref[4:5, :]) + mod_ref[3:4, :]).astype(BF16)


def _merge(x2d, attn_o, rwkv_o, sga, sgr, mod, norm2_g, wab, wrb, wo, seq_len):
    n_tok = x2d.shape[0]
    tm = PROJ_ROWS
    row = lambda i: (i, 0)
    const = lambda i: (0, 0)
    return pl.pallas_call(
        _merge_kernel,
        out_shape=(jax.ShapeDtypeStruct((n_tok, D_MODEL), F32),
                   jax.ShapeDtypeStruct((n_tok, D_MODEL), BF16)),
        grid=(n_tok // tm,),
        in_specs=[
            pl.BlockSpec((tm, D_MODEL), row),
            pl.BlockSpec((tm, ATTN_WIDTH), row),
            pl.BlockSpec((tm, R_WIDTH), row),
            pl.BlockSpec((tm, D_MODEL), row),
            pl.BlockSpec((tm, D_MODEL), row),
            pl.BlockSpec((None, N_MOD, D_MODEL), lambda i: ((i * tm) // seq_len, 0, 0)),
            pl.BlockSpec((1, D_MODEL), const),
            pl.BlockSpec((ATTN_WIDTH, D_MODEL), const),
            pl.BlockSpec((R_WIDTH, D_MODEL), const),
            pl.BlockSpec((D_MODEL, D_MODEL), const),
        ],
        out_specs=(pl.BlockSpec((tm, D_MODEL), row), pl.BlockSpec((tm, D_MODEL), row)),
        compiler_params=_params("arbitrary"),
        name="merge_out_proj",
    )(x2d, attn_o, rwkv_o, sga, sgr, mod, norm2_g, wab, wrb, wo)


def _ffn_kernel(h2_ref, x1_ref, mod_ref, wu_ref, wg_ref, cu_ref, cg_ref, wd_ref, fg_ref,
                y_ref, acc_ref, *, seq_len):
    j = pl.program_id(1)
    rows = h2_ref.shape[0]
    h2 = h2_ref[...]
    pos = lax.broadcasted_iota(jnp.int32, (rows, 1), 0) % seq_len
    first = pos == 0
    last = pos == seq_len - 1

    def conv(up, c_ref):
        prev = jnp.where(first, 0.0, pltpu.roll(up, 1, 0))
        nxt = jnp.where(last, 0.0, pltpu.roll(up, rows - 1, 0))
        return prev * c_ref[0:1, :] + up * c_ref[1:2, :] + nxt * c_ref[2:3, :]

    up_u = conv(_dot(h2, wu_ref[...]), cu_ref)
    up_g = conv(_dot(h2, wg_ref[...]), cg_ref)
    act = (up_g * _sigmoid(up_g) * up_u).astype(BF16)
    part = _dot(act, wd_ref[...])

    @pl.when(j == 0)
    def _():
        acc_ref[...] = part

    @pl.when(j > 0)
    def _():
        acc_ref[...] = acc_ref[...] + part

    @pl.when(j == pl.num_programs(1) - 1)
    def _():
        x2 = x1_ref[...] + mod_ref[...] * acc_ref[...]
        y_ref[...] = _rms(x2) * fg_ref[...]


def _ffn(h2, x1, gate2, ffn_up, ffn_conv, ffn_down, final_g, seq_len):
    n_tok = h2.shape[0]
    tm = FFN_ROWS
    tn = FFN_TILE
    nt = D_FF // tn
    assert tm % seq_len == 0 or seq_len % tm == 0
    gate_map = lambda i, j: ((i * tm) // seq_len, 0, 0)
    return pl.pallas_call(
        functools.partial(_ffn_kernel, seq_len=seq_len),
        out_shape=jax.ShapeDtypeStruct((n_tok, D_MODEL), F32),
        grid=(n_tok // tm, nt),
        in_specs=[
            pl.BlockSpec((tm, D_MODEL), lambda i, j: (i, 0)),
            pl.BlockSpec((tm, D_MODEL), lambda i, j: (i, 0)),
            pl.BlockSpec((None, 1, D_MODEL), gate_map),
            pl.BlockSpec((D_MODEL, tn), lambda i, j: (0, j)),
            pl.BlockSpec((D_MODEL, tn), lambda i, j: (0, nt + j)),
            pl.BlockSpec((3, tn), lambda i, j: (0, j)),
            pl.BlockSpec((3, tn), lambda i, j: (0, nt + j)),
            pl.BlockSpec((tn, D_MODEL), lambda i, j: (j, 0)),
            pl.BlockSpec((1, D_MODEL), lambda i, j: (0, 0)),
        ],
        out_specs=pl.BlockSpec((tm, D_MODEL), lambda i, j: (i, 0)),
        scratch_shapes=[pltpu.VMEM((tm, D_MODEL), F32)],
        compiler_params=_params("arbitrary", "arbitrary"),
        name="conv_ffn",
    )(h2, x1, gate2, ffn_up, ffn_up, ffn_conv, ffn_conv, ffn_down, final_g)


def _layer_path(x, mod, w, rope, cache, s0):
    b, t, _ = x.shape
    x2d = x.reshape(b * t, D_MODEL)
    qkv, rkv, lora, sga, sgr = _in_projection(x2d, mod, w["norm1_g"], w["w_in"], t)
    q = qkv[:, :ATTN_WIDTH].reshape(b, t, N_HEADS, HEAD_DIM).transpose(0, 2, 1, 3)
    k = qkv[:, ATTN_WIDTH:ATTN_WIDTH + KV_WIDTH].reshape(b, t, KV_HEADS, HEAD_DIM).transpose(0, 2, 1, 3)
    v = qkv[:, ATTN_WIDTH + KV_WIDTH:].reshape(b, t, KV_HEADS, HEAD_DIM).transpose(0, 2, 1, 3)
    if rope is None:
        attn_hm, k_norm = _attention(q, k, v, w["q_norm_g"], w["k_norm_g"])
    else:
        attn_hm = _attention(q, k, v, w["q_norm_g"], w["k_norm_g"], cache=cache, rope=rope)
        k_norm = None
    attn_o = attn_hm.transpose(0, 2, 1, 3).reshape(b * t, ATTN_WIDTH)
    rkv_hm = rkv.reshape(b, t, 3, R_HEADS, R_HEAD_DIM).transpose(0, 2, 3, 1, 4)
    y_hm, s_fin = _rwkv(rkv_hm, lora.reshape(b, t, LORA_WIDTH), w["rwkv"], s0)
    rwkv_o = y_hm.transpose(0, 2, 1, 3).reshape(b * t, R_WIDTH)
    x1, h2 = _merge(x2d, attn_o, rwkv_o, sga, sgr, mod, w["norm2_g"], w["w_attn_br"],
                    w["w_rwkv_br"], w["w_out"], t)
    gate2 = mod[:, 5:6, :]
    y = _ffn(h2, x1, gate2, w["ffn_up"], w["ffn_conv"], w["ffn_down"], w["final_norm_g"], t)
    return y.reshape(b, t, D_MODEL), k_norm, v, s_fin


def _rope_tables(rows):
    row = jnp.repeat(jnp.arange(rows), GRID_W).astype(F32)
    col = jnp.tile(jnp.arange(GRID_W), rows).astype(F32)
    inv = 1.0 / (ROPE_THETA ** (jnp.arange(ROPE_PAIRS, dtype=F32) / ROPE_PAIRS))
    ang = jnp.concatenate([row[:, None] * inv, col[:, None] * inv], axis=-1)
    cos, sin = jnp.cos(ang), jnp.sin(ang)
    return jnp.concatenate([cos, cos], axis=-1), jnp.concatenate([-sin, sin], axis=-1)


def kernel(x_prompt, x_sample, cache_k, cache_v, state_rwkv, c, c_ctx, w_ada, b_ada, norm1_g, w_in, q_norm_g, k_norm_g, rwkv_conv, rwkv_w0, rwkv_w2, rwkv_a0, rwkv_a2, rwkv_g2, rwkv_k_k, rwkv_k_a, rwkv_r_k, rwkv_ln_g, rwkv_ln_b, w_attn_br, w_rwkv_br, w_out, norm2_g, ffn_up, ffn_conv, ffn_down, final_norm_g):
    depth = w_in.shape[0]
    assert depth == 1, "single trunk layer"
    l = 0
    n_ctx = x_prompt.shape[0]
    n_lat = x_sample.shape[0]
    h, n = R_HEADS, R_HEAD_DIM

    cc = jnp.concatenate([c_ctx[None, :], c, jnp.zeros((8 - 1 - n_lat, D_MODEL), F32)], axis=0)
    mod_all = _modulation(cc, w_ada[l], b_ada[l][None, :])
    mod_ctx = jnp.broadcast_to(mod_all[0:1].reshape(1, N_MOD, D_MODEL), (n_ctx, N_MOD, D_MODEL))
    mod_lat = mod_all[1:1 + n_lat].reshape(n_lat, N_MOD, D_MODEL)

    conv = rwkv_conv[l]
    rwkv_p = {
        "cw_rkv": conv[:, :3 * R_WIDTH].reshape(3, 3, h, 1, n),
        "cw_lora": conv[:, 3 * R_WIDTH:],
        "w0": rwkv_w0[l].reshape(2, h, 1, n),
        "w2": rwkv_w2[l].reshape(2, DECAY_RANK, h, n).transpose(0, 2, 1, 3),
        "a0": rwkv_a0[l].reshape(2, h, 1, n),
        "a2": rwkv_a2[l].reshape(2, ICLR_RANK, h, n).transpose(0, 2, 1, 3),
        "g2": rwkv_g2[l].reshape(GATE_RANK, h, n).transpose(1, 0, 2),
        "k_k": rwkv_k_k[l].reshape(h, 1, n),
        "k_a": rwkv_k_a[l].reshape(h, 1, n),
        "r_k": rwkv_r_k[l].reshape(h, 1, n),
        "ln_g": rwkv_ln_g[l].reshape(h, 1, n),
        "ln_b": rwkv_ln_b[l].reshape(h, 1, n),
    }
    w = {
        "norm1_g": norm1_g[l][None, :], "w_in": w_in[l].astype(BF16),
        "q_norm_g": q_norm_g[l][None, :], "k_norm_g": k_norm_g[l][None, :],
        "rwkv": rwkv_p,
        "w_attn_br": w_attn_br[l].astype(BF16), "w_rwkv_br": w_rwkv_br[l].astype(BF16),
        "w_out": w_out[l].astype(BF16), "norm2_g": norm2_g[l][None, :],
        "ffn_up": ffn_up[l].astype(BF16), "ffn_conv": ffn_conv[l],
        "ffn_down": ffn_down[l].astype(BF16), "final_norm_g": final_norm_g[None, :],
    }

    y_prompt, k_ctx, v_ctx, s_ctx = _layer_path(x_prompt, mod_ctx, w, None, None, None)
    rope = _rope_tables(x_sample.shape[1] // GRID_W)
    y_sample, _, _, _ = _layer_path(x_sample, mod_lat, w, rope, (cache_k[:, l], cache_v[:, l]),
                                    state_rwkv[:, l])
    return (y_prompt, y_sample, k_ctx[:, None], v_ctx[:, None], s_ctx[:, None])
```

```python
import functools

import jax
import jax.numpy as jnp
import numpy as np
from jax import lax
from jax.experimental import pallas as pl
from jax.experimental.pallas import tpu as pltpu

D_MODEL = 1024
GRID_W = 64
N_HEADS = 8
KV_HEADS = 2
Q_PER_KV = N_HEADS // KV_HEADS
HEAD_DIM = 64
ATTN_WIDTH = N_HEADS * HEAD_DIM
KV_WIDTH = KV_HEADS * HEAD_DIM
ROPE_THETA = 10000.0
ROPE_PAIRS = HEAD_DIM // 4
R_HEADS = 8
R_HEAD_DIM = 64
R_WIDTH = R_HEADS * R_HEAD_DIM
DECAY_RANK = 64
ICLR_RANK = 64
GATE_RANK = 128
LORA_WIDTH = DECAY_RANK + ICLR_RANK + GATE_RANK
R_IN_WIDTH = 3 * R_WIDTH + LORA_WIDTH
D_FF = 2816
N_MOD = 6
NORM_EPS = 1e-6
GN_EPS = 64e-5
DECAY_SCALE = float(np.exp(-0.5))

F32 = jnp.float32
BF16 = jnp.bfloat16

VMEM_LIMIT_BYTES = 56 * 1024 * 1024
PROJ_ROWS = 256
FFN_ROWS = 1024
FFN_TILE = 256
Q_ROWS = 128
GROUP = 256
CHUNK = 64

_NN = (((1,), (0,)), ((), ()))
_NT = (((1,), (1,)), ((), ()))
_TN = (((0,), (0,)), ((), ()))

P_LORA = 1
P_AMAT = 1
P_INV = 1
P_AV = 1
P_SERIAL = 1


def _dg(a, b, dims):
    return lax.dot_general(a, b, dims, preferred_element_type=F32)


def _dot(a, b):
    return _dg(a, b, _NN)


def _split2(x):
    hi = x.astype(BF16)
    return hi, (x - hi.astype(F32)).astype(BF16)


def _mm(a, b, dims=_NN, passes=1):
    if passes == 1:
        return _dg(a.astype(BF16), b.astype(BF16), dims)
    a_hi, a_lo = _split2(a)
    b_hi, b_lo = _split2(b)
    return _dg(a_hi, b_hi, dims) + (_dg(a_hi, b_lo, dims) + _dg(a_lo, b_hi, dims))


def _mm_exact_lhs(m, x):
    hi = x.astype(BF16)
    r1 = x - hi.astype(F32)
    mid = r1.astype(BF16)
    lo = (r1 - mid.astype(F32)).astype(BF16)
    return _dot(m, hi) + (_dot(m, mid) + _dot(m, lo))


def _sigmoid(x):
    return 1.0 / (1.0 + jnp.exp(-x))


def _rms(x):
    return x * lax.rsqrt(jnp.mean(x * x, axis=-1, keepdims=True) + NORM_EPS)


def _params(*semantics):
    return pltpu.CompilerParams(dimension_semantics=semantics,
                                vmem_limit_bytes=VMEM_LIMIT_BYTES)


def _mod_kernel(c_ref, w_ref, b_ref, o_ref):
    c = c_ref[...]
    s = (c * _sigmoid(c)).astype(BF16)
    o_ref[...] = _dot(s, w_ref[...].astype(BF16)) + b_ref[...]


def _modulation(cc, w_ada, b_ada):
    rows = cc.shape[0]
    n = w_ada.shape[1]
    tile = 1024
    return pl.pallas_call(
        _mod_kernel,
        out_shape=jax.ShapeDtypeStruct((rows, n), F32),
        grid=(n // tile,),
        in_specs=[
            pl.BlockSpec((rows, D_MODEL), lambda j: (0, 0)),
            pl.BlockSpec((D_MODEL, tile), lambda j: (0, j)),
            pl.BlockSpec((1, tile), lambda j: (0, j)),
        ],
        out_specs=pl.BlockSpec((rows, tile), lambda j: (0, j)),
        compiler_params=_params("arbitrary"),
        name="adaln_mod",
    )(cc, w_ada, b_ada)


def _inproj_kernel(x_ref, mod_ref, g_ref, w_ref, qkv_ref, rkv_ref, lora_ref, sga_ref, sgr_ref):
    x = x_ref[...]
    h = _rms(x) * g_ref[...]
    h = h * (1.0 + mod_ref[1:2, :]) + mod_ref[0:1, :]
    hb = h.astype(BF16)
    c0 = ATTN_WIDTH + 2 * KV_WIDTH
    c1 = c0 + 3 * R_WIDTH
    c2 = c1 + LORA_WIDTH
    c3 = c2 + D_MODEL
    qkv_ref[...] = _dot(hb, w_ref[:, 0:c0])
    rkv_ref[...] = _dot(hb, w_ref[:, c0:c1])
    lora_ref[...] = _dot(hb, w_ref[:, c1:c2])
    sga_ref[...] = _sigmoid(_dot(hb, w_ref[:, c2:c3]))
    sgr_ref[...] = _sigmoid(_dot(hb, w_ref[:, c3:c3 + D_MODEL]))


def _in_projection(x2d, mod, norm1_g, w_in_bf16, seq_len):
    n_tok = x2d.shape[0]
    tm = PROJ_ROWS
    in_width = w_in_bf16.shape[1]
    widths = (ATTN_WIDTH + 2 * KV_WIDTH, 3 * R_WIDTH, LORA_WIDTH, D_MODEL, D_MODEL)
    row = lambda i: (i, 0)
    return pl.pallas_call(
        _inproj_kernel,
        out_shape=tuple(jax.ShapeDtypeStruct((n_tok, w), F32) for w in widths),
        grid=(n_tok // tm,),
        in_specs=[
            pl.BlockSpec((tm, D_MODEL), row),
            pl.BlockSpec((None, N_MOD, D_MODEL), lambda i: ((i * tm) // seq_len, 0, 0)),
            pl.BlockSpec((1, D_MODEL), lambda i: (0, 0)),
            pl.BlockSpec((D_MODEL, in_width), lambda i: (0, 0)),
        ],
        out_specs=tuple(pl.BlockSpec((tm, w), row) for w in widths),
        compiler_params=_params("arbitrary"),
        name="in_proj",
    )(x2d, mod, norm1_g, w_in_bf16)


def _rope(x, cos2, sin2):
    half = HEAD_DIM // 2
    partner = jnp.concatenate([x[:, half:], x[:, :half]], axis=-1)
    return x * cos2 + partner * sin2


def _attn_kernel(*refs, seq_len, past_len, latent):
    if latent:
        (q_ref, k_ref, v_ref, ck_ref, cv_ref, cos_ref, sin_ref, qg_ref, kg_ref,
         o_ref, qs_ref, ks_ref, vs_ref) = refs
    else:
        (q_ref, k_ref, v_ref, qg_ref, kg_ref, o_ref, kn_ref, qs_ref, ks_ref, vs_ref) = refs
    t = seq_len
    kn = _rms(k_ref[...]) * kg_ref[...]
    if latent:
        cos2 = cos_ref[...]
        sin2 = sin_ref[...]
        ks_ref[0:t, :] = _rope(kn, cos2, sin2).astype(BF16)
        ks_ref[t:t + past_len, :] = ck_ref[...].astype(BF16)
        vs_ref[0:t, :] = v_ref[...].astype(BF16)
        vs_ref[t:t + past_len, :] = cv_ref[...].astype(BF16)
    else:
        kn_ref[...] = kn
        ks_ref[...] = kn.astype(BF16)
        vs_ref[...] = v_ref[...].astype(BF16)
    for hq in range(Q_PER_KV):
        qn = _rms(q_ref[hq]) * qg_ref[...]
        if latent:
            qn = _rope(qn, cos2, sin2)
        qs_ref[hq] = qn.astype(BF16)
    scale = HEAD_DIM ** -0.5

    def body(i, carry):
        hq = i // (t // Q_ROWS)
        qb = i % (t // Q_ROWS)
        r0 = pl.multiple_of(qb * Q_ROWS, Q_ROWS)
        q = qs_ref[hq, pl.ds(r0, Q_ROWS), :]
        s = _dg(q, ks_ref[...], _NT) * scale
        m = jnp.max(s, axis=-1, keepdims=True)
        p = jnp.exp(s - m)
        l = jnp.sum(p, axis=-1, keepdims=True)
        o = _dot(p.astype(BF16), vs_ref[...])
        o_ref[hq, pl.ds(r0, Q_ROWS), :] = o / l
        return carry

    lax.fori_loop(0, Q_PER_KV * (t // Q_ROWS), body, 0)


def _attention(q, k, v, qg, kg, cache=None, rope=None):
    b, _, t, d = q.shape
    latent = cache is not None
    past_len = cache[0].shape[2] if latent else 0
    total = t + past_len
    kv_spec = pl.BlockSpec((None, None, t, d), lambda i, g: (i, g, 0, 0))
    q_spec = pl.BlockSpec((None, Q_PER_KV, t, d), lambda i, g: (i, g, 0, 0))
    vec_spec = pl.BlockSpec((1, d), lambda i, g: (0, 0))
    scratch = [pltpu.VMEM((Q_PER_KV, t, d), BF16), pltpu.VMEM((total, d), BF16),
               pltpu.VMEM((total, d), BF16)]
    kern = functools.partial(_attn_kernel, seq_len=t, past_len=past_len, latent=latent)
    if latent:
        ck, cv = cache
        cos2, sin2 = rope
        c_spec = pl.BlockSpec((None, None, past_len, d), lambda i, g: (i, g, 0, 0))
        tab_spec = pl.BlockSpec((t, d), lambda i, g: (0, 0))
        return pl.pallas_call(
            kern,
            out_shape=jax.ShapeDtypeStruct(q.shape, F32),
            grid=(b, KV_HEADS),
            in_specs=[q_spec, kv_spec, kv_spec, c_spec, c_spec, tab_spec, tab_spec,
                      vec_spec, vec_spec],
            out_specs=q_spec,
            scratch_shapes=scratch,
            compiler_params=_params("arbitrary", "arbitrary"),
            name="attn_latent",
        )(q, k, v, ck, cv, cos2, sin2, qg, kg)
    return pl.pallas_call(
        kern,
        out_shape=(jax.ShapeDtypeStruct(q.shape, F32), jax.ShapeDtypeStruct(k.shape, F32)),
        grid=(b, KV_HEADS),
        in_specs=[q_spec, kv_spec, kv_spec, vec_spec, vec_spec],
        out_specs=(q_spec, kv_spec),
        scratch_shapes=scratch,
        compiler_params=_params("arbitrary", "arbitrary"),
        name="attn_context",
    )(q, k, v, qg, kg)


def _conv_rows(ref_rows, taps, start, n, total):
    cur = ref_rows(start, n)
    width = cur.shape[-1]
    zero = jnp.zeros((1, width), F32)
    prev_row = ref_rows(start - 1, 1) if start > 0 else zero
    next_row = ref_rows(start + n, 1) if start + n < total else zero
    ridx = lax.broadcasted_iota(jnp.int32, (n, 1), 0)
    prev = jnp.where(ridx == 0, prev_row, pltpu.roll(cur, 1, 0))
    nxt = jnp.where(ridx == n - 1, next_row, pltpu.roll(cur, n - 1, 0))
    return prev * taps[0] + cur * taps[1] + nxt * taps[2]


def _rwkv_kernel(*refs, seq_len, has_state):
    n_in = 15 if has_state else 14
    (rkv_ref, lora_ref, cwr_ref, cwl_ref, w0_ref, w2_ref, a0_ref, a2_ref, g2_ref,
     kk_ref, ka_ref, rk_ref, lng_ref, lnb_ref) = refs[:14]
    s0_ref = refs[14] if has_state else None
    y_ref, sfin_ref, rs_ref, ks_ref, vs_ref, ls_ref, yacc_ref, bacc_ref, st_ref = refs[n_in:]
    t = seq_len
    n_groups = t // GROUP
    n_chunks = GROUP // CHUNK
    n = R_HEAD_DIM
    step = pl.program_id(2)

    @pl.when(step == 0)
    def _():
        for g in range(n_groups):
            start = g * GROUP
            rows = slice(start, start + GROUP)
            for which, dst in ((0, rs_ref), (1, ks_ref), (2, vs_ref)):
                taps = [cwr_ref[j, which] for j in range(3)]
                dst[rows, :] = _conv_rows(lambda s, m, w=which: rkv_ref[w, s:s + m, :],
                                          taps, start, GROUP, t)
            taps = [cwl_ref[j:j + 1, :] for j in range(3)]
            ls_ref[rows, :] = _conv_rows(lambda s, m: lora_ref[s:s + m, :], taps, start, GROUP, t)
        st_ref[...] = s0_ref[...] if has_state else jnp.zeros((2, n, n), F32)

    def scan_group(d, g, state):
        rev = d == 1
        row0 = pl.multiple_of(g * GROUP, GROUP)
        rows = pl.ds(row0, GROUP)
        ri = lax.broadcasted_iota(jnp.int32, (GROUP, GROUP), 0)
        ci = lax.broadcasted_iota(jnp.int32, (GROUP, GROUP), 1)
        same = (ri // CHUNK) == (ci // CHUNK)
        eye = (ri == ci).astype(F32)
        blk = same.astype(BF16)
        eye_c = eye[0:CHUNK, 0:CHUNK]
        strict = same & ((ci > ri) if rev else (ci < ri))
        incl = same & ((ci >= ri) if rev else (ci <= ri))
        tri = incl.astype(BF16)

        r = rs_ref[rows, :]
        k = ks_ref[rows, :]
        v = vs_ref[rows, :]
        lo = ls_ref[rows, :]
        wd_t = jnp.tanh(lo[:, 0:DECAY_RANK])
        ad = lo[:, DECAY_RANK:DECAY_RANK + ICLR_RANK]
        z = w0_ref[d] + _mm(wd_t, w2_ref[d], _NN, P_LORA)
        lw = -DECAY_SCALE * _sigmoid(z)
        yield
        cum = _mm_exact_lhs(tri, lw)
        cum_c = _mm_exact_lhs(blk, lw)
        a = _sigmoid(a0_ref[d] + _mm(ad, a2_ref[d], _NN, P_LORA))
        kd = k * (1.0 + (a - 1.0) * ka_ref[...])
        kk = k * kk_ref[...]
        kk = kk * lax.rsqrt(jnp.sum(kk * kk, axis=-1, keepdims=True) + 1e-12)
        b = kk * a
        bacc_ref[d, rows, :] = jnp.sum(r * kd * rk_ref[...], axis=-1, keepdims=True) * v
        yield
        a_t = -kk * jnp.exp(cum - lw)
        r_t = r * jnp.exp(cum)
        inv = jnp.exp(-cum)
        b_t = b * inv
        k_t = kd * inv
        tail = jnp.exp(cum_c - cum)
        b_h = b * tail
        k_h = kd * tail
        p_c = jnp.exp(cum_c)
        yield
        a_ab = jnp.where(strict, _mm(a_t, b_t, _NT, P_AMAT), 0.0)
        a_ak = jnp.where(strict, _mm(a_t, k_t, _NT, P_AMAT), 0.0)
        yield
        a_rb = jnp.where(incl, _mm(r_t, b_t, _NT, P_AMAT), 0.0)
        a_rk = jnp.where(incl, _mm(r_t, k_t, _NT, P_AMAT), 0.0)
        akv = _mm(a_ak, v, _NN, P_AV)
        tm = eye + a_ab
        pw = a_ab
        for _ in range(int(np.log2(CHUNK)) - 1):
            yield
            pw = _mm(pw, pw, _NN, P_INV)
            tm = tm + _mm(tm, pw, _NN, P_INV)
        yield
        ta = _mm(tm, a_t, _NN, P_AV)
        tk = _mm(tm, akv, _NN, P_AV)
        yield
        r2 = r_t + _mm(a_rb, ta, _NN, P_AV)
        y2 = _mm(a_rk, v, _NN, P_AV) + _mm(a_rb, tk, _NN, P_AV)
        ys = [None] * n_chunks
        for c in (range(n_chunks - 1, -1, -1) if rev else range(n_chunks)):
            yield
            cr = slice(c * CHUNK, (c + 1) * CHUNK)
            m_c = eye_c * p_c[c * CHUNK:c * CHUNK + 1, :] + _mm(ta[cr], b_h[cr], _TN, P_SERIAL)
            g_c = _mm(tk[cr], b_h[cr], _TN, P_SERIAL) + _mm(v[cr], k_h[cr], _TN, P_SERIAL)
            ys[c] = _mm(r2[cr], state, _NT, P_SERIAL) + y2[cr]
            state = _mm(state, m_c, _NN, P_SERIAL) + g_c
        yacc_ref[d, rows, :] = jnp.concatenate(ys, axis=0)
        return state

    chains = [scan_group(0, step, st_ref[0]), scan_group(1, n_groups - 1 - step, st_ref[1])]
    new_state = [None] * len(chains)
    while any(s is None for s in new_state):
        for i, chain in enumerate(chains):
            if new_state[i] is None:
                try:
                    next(chain)
                except StopIteration as done:
                    new_state[i] = done.value
    st_ref[0] = new_state[0]
    st_ref[1] = new_state[1]

    @pl.when(step == n_groups - 1)
    def _():
        sfin_ref[...] = st_ref[...]
        for g in range(n_groups):
            rows = slice(g * GROUP, (g + 1) * GROUP)
            y = yacc_ref[0, rows, :] + yacc_ref[1, rows, :]
            mu = jnp.mean(y, axis=-1, keepdims=True)
            yc = y - mu
            var = jnp.mean(yc * yc, axis=-1, keepdims=True)
            yn = yc * lax.rsqrt(var + GN_EPS)
            out = yn * lng_ref[...] + lnb_ref[...] + (bacc_ref[0, rows, :] + bacc_ref[1, rows, :])
            gd = ls_ref[rows, DECAY_RANK + ICLR_RANK:LORA_WIDTH]
            gate = _dot(_sigmoid(gd).astype(BF16), g2_ref[...].astype(BF16))
            y_ref[rows, :] = out * gate


def _rwkv(rkv_hm, lora, p, s0):
    n_seq, _, h, t, n = rkv_hm.shape
    has_state = s0 is not None
    hv = lambda s, i, j: (i, 0, 0)
    dir_vec = lambda s, i, j: (0, i, 0, 0)
    in_specs = [
        pl.BlockSpec((None, 3, None, t, n), lambda s, i, j: (s, 0, i, 0, 0)),
        pl.BlockSpec((None, t, LORA_WIDTH), lambda s, i, j: (s, 0, 0)),
        pl.BlockSpec((3, 3, None, 1, n), lambda s, i, j: (0, 0, i, 0, 0)),
        pl.BlockSpec((3, LORA_WIDTH), lambda s, i, j: (0, 0)),
        pl.BlockSpec((2, None, 1, n), dir_vec),
        pl.BlockSpec((2, None, DECAY_RANK, n), dir_vec),
        pl.BlockSpec((2, None, 1, n), dir_vec),
        pl.BlockSpec((2, None, ICLR_RANK, n), dir_vec),
        pl.BlockSpec((None, GATE_RANK, n), hv),
        pl.BlockSpec((None, 1, n), hv),
        pl.BlockSpec((None, 1, n), hv),
        pl.BlockSpec((None, 1, n), hv),
        pl.BlockSpec((None, 1, n), hv),
        pl.BlockSpec((None, 1, n), hv),
    ]
    args = [rkv_hm, lora, p["cw_rkv"], p["cw_lora"], p["w0"], p["w2"], p["a0"], p["a2"],
            p["g2"], p["k_k"], p["k_a"], p["r_k"], p["ln_g"], p["ln_b"]]
    state_spec = pl.BlockSpec((None, 2, None, n, n), lambda s, i, j: (s, 0, i, 0, 0))
    if has_state:
        in_specs.append(state_spec)
        args.append(s0)
    seq_buf = pltpu.VMEM((t, n), F32)
    dir_buf = pltpu.VMEM((2, t, n), F32)
    return pl.pallas_call(
        functools.partial(_rwkv_kernel, seq_len=t, has_state=has_state),
        out_shape=(jax.ShapeDtypeStruct((n_seq, h, t, n), F32),
                   jax.ShapeDtypeStruct((n_seq, 2, h, n, n), F32)),
        grid=(n_seq, h, t // GROUP),
        in_specs=in_specs,
        out_specs=(pl.BlockSpec((None, None, t, n), lambda s, i, j: (s, i, 0, 0)), state_spec),
        scratch_shapes=[seq_buf, seq_buf, seq_buf, pltpu.VMEM((t, LORA_WIDTH), F32),
                        dir_buf, dir_buf, pltpu.VMEM((2, n, n), F32)],
        compiler_params=_params("arbitrary", "arbitrary", "arbitrary"),
        name="rwkv7_latent" if has_state else "rwkv7_context",
    )(*args)


def _merge_kernel(x_ref, ao_ref, ro_ref, sga_ref, sgr_ref, mod_ref, g_ref, wab_ref, wrb_ref,
                  wo_ref, x1_ref, h2_ref):
    merged = (sga_ref[...] * _dot(ao_ref[...].astype(BF16), wab_ref[...])
              + sgr_ref[...] * _dot(ro_ref[...].astype(BF16), wrb_ref[...]))
    x1 = x_ref[...] + mod_ref[2:3, :] * _dot(merged.astype(BF16), wo_ref[...])
    x1_ref[...] = x1
    h2 = _rms(x1) * g_ref[...]
    h2_ref[...] = (h2 * (1.0 + mod_ref[4:5, :]) + mod_ref[3:4, :]).astype(BF16)


def _merge(x2d, attn_o, rwkv_o, sga, sgr, mod, norm2_g, wab, wrb, wo, seq_len):
    n_tok = x2d.shape[0]
    tm = PROJ_ROWS
    row = lambda i: (i, 0)
    const = lambda i: (0, 0)
    return pl.pallas_call(
        _merge_kernel,
        out_shape=(jax.ShapeDtypeStruct((n_tok, D_MODEL), F32),
                   jax.ShapeDtypeStruct((n_tok, D_MODEL), BF16)),
        grid=(n_tok // tm,),
        in_specs=[
            pl.BlockSpec((tm, D_MODEL), row),
            pl.BlockSpec((tm, ATTN_WIDTH), row),
            pl.BlockSpec((tm, R_WIDTH), row),
            pl.BlockSpec((tm, D_MODEL), row),
            pl.BlockSpec((tm, D_MODEL), row),
            pl.BlockSpec((None, N_MOD, D_MODEL), lambda i: ((i * tm) // seq_len, 0, 0)),
            pl.BlockSpec((1, D_MODEL), const),
            pl.BlockSpec((ATTN_WIDTH, D_MODEL), const),
            pl.BlockSpec((R_WIDTH, D_MODEL), const),
            pl.BlockSpec((D_MODEL, D_MODEL), const),
        ],
        out_specs=(pl.BlockSpec((tm, D_MODEL), row), pl.BlockSpec((tm, D_MODEL), row)),
        compiler_params=_params("arbitrary"),
        name="merge_out_proj",
    )(x2d, attn_o, rwkv_o, sga, sgr, mod, norm2_g, wab, wrb, wo)


def _ffn_kernel(h2_ref, x1_ref, mod_ref, wu_ref, wg_ref, cu_ref, cg_ref, wd_ref, fg_ref,
                y_ref, acc_ref, *, seq_len):
    j = pl.program_id(1)
    rows = h2_ref.shape[0]
    h2 = h2_ref[...]
    pos = lax.broadcasted_iota(jnp.int32, (rows, 1), 0) % seq_len
    first = pos == 0
    last = pos == seq_len - 1

    def conv(up, c_ref):
        prev = jnp.where(first, 0.0, pltpu.roll(up, 1, 0))
        nxt = jnp.where(last, 0.0, pltpu.roll(up, rows - 1, 0))
        return prev * c_ref[0:1, :] + up * c_ref[1:2, :] + nxt * c_ref[2:3, :]

    up_u = conv(_dot(h2, wu_ref[...]), cu_ref)
    up_g = conv(_dot(h2, wg_ref[...]), cg_ref)
    act = (up_g * _sigmoid(up_g) * up_u).astype(BF16)
    part = _dot(act, wd_ref[...])

    @pl.when(j == 0)
    def _():
        acc_ref[...] = part

    @pl.when(j > 0)
    def _():
        acc_ref[...] = acc_ref[...] + part

    @pl.when(j == pl.num_programs(1) - 1)
    def _():
        x2 = x1_ref[...] + mod_ref[...] * acc_ref[...]
        y_ref[...] = _rms(x2) * fg_ref[...]


def _ffn(h2, x1, gate2, ffn_up, ffn_conv, ffn_down, final_g, seq_len):
    n_tok = h2.shape[0]
    tm = FFN_ROWS
    tn = FFN_TILE
    nt = D_FF // tn
    assert tm % seq_len == 0 or seq_len % tm == 0
    gate_map = lambda i, j: ((i * tm) // seq_len, 0, 0)
    return pl.pallas_call(
        functools.partial(_ffn_kernel, seq_len=seq_len),
        out_shape=jax.ShapeDtypeStruct((n_tok, D_MODEL), F32),
        grid=(n_tok // tm, nt),
        in_specs=[
            pl.BlockSpec((tm, D_MODEL), lambda i, j: (i, 0)),
            pl.BlockSpec((tm, D_MODEL), lambda i, j: (i, 0)),
            pl.BlockSpec((None, 1, D_MODEL), gate_map),
            pl.BlockSpec((D_MODEL, tn), lambda i, j: (0, j)),
            pl.BlockSpec((D_MODEL, tn), lambda i, j: (0, nt + j)),
            pl.BlockSpec((3, tn), lambda i, j: (0, j)),
            pl.BlockSpec((3, tn), lambda i, j: (0, nt + j)),
            pl.BlockSpec((tn, D_MODEL), lambda i, j: (j, 0)),
            pl.BlockSpec((1, D_MODEL), lambda i, j: (0, 0)),
        ],
        out_specs=pl.BlockSpec((tm, D_MODEL), lambda i, j: (i, 0)),
        scratch_shapes=[pltpu.VMEM((tm, D_MODEL), F32)],
        compiler_params=_params("arbitrary", "arbitrary"),
        name="conv_ffn",
    )(h2, x1, gate2, ffn_up, ffn_up, ffn_conv, ffn_conv, ffn_down, final_g)


def _layer_path(x, mod, w, rope, cache, s0):
    b, t, _ = x.shape
    x2d = x.reshape(b * t, D_MODEL)
    qkv, rkv, lora, sga, sgr = _in_projection(x2d, mod, w["norm1_g"], w["w_in"], t)
    q = qkv[:, :ATTN_WIDTH].reshape(b, t, N_HEADS, HEAD_DIM).transpose(0, 2, 1, 3)
    k = qkv[:, ATTN_WIDTH:ATTN_WIDTH + KV_WIDTH].reshape(b, t, KV_HEADS, HEAD_DIM).transpose(0, 2, 1, 3)
    v = qkv[:, ATTN_WIDTH + KV_WIDTH:].reshape(b, t, KV_HEADS, HEAD_DIM).transpose(0, 2, 1, 3)
    if rope is None:
        attn_hm, k_norm = _attention(q, k, v, w["q_norm_g"], w["k_norm_g"])
    else:
        attn_hm = _attention(q, k, v, w["q_norm_g"], w["k_norm_g"], cache=cache, rope=rope)
        k_norm = None
    attn_o = attn_hm.transpose(0, 2, 1, 3).reshape(b * t, ATTN_WIDTH)
    rkv_hm = rkv.reshape(b, t, 3, R_HEADS, R_HEAD_DIM).transpose(0, 2, 3, 1, 4)
    y_hm, s_fin = _rwkv(rkv_hm, lora.reshape(b, t, LORA_WIDTH), w["rwkv"], s0)
    rwkv_o = y_hm.transpose(0, 2, 1, 3).reshape(b * t, R_WIDTH)
    x1, h2 = _merge(x2d, attn_o, rwkv_o, sga, sgr, mod, w["norm2_g"], w["w_attn_br"],
                    w["w_rwkv_br"], w["w_out"], t)
    gate2 = mod[:, 5:6, :]
    y = _ffn(h2, x1, gate2, w["ffn_up"], w["ffn_conv"], w["ffn_down"], w["final_norm_g"], t)
    return y.reshape(b, t, D_MODEL), k_norm, v, s_fin


def _rope_tables(rows):
    row = jnp.repeat(jnp.arange(rows), GRID_W).astype(F32)
    col = jnp.tile(jnp.arange(GRID_W), rows).astype(F32)
    inv = 1.0 / (ROPE_THETA ** (jnp.arange(ROPE_PAIRS, dtype=F32) / ROPE_PAIRS))
    ang = jnp.concatenate([row[:, None] * inv, col[:, None] * inv], axis=-1)
    cos, sin = jnp.cos(ang), jnp.sin(ang)
    return jnp.concatenate([cos, cos], axis=-1), jnp.concatenate([-sin, sin], axis=-1)


def kernel(x_prompt, x_sample, cache_k, cache_v, state_rwkv, c, c_ctx, w_ada, b_ada, norm1_g, w_in, q_norm_g, k_norm_g, rwkv_conv, rwkv_w0, rwkv_w2, rwkv_a0, rwkv_a2, rwkv_g2, rwkv_k_k, rwkv_k_a, rwkv_r_k, rwkv_ln_g, rwkv_ln_b, w_attn_br, w_rwkv_br, w_out, norm2_g, ffn_up, ffn_conv, ffn_down, final_norm_g):
    depth = w_in.shape[0]
    assert depth == 1, "single trunk layer"
    l = 0
    n_ctx = x_prompt.shape[0]
    n_lat = x_sample.shape[0]
    h, n = R_HEADS, R_HEAD_DIM

    cc = jnp.concatenate([c_ctx[None, :], c, jnp.zeros((8 - 1 - n_lat, D_MODEL), F32)], axis=0)
    mod_all = _modulation(cc, w_ada[l], b_ada[l][None, :])
    mod_ctx = jnp.broadcast_to(mod_all[0:1].reshape(1, N_MOD, D_MODEL), (n_ctx, N_MOD, D_MODEL))
    mod_lat = mod_all[1:1 + n_lat].reshape(n_lat, N_MOD, D_MODEL)

    conv = rwkv_conv[l]
    rwkv_p = {
        "cw_rkv": conv[:, :3 * R_WIDTH].reshape(3, 3, h, 1, n),
        "cw_lora": conv[:, 3 * R_WIDTH:],
        "w0": rwkv_w0[l].reshape(2, h, 1, n),
        "w2": rwkv_w2[l].reshape(2, DECAY_RANK, h, n).transpose(0, 2, 1, 3),
        "a0": rwkv_a0[l].reshape(2, h, 1, n),
        "a2": rwkv_a2[l].reshape(2, ICLR_RANK, h, n).transpose(0, 2, 1, 3),
        "g2": rwkv_g2[l].reshape(GATE_RANK, h, n).transpose(1, 0, 2),
        "k_k": rwkv_k_k[l].reshape(h, 1, n),
        "k_a": rwkv_k_a[l].reshape(h, 1, n),
        "r_k": rwkv_r_k[l].reshape(h, 1, n),
        "ln_g": rwkv_ln_g[l].reshape(h, 1, n),
        "ln_b": rwkv_ln_b[l].reshape(h, 1, n),
    }
    w = {
        "norm1_g": norm1_g[l][None, :], "w_in": w_in[l].astype(BF16),
        "q_norm_g": q_norm_g[l][None, :], "k_norm_g": k_norm_g[l][None, :],
        "rwkv": rwkv_p,
        "w_attn_br": w_attn_br[l].astype(BF16), "w_rwkv_br": w_rwkv_br[l].astype(BF16),
        "w_out": w_out[l].astype(BF16), "norm2_g": norm2_g[l][None, :],
        "ffn_up": ffn_up[l].astype(BF16), "ffn_conv": ffn_conv[l],
        "ffn_down": ffn_down[l].astype(BF16), "final_norm_g": final_norm_g[None, :],
    }

    y_prompt, k_ctx, v_ctx, s_ctx = _layer_path(x_prompt, mod_ctx, w, None, None, None)
    rope = _rope_tables(x_sample.shape[1] // GRID_W)
    y_sample, _, _, _ = _layer_path(x_sample, mod_lat, w, rope, (cache_k[:, l], cache_v[:, l]),
                                    state_rwkv[:, l])
    return (y_prompt, y_sample, k_ctx[:, None], v_ctx[:, None], s_ctx[:, None])
```

```python
import functools

import jax
import jax.numpy as jnp
import numpy as np
from jax import lax
from jax.experimental import pallas as pl
from jax.experimental.pallas import tpu as pltpu

D_MODEL = 1024
GRID_W = 64
N_HEADS = 8
KV_HEADS = 2
Q_PER_KV = N_HEADS // KV_HEADS
HEAD_DIM = 64
ATTN_WIDTH = N_HEADS * HEAD_DIM
KV_WIDTH = KV_HEADS * HEAD_DIM
ROPE_THETA = 10000.0
ROPE_PAIRS = HEAD_DIM // 4
R_HEADS = 8
R_HEAD_DIM = 64
R_WIDTH = R_HEADS * R_HEAD_DIM
DECAY_RANK = 64
ICLR_RANK = 64
GATE_RANK = 128
LORA_WIDTH = DECAY_RANK + ICLR_RANK + GATE_RANK
R_IN_WIDTH = 3 * R_WIDTH + LORA_WIDTH
D_FF = 2816
N_MOD = 6
NORM_EPS = 1e-6
GN_EPS = 64e-5
DECAY_SCALE = float(np.exp(-0.5))

F32 = jnp.float32
BF16 = jnp.bfloat16

VMEM_LIMIT_BYTES = 56 * 1024 * 1024
PROJ_ROWS = 256
FFN_ROWS = 1024
FFN_TILE = 256
Q_ROWS = 128
STEP_ROWS = 256
GROUP = 128
CHUNK = 64

_NN = (((1,), (0,)), ((), ()))
_NT = (((1,), (1,)), ((), ()))
_TN = (((0,), (0,)), ((), ()))


def _dg(a, b, dims):
    return lax.dot_general(a, b, dims, preferred_element_type=F32)


def _dot(a, b):
    return _dg(a, b, _NN)


def _mm(a, b, dims=_NN):
    return _dg(a.astype(BF16), b.astype(BF16), dims)


def _mm_exact_lhs(m, x):
    hi = x.astype(BF16)
    r1 = x - hi.astype(F32)
    mid = r1.astype(BF16)
    lo = (r1 - mid.astype(F32)).astype(BF16)
    return _dot(m, hi) + (_dot(m, mid) + _dot(m, lo))


def _sigmoid(x):
    return 1.0 / (1.0 + jnp.exp(-x))


def _rms(x):
    return x * lax.rsqrt(jnp.mean(x * x, axis=-1, keepdims=True) + NORM_EPS)


def _params(*semantics):
    return pltpu.CompilerParams(dimension_semantics=semantics,
                                vmem_limit_bytes=VMEM_LIMIT_BYTES)


def _mod_kernel(c_ref, w_ref, b_ref, o_ref):
    c = c_ref[...]
    s = (c * _sigmoid(c)).astype(BF16)
    o_ref[...] = _dot(s, w_ref[...].astype(BF16)) + b_ref[...]


def _modulation(cc, w_ada, b_ada):
    rows = cc.shape[0]
    n = w_ada.shape[1]
    tile = 1024
    return pl.pallas_call(
        _mod_kernel,
        out_shape=jax.ShapeDtypeStruct((rows, n), F32),
        grid=(n // tile,),
        in_specs=[
            pl.BlockSpec((rows, D_MODEL), lambda j: (0, 0)),
            pl.BlockSpec((D_MODEL, tile), lambda j: (0, j)),
            pl.BlockSpec((1, tile), lambda j: (0, j)),
        ],
        out_specs=pl.BlockSpec((rows, tile), lambda j: (0, j)),
        compiler_params=_params("arbitrary"),
        name="adaln_mod",
    )(cc, w_ada, b_ada)


def _inproj_kernel(x_ref, mod_ref, g_ref, w_ref, qkv_ref, rkv_ref, lora_ref, sga_ref, sgr_ref):
    x = x_ref[...]
    h = _rms(x) * g_ref[...]
    h = h * (1.0 + mod_ref[1:2, :]) + mod_ref[0:1, :]
    hb = h.astype(BF16)
    c0 = ATTN_WIDTH + 2 * KV_WIDTH
    c1 = c0 + 3 * R_WIDTH
    c2 = c1 + LORA_WIDTH
    c3 = c2 + D_MODEL
    qkv_ref[...] = _dot(hb, w_ref[:, 0:c0])
    rkv_ref[...] = _dot(hb, w_ref[:, c0:c1])
    lora_ref[...] = _dot(hb, w_ref[:, c1:c2])
    sga_ref[...] = _sigmoid(_dot(hb, w_ref[:, c2:c3]))
    sgr_ref[...] = _sigmoid(_dot(hb, w_ref[:, c3:c3 + D_MODEL]))


def _in_projection(x2d, mod, norm1_g, w_in_bf16, seq_len):
    n_tok = x2d.shape[0]
    tm = PROJ_ROWS
    in_width = w_in_bf16.shape[1]
    widths = (ATTN_WIDTH + 2 * KV_WIDTH, 3 * R_WIDTH, LORA_WIDTH, D_MODEL, D_MODEL)
    row = lambda i: (i, 0)
    return pl.pallas_call(
        _inproj_kernel,
        out_shape=tuple(jax.ShapeDtypeStruct((n_tok, w), F32) for w in widths),
        grid=(n_tok // tm,),
        in_specs=[
            pl.BlockSpec((tm, D_MODEL), row),
            pl.BlockSpec((None, N_MOD, D_MODEL), lambda i: ((i * tm) // seq_len, 0, 0)),
            pl.BlockSpec((1, D_MODEL), lambda i: (0, 0)),
            pl.BlockSpec((D_MODEL, in_width), lambda i: (0, 0)),
        ],
        out_specs=tuple(pl.BlockSpec((tm, w), row) for w in widths),
        compiler_params=_params("arbitrary"),
        name="in_proj",
    )(x2d, mod, norm1_g, w_in_bf16)


def _rope(x, cos2, sin2):
    half = HEAD_DIM // 2
    partner = jnp.concatenate([x[:, half:], x[:, :half]], axis=-1)
    return x * cos2 + partner * sin2


def _attn_kernel(*refs, seq_len, past_len, latent):
    if latent:
        (q_ref, k_ref, v_ref, ck_ref, cv_ref, cos_ref, sin_ref, qg_ref, kg_ref,
         o_ref, qs_ref, ks_ref, vs_ref) = refs
    else:
        (q_ref, k_ref, v_ref, qg_ref, kg_ref, o_ref, kn_ref, vn_ref, qs_ref, ks_ref, vs_ref) = refs
    t = seq_len
    d = HEAD_DIM
    group = pl.program_id(1)
    k = jnp.where(group == 0, k_ref[:, 0:d], k_ref[:, d:2 * d])
    v = jnp.where(group == 0, v_ref[:, 0:d], v_ref[:, d:2 * d])
    kn = _rms(k) * kg_ref[...]
    if latent:
        cos2 = cos_ref[...]
        sin2 = sin_ref[...]
        ks_ref[0:t, :] = _rope(kn, cos2, sin2).astype(BF16)
        ks_ref[t:t + past_len, :] = ck_ref[...].astype(BF16)
        vs_ref[0:t, :] = v.astype(BF16)
        vs_ref[t:t + past_len, :] = cv_ref[...].astype(BF16)
    else:
        kn_ref[...] = kn
        vn_ref[...] = v
        ks_ref[...] = kn.astype(BF16)
        vs_ref[...] = v.astype(BF16)
    for hq in range(Q_PER_KV):
        qn = _rms(q_ref[:, hq * d:(hq + 1) * d]) * qg_ref[...]
        if latent:
            qn = _rope(qn, cos2, sin2)
        qs_ref[hq] = qn.astype(BF16)
    scale = HEAD_DIM ** -0.5

    def body(qb, carry):
        r0 = pl.multiple_of(qb * Q_ROWS, Q_ROWS)
        outs = []
        for hq in range(Q_PER_KV):
            q = qs_ref[hq, pl.ds(r0, Q_ROWS), :]
            s = _dg(q, ks_ref[...], _NT) * scale
            m = jnp.max(s, axis=-1, keepdims=True)
            p = jnp.exp(s - m)
            l = jnp.sum(p, axis=-1, keepdims=True)
            outs.append(_dot(p.astype(BF16), vs_ref[...]) / l)
        o_ref[pl.ds(r0, Q_ROWS), :] = jnp.concatenate(outs, axis=-1)
        return carry

    lax.fori_loop(0, t // Q_ROWS, body, 0)


def _attention(qkv, qg, kg, cache=None, rope=None):
    b, t, _ = qkv.shape
    d = HEAD_DIM
    latent = cache is not None
    past_len = cache[0].shape[2] if latent else 0
    total = t + past_len
    gw = Q_PER_KV * d
    q_spec = pl.BlockSpec((None, t, gw), lambda i, g: (i, 0, g))
    k_spec = pl.BlockSpec((None, t, KV_WIDTH), lambda i, g: (i, 0, ATTN_WIDTH // KV_WIDTH))
    v_spec = pl.BlockSpec((None, t, KV_WIDTH), lambda i, g: (i, 0, ATTN_WIDTH // KV_WIDTH + 1))
    hm_spec = pl.BlockSpec((None, None, t, d), lambda i, g: (i, g, 0, 0))
    vec_spec = pl.BlockSpec((1, d), lambda i, g: (0, 0))
    out_shape = jax.ShapeDtypeStruct((b, t, ATTN_WIDTH), F32)
    scratch = [pltpu.VMEM((Q_PER_KV, t, d), BF16), pltpu.VMEM((total, d), BF16),
               pltpu.VMEM((total, d), BF16)]
    kern = functools.partial(_attn_kernel, seq_len=t, past_len=past_len, latent=latent)
    if latent:
        ck, cv = cache
        cos2, sin2 = rope
        c_spec = pl.BlockSpec((None, None, past_len, d), lambda i, g: (i, g, 0, 0))
        tab_spec = pl.BlockSpec((t, d), lambda i, g: (0, 0))
        return pl.pallas_call(
            kern,
            out_shape=out_shape,
            grid=(b, KV_HEADS),
            in_specs=[q_spec, k_spec, v_spec, c_spec, c_spec, tab_spec, tab_spec,
                      vec_spec, vec_spec],
            out_specs=q_spec,
            scratch_shapes=scratch,
            compiler_params=_params("arbitrary", "arbitrary"),
            name="attn_latent",
        )(qkv, qkv, qkv, ck, cv, cos2, sin2, qg, kg)
    hm_shape = jax.ShapeDtypeStruct((b, KV_HEADS, t, d), F32)
    return pl.pallas_call(
        kern,
        out_shape=(out_shape, hm_shape, hm_shape),
        grid=(b, KV_HEADS),
        in_specs=[q_spec, k_spec, v_spec, vec_spec, vec_spec],
        out_specs=(q_spec, hm_spec, hm_spec),
        scratch_shapes=scratch,
        compiler_params=_params("arbitrary", "arbitrary"),
        name="attn_context",
    )(qkv, qkv, qkv, qg, kg)


def _conv_rows(ref_rows, taps, start, n, total):
    cur = ref_rows(start, n)
    width = cur.shape[-1]
    zero = jnp.zeros((1, width), F32)
    prev_row = ref_rows(start - 1, 1) if start > 0 else zero
    next_row = ref_rows(start + n, 1) if start + n < total else zero
    ridx = lax.broadcasted_iota(jnp.int32, (n, 1), 0)
    prev = jnp.where(ridx == 0, prev_row, pltpu.roll(cur, 1, 0))
    nxt = jnp.where(ridx == n - 1, next_row, pltpu.roll(cur, n - 1, 0))
    return prev * taps[0] + cur * taps[1] + nxt * taps[2]


def _round_robin(chains):
    results = [None] * len(chains)
    live = list(range(len(chains)))
    while live:
        for i in list(live):
            try:
                next(chains[i])
            except StopIteration as done:
                results[i] = done.value
                live.remove(i)
    return results


def _rwkv_kernel(*refs, seq_len, has_state):
    n_in = 19 if has_state else 18
    (r_ref, k_ref, v_ref, lora_ref, cr_ref, ck_ref, cv_ref, cl_ref, w0_ref, w2_ref, a0_ref, a2_ref,
     g2_ref, kk_ref, ka_ref, rk_ref, lng_ref, lnb_ref) = refs[:18]
    s0_ref = refs[18] if has_state else None
    y_ref, sfin_ref, rs_ref, ks_ref, vs_ref, ls_ref, yacc_ref, bacc_ref, st_ref = refs[n_in:]
    t = seq_len
    n_steps = t // STEP_ROWS
    n_sub = STEP_ROWS // GROUP
    n_chunks = GROUP // CHUNK
    width = 2 * R_HEAD_DIM
    step = pl.program_id(2)

    @pl.when(step == 0)
    def _():
        for blk in range(n_steps):
            start = blk * STEP_ROWS
            rows = slice(start, start + STEP_ROWS)
            for src, cw, dst in ((r_ref, cr_ref, rs_ref), (k_ref, ck_ref, ks_ref),
                                 (v_ref, cv_ref, vs_ref), (lora_ref, cl_ref, ls_ref)):
                taps = [cw[j:j + 1, :] for j in range(3)]
                dst[rows, :] = _conv_rows(lambda s, m, src=src: src[s:s + m, :],
                                          taps, start, STEP_ROWS, t)
        st_ref[...] = s0_ref[...] if has_state else jnp.zeros((2, width, width), F32)

    lane = lax.broadcasted_iota(jnp.int32, (1, width), 1)
    head0 = lane < R_HEAD_DIM

    def per_head(x):
        x0 = jnp.where(head0, x, 0.0)
        return (x0, x - x0)

    def pick(x0, x1):
        return jnp.where(head0, x0, x1)

    def head_sum(x):
        s0 = jnp.sum(jnp.where(head0, x, 0.0), axis=-1, keepdims=True)
        s1 = jnp.sum(jnp.where(head0, 0.0, x), axis=-1, keepdims=True)
        return jnp.where(head0, s0, s1)

    def scan_rows(d, blk, state):
        rev = d == 1
        base = blk * STEP_ROWS
        ri = lax.broadcasted_iota(jnp.int32, (GROUP, GROUP), 0)
        ci = lax.broadcasted_iota(jnp.int32, (GROUP, GROUP), 1)
        same = (ri // CHUNK) == (ci // CHUNK)
        eye = (ri == ci).astype(F32)
        blk_ones = same.astype(BF16)
        strict = same & ((ci > ri) if rev else (ci < ri))
        incl = same & ((ci >= ri) if rev else (ci <= ri))
        tri = incl.astype(BF16)

        def half_mask(m):
            return ((ri // (2 * m)) == (ci // (2 * m))) & ((ri // m) != (ci // m))

        subs = []
        for s in (range(n_sub - 1, -1, -1) if rev else range(n_sub)):
            rows = pl.ds(pl.multiple_of(base + s * GROUP, GROUP), GROUP)
            lo = ls_ref[rows, :]
            wd_t = jnp.tanh(lo[:, 0:DECAY_RANK])
            z = w0_ref[d:d + 1, :] + _mm(wd_t, w2_ref[d])
            subs.append({"rows": rows, "r": rs_ref[rows, :], "k": ks_ref[rows, :],
                         "v": vs_ref[rows, :], "ad": lo[:, DECAY_RANK:DECAY_RANK + ICLR_RANK],
                         "lw": -DECAY_SCALE * _sigmoid(z)})
        yield
        for p in subs:
            p["cum"] = _mm_exact_lhs(tri, p["lw"])
            p["cum_c"] = _mm_exact_lhs(blk_ones, p["lw"])
            a = _sigmoid(a0_ref[d:d + 1, :] + _mm(p["ad"], a2_ref[d]))
            p["kd"] = p["k"] * (1.0 + (a - 1.0) * ka_ref[...])
            kk = p["k"] * kk_ref[...]
            p["kk"] = kk * lax.rsqrt(head_sum(kk * kk) + 1e-12)
            p["b"] = p["kk"] * a
            bacc_ref[d, p["rows"], :] = head_sum(p["r"] * p["kd"] * rk_ref[...]) * p["v"]
        yield
        for p in subs:
            cum, cum_c = p["cum"], p["cum_c"]
            p["a_t"] = -p["kk"] * jnp.exp(cum - p["lw"])
            p["r_t"] = p["r"] * jnp.exp(cum)
            inv = jnp.exp(-cum)
            p["b_t"] = p["b"] * inv
            p["k_t"] = p["kd"] * inv
            tail = jnp.exp(cum_c - cum)
            p["b_h"] = p["b"] * tail
            p["k_h"] = p["kd"] * tail
            p["p_c"] = jnp.exp(cum_c)
        yield
        for p in subs:
            a_th = per_head(p["a_t"])
            p["a_ab"] = [jnp.where(strict, _mm(x, p["b_t"], _NT), 0.0) for x in a_th]
            p["a_ak"] = [jnp.where(strict, _mm(x, p["k_t"], _NT), 0.0) for x in a_th]
            r_th = per_head(p["r_t"])
            p["a_rb"] = [jnp.where(incl, _mm(x, p["b_t"], _NT), 0.0) for x in r_th]
            p["a_rk"] = [jnp.where(incl, _mm(x, p["k_t"], _NT), 0.0) for x in r_th]
        yield
        for p in subs:
            p["akv"] = pick(*[_mm(m, p["v"]) for m in p["a_ak"]])
            p["rkv"] = pick(*[_mm(m, p["v"]) for m in p["a_rk"]])
            p["tm"] = [eye + jnp.where(half_mask(1), m, 0.0) for m in p["a_ab"]]
        m_size = 2
        while m_size < CHUNK:
            yield
            off = half_mask(m_size)
            for p in subs:
                p["ot"] = [_mm(jnp.where(off, n_mat, 0.0), tm)
                           for n_mat, tm in zip(p["a_ab"], p["tm"])]
            yield
            for p in subs:
                p["tm"] = [tm + _mm(tm, ot) for tm, ot in zip(p["tm"], p["ot"])]
            m_size *= 2
        yield
        for p in subs:
            p["ta"] = pick(*[_mm(tm, p["a_t"]) for tm in p["tm"]])
            p["tk"] = pick(*[_mm(tm, p["akv"]) for tm in p["tm"]])
        yield
        cr = [slice(c * CHUNK, (c + 1) * CHUNK) for c in range(n_chunks)]
        for p in subs:
            p["r2"] = p["r_t"] + pick(*[_mm(m, p["ta"]) for m in p["a_rb"]])
            p["y2"] = p["rkv"] + pick(*[_mm(m, p["tk"]) for m in p["a_rb"]])
            p["m_t"] = [eye * p["p_c"][c * CHUNK:c * CHUNK + 1, :]
                        + jnp.where(same, _mm(p["b_h"][cr[c]], p["ta"][cr[c]], _TN), 0.0)
                        for c in range(n_chunks)]
            p["g_t"] = [jnp.where(same, _mm(p["b_h"][cr[c]], p["tk"][cr[c]], _TN)
                                  + _mm(p["k_h"][cr[c]], p["v"][cr[c]], _TN), 0.0)
                        for c in range(n_chunks)]
        for p in subs:
            ys = [None] * n_chunks
            for c in (range(n_chunks - 1, -1, -1) if rev else range(n_chunks)):
                yield
                ys[c] = _mm(p["r2"][cr[c]], state) + p["y2"][cr[c]]
                state = _mm(p["m_t"][c], state) + p["g_t"][c]
            yacc_ref[d, p["rows"], :] = jnp.concatenate(ys, axis=0)
        return state

    new_state = _round_robin([scan_rows(0, step, st_ref[0]),
                              scan_rows(1, n_steps - 1 - step, st_ref[1])])
    st_ref[0] = new_state[0]
    st_ref[1] = new_state[1]

    @pl.when(step == n_steps - 1)
    def _():
        sfin_ref[...] = st_ref[...]
        inv_n = 1.0 / R_HEAD_DIM
        for blk in range(n_steps):
            rows = slice(blk * STEP_ROWS, (blk + 1) * STEP_ROWS)
            y = yacc_ref[0, rows, :] + yacc_ref[1, rows, :]
            yc = y - head_sum(y) * inv_n
            yn = yc * lax.rsqrt(head_sum(yc * yc) * inv_n + GN_EPS)
            out = yn * lng_ref[...] + lnb_ref[...] + (bacc_ref[0, rows, :] + bacc_ref[1, rows, :])
            gd = ls_ref[rows, DECAY_RANK + ICLR_RANK:LORA_WIDTH]
            gate = _dot(_sigmoid(gd).astype(BF16), g2_ref[...].astype(BF16))
            y_ref[rows, :] = out * gate


def _pair_state(s):
    n_seq = s.shape[0]
    n = R_HEAD_DIM
    h = jnp.swapaxes(s, -1, -2).reshape(n_seq, 2, R_HEADS // 2, 2, n, n)
    bd = jnp.einsum("sdphkv,hg->sdphkgv", h, jnp.eye(2, dtype=s.dtype))
    return bd.reshape(n_seq, 2, R_HEADS // 2, 2 * n, 2 * n)


def _unpair_state(bd):
    n_seq = bd.shape[0]
    n = R_HEAD_DIM
    h = jnp.einsum("sdphkhv->sdphkv", bd.reshape(n_seq, 2, R_HEADS // 2, 2, n, 2, n))
    return jnp.swapaxes(h.reshape(n_seq, 2, R_HEADS, n, n), -1, -2)


def _rwkv(rkv, lora, p, s0):
    n_seq, t, _ = rkv.shape
    has_state = s0 is not None
    width = 2 * R_HEAD_DIM
    n_pairs = R_HEADS // 2
    blocks = R_WIDTH // width
    col = lambda off: (lambda s, i, j: (s, 0, off + i))
    vec = lambda off: (lambda s, i, j: (0, off + i))
    dir_mat = lambda s, i, j: (0, 0, i)
    in_specs = [
        pl.BlockSpec((None, t, width), col(0)),
        pl.BlockSpec((None, t, width), col(blocks)),
        pl.BlockSpec((None, t, width), col(2 * blocks)),
        pl.BlockSpec((None, t, LORA_WIDTH), lambda s, i, j: (s, 0, 0)),
        pl.BlockSpec((3, width), vec(0)),
        pl.BlockSpec((3, width), vec(blocks)),
        pl.BlockSpec((3, width), vec(2 * blocks)),
        pl.BlockSpec((3, LORA_WIDTH), lambda s, i, j: (0, 3 * R_WIDTH // LORA_WIDTH)),
        pl.BlockSpec((2, width), vec(0)),
        pl.BlockSpec((2, DECAY_RANK, width), dir_mat),
        pl.BlockSpec((2, width), vec(0)),
        pl.BlockSpec((2, ICLR_RANK, width), dir_mat),
        pl.BlockSpec((GATE_RANK, width), vec(0)),
        pl.BlockSpec((1, width), vec(0)),
        pl.BlockSpec((1, width), vec(0)),
        pl.BlockSpec((1, width), vec(0)),
        pl.BlockSpec((1, width), vec(0)),
        pl.BlockSpec((1, width), vec(0)),
    ]
    conv = p["conv"]
    args = [rkv, rkv, rkv, lora, conv, conv, conv, conv, p["w0"], p["w2"], p["a0"], p["a2"],
            p["g2"], p["k_k"], p["k_a"], p["r_k"], p["ln_g"], p["ln_b"]]
    state_spec = pl.BlockSpec((None, 2, None, width, width), lambda s, i, j: (s, 0, i, 0, 0))
    if has_state:
        in_specs.append(state_spec)
        args.append(_pair_state(s0))
    seq_buf = pltpu.VMEM((t, width), F32)
    dir_buf = pltpu.VMEM((2, t, width), F32)
    y, s_fin = pl.pallas_call(
        functools.partial(_rwkv_kernel, seq_len=t, has_state=has_state),
        out_shape=(jax.ShapeDtypeStruct((n_seq, t, R_WIDTH), F32),
                   jax.ShapeDtypeStruct((n_seq, 2, n_pairs, width, width), F32)),
        grid=(n_seq, n_pairs, t // STEP_ROWS),
        in_specs=in_specs,
        out_specs=(pl.BlockSpec((None, t, width), col(0)), state_spec),
        scratch_shapes=[seq_buf, seq_buf, seq_buf, pltpu.VMEM((t, LORA_WIDTH), F32),
                        dir_buf, dir_buf, pltpu.VMEM((2, width, width), F32)],
        compiler_params=_params("arbitrary", "arbitrary", "arbitrary"),
        name="rwkv7_latent" if has_state else "rwkv7_context",
    )(*args)
    return y, _unpair_state(s_fin)


def _merge_kernel(x_ref, ao_ref, ro_ref, sga_ref, sgr_ref, mod_ref, g_ref, wab_ref, wrb_ref,
                  wo_ref, x1_ref, h2_ref):
    merged = (sga_ref[...] * _dot(ao_ref[...].astype(BF16), wab_ref[...])
              + sgr_ref[...] * _dot(ro_ref[...].astype(BF16), wrb_ref[...]))
    x1 = x_ref[...] + mod_ref[2:3, :] * _dot(merged.astype(BF16), wo_ref[...])
    x1_ref[...] = x1
    h2 = _rms(x1) * g_ref[...]
    h2_ref[...] = (h2 * (1.0 + mod_ref[4:5, :]) + mod_ref[3:4, :]).astype(BF16)


def _merge(x2d, attn_o, rwkv_o, sga, sgr, mod, norm2_g, wab, wrb, wo, seq_len):
    n_tok = x2d.shape[0]
    tm = PROJ_ROWS
    row = lambda i: (i, 0)
    const = lambda i: (0, 0)
    return pl.pallas_call(
        _merge_kernel,
        out_shape=(jax.ShapeDtypeStruct((n_tok, D_MODEL), F32),
                   jax.ShapeDtypeStruct((n_tok, D_MODEL), BF16)),
        grid=(n_tok // tm,),
        in_specs=[
            pl.BlockSpec((tm, D_MODEL), row),
            pl.BlockSpec((tm, ATTN_WIDTH), row),
            pl.BlockSpec((tm, R_WIDTH), row),
            pl.BlockSpec((tm, D_MODEL), row),
            pl.BlockSpec((tm, D_MODEL), row),
            pl.BlockSpec((None, N_MOD, D_MODEL), lambda i: ((i * tm) // seq_len, 0, 0)),
            pl.BlockSpec((1, D_MODEL), const),
            pl.BlockSpec((ATTN_WIDTH, D_MODEL), const),
            pl.BlockSpec((R_WIDTH, D_MODEL), const),
            pl.BlockSpec((D_MODEL, D_MODEL), const),
        ],
        out_specs=(pl.BlockSpec((tm, D_MODEL), row), pl.BlockSpec((tm, D_MODEL), row)),
        compiler_params=_params("arbitrary"),
        name="merge_out_proj",
    )(x2d, attn_o, rwkv_o, sga, sgr, mod, norm2_g, wab, wrb, wo)


def _ffn_kernel(h2_ref, x1_ref, mod_ref, wu_ref, wg_ref, cu_ref, cg_ref, wd_ref, fg_ref,
                y_ref, acc_ref, *, seq_len):
    j = pl.program_id(1)
    rows = h2_ref.shape[0]
    h2 = h2_ref[...]
    pos = lax.broadcasted_iota(jnp.int32, (rows, 1), 0) % seq_len
    first = pos == 0
    last = pos == seq_len - 1

    def conv(up, c_ref):
        prev = jnp.where(first, 0.0, pltpu.roll(up, 1, 0))
        nxt = jnp.where(last, 0.0, pltpu.roll(up, rows - 1, 0))
        return prev * c_ref[0:1, :] + up * c_ref[1:2, :] + nxt * c_ref[2:3, :]

    up_u = conv(_dot(h2, wu_ref[...]), cu_ref)
    up_g = conv(_dot(h2, wg_ref[...]), cg_ref)
    act = (up_g * _sigmoid(up_g) * up_u).astype(BF16)
    part = _dot(act, wd_ref[...])

    @pl.when(j == 0)
    def _():
        acc_ref[...] = part

    @pl.when(j > 0)
    def _():
        acc_ref[...] = acc_ref[...] + part

    @pl.when(j == pl.num_programs(1) - 1)
    def _():
        x2 = x1_ref[...] + mod_ref[...] * acc_ref[...]
        y_ref[...] = _rms(x2) * fg_ref[...]


def _ffn(h2, x1, gate2, ffn_up, ffn_conv, ffn_down, final_g, seq_len):
    n_tok = h2.shape[0]
    tm = FFN_ROWS
    tn = FFN_TILE
    nt = D_FF // tn
    assert tm % seq_len == 0 or seq_len % tm == 0
    gate_map = lambda i, j: ((i * tm) // seq_len, 0, 0)
    return pl.pallas_call(
        functools.partial(_ffn_kernel, seq_len=seq_len),
        out_shape=jax.ShapeDtypeStruct((n_tok, D_MODEL), F32),
        grid=(n_tok // tm, nt),
        in_specs=[
            pl.BlockSpec((tm, D_MODEL), lambda i, j: (i, 0)),
            pl.BlockSpec((tm, D_MODEL), lambda i, j: (i, 0)),
            pl.BlockSpec((None, 1, D_MODEL), gate_map),
            pl.BlockSpec((D_MODEL, tn), lambda i, j: (0, j)),
            pl.BlockSpec((D_MODEL, tn), lambda i, j: (0, nt + j)),
            pl.BlockSpec((3, tn), lambda i, j: (0, j)),
            pl.BlockSpec((3, tn), lambda i, j: (0, nt + j)),
            pl.BlockSpec((tn, D_MODEL), lambda i, j: (j, 0)),
            pl.BlockSpec((1, D_MODEL), lambda i, j: (0, 0)),
        ],
        out_specs=pl.BlockSpec((tm, D_MODEL), lambda i, j: (i, 0)),
        scratch_shapes=[pltpu.VMEM((tm, D_MODEL), F32)],
        compiler_params=_params("arbitrary", "arbitrary"),
        name="conv_ffn",
    )(h2, x1, gate2, ffn_up, ffn_up, ffn_conv, ffn_conv, ffn_down, final_g)


def _layer_path(x, mod, w, rope, cache, s0):
    b, t, _ = x.shape
    x2d = x.reshape(b * t, D_MODEL)
    qkv, rkv, lora, sga, sgr = _in_projection(x2d, mod, w["norm1_g"], w["w_in"], t)
    qkv = qkv.reshape(b, t, ATTN_WIDTH + 2 * KV_WIDTH)
    if rope is None:
        attn_o, k_norm, v_hm = _attention(qkv, w["q_norm_g"], w["k_norm_g"])
    else:
        attn_o = _attention(qkv, w["q_norm_g"], w["k_norm_g"], cache=cache, rope=rope)
        k_norm = v_hm = None
    rwkv_o, s_fin = _rwkv(rkv.reshape(b, t, 3 * R_WIDTH), lora.reshape(b, t, LORA_WIDTH),
                          w["rwkv"], s0)
    x1, h2 = _merge(x2d, attn_o.reshape(b * t, ATTN_WIDTH), rwkv_o.reshape(b * t, R_WIDTH),
                    sga, sgr, mod, w["norm2_g"], w["w_attn_br"], w["w_rwkv_br"], w["w_out"], t)
    gate2 = mod[:, 5:6, :]
    y = _ffn(h2, x1, gate2, w["ffn_up"], w["ffn_conv"], w["ffn_down"], w["final_norm_g"], t)
    return y.reshape(b, t, D_MODEL), k_norm, v_hm, s_fin


def _rope_tables(rows):
    row = jnp.repeat(jnp.arange(rows), GRID_W).astype(F32)
    col = jnp.tile(jnp.arange(GRID_W), rows).astype(F32)
    inv = 1.0 / (ROPE_THETA ** (jnp.arange(ROPE_PAIRS, dtype=F32) / ROPE_PAIRS))
    ang = jnp.concatenate([row[:, None] * inv, col[:, None] * inv], axis=-1)
    cos, sin = jnp.cos(ang), jnp.sin(ang)
    return jnp.concatenate([cos, cos], axis=-1), jnp.concatenate([-sin, sin], axis=-1)


def kernel(x_prompt, x_sample, cache_k, cache_v, state_rwkv, c, c_ctx, w_ada, b_ada, norm1_g, w_in, q_norm_g, k_norm_g, rwkv_conv, rwkv_w0, rwkv_w2, rwkv_a0, rwkv_a2, rwkv_g2, rwkv_k_k, rwkv_k_a, rwkv_r_k, rwkv_ln_g, rwkv_ln_b, w_attn_br, w_rwkv_br, w_out, norm2_g, ffn_up, ffn_conv, ffn_down, final_norm_g):
    depth = w_in.shape[0]
    assert depth == 1, "single trunk layer"
    l = 0
    n_ctx = x_prompt.shape[0]
    n_lat = x_sample.shape[0]

    cc = jnp.concatenate([c_ctx[None, :], c, jnp.zeros((8 - 1 - n_lat, D_MODEL), F32)], axis=0)
    mod_all = _modulation(cc, w_ada[l], b_ada[l][None, :])
    mod_ctx = jnp.broadcast_to(mod_all[0:1].reshape(1, N_MOD, D_MODEL), (n_ctx, N_MOD, D_MODEL))
    mod_lat = mod_all[1:1 + n_lat].reshape(n_lat, N_MOD, D_MODEL)

    rwkv_p = {
        "conv": rwkv_conv[l], "w0": rwkv_w0[l], "w2": rwkv_w2[l], "a0": rwkv_a0[l],
        "a2": rwkv_a2[l], "g2": rwkv_g2[l], "k_k": rwkv_k_k[l][None, :],
        "k_a": rwkv_k_a[l][None, :], "r_k": rwkv_r_k[l].reshape(1, R_WIDTH),
        "ln_g": rwkv_ln_g[l][None, :], "ln_b": rwkv_ln_b[l][None, :],
    }
    w = {
        "norm1_g": norm1_g[l][None, :], "w_in": w_in[l].astype(BF16),
        "q_norm_g": q_norm_g[l][None, :], "k_norm_g": k_norm_g[l][None, :],
        "rwkv": rwkv_p,
        "w_attn_br": w_attn_br[l].astype(BF16), "w_rwkv_br": w_rwkv_br[l].astype(BF16),
        "w_out": w_out[l].astype(BF16), "norm2_g": norm2_g[l][None, :],
        "ffn_up": ffn_up[l].astype(BF16), "ffn_conv": ffn_conv[l],
        "ffn_down": ffn_down[l].astype(BF16), "final_norm_g": final_norm_g[None, :],
    }

    y_prompt, k_ctx, v_ctx, s_ctx = _layer_path(x_prompt, mod_ctx, w, None, None, None)
    rope = _rope_tables(x_sample.shape[1] // GRID_W)
    y_sample, _, _, _ = _layer_path(x_sample, mod_lat, w, rope, (cache_k[:, l], cache_v[:, l]),
                                    state_rwkv[:, l])
    return (y_prompt, y_sample, k_ctx[:, None], v_ctx[:, None], s_ctx[:, None])
```

```python
import functools

import jax
import jax.numpy as jnp
import numpy as np
from jax import lax
from jax.experimental import pallas as pl
from jax.experimental.pallas import tpu as pltpu

D_MODEL = 1024
GRID_W = 64
N_HEADS = 8
KV_HEADS = 2
Q_PER_KV = N_HEADS // KV_HEADS
HEAD_DIM = 64
ATTN_WIDTH = N_HEADS * HEAD_DIM
KV_WIDTH = KV_HEADS * HEAD_DIM
ROPE_THETA = 10000.0
ROPE_PAIRS = HEAD_DIM // 4
R_HEADS = 8
R_HEAD_DIM = 64
R_WIDTH = R_HEADS * R_HEAD_DIM
DECAY_RANK = 64
ICLR_RANK = 64
GATE_RANK = 128
LORA_WIDTH = DECAY_RANK + ICLR_RANK + GATE_RANK
R_IN_WIDTH = 3 * R_WIDTH + LORA_WIDTH
D_FF = 2816
N_MOD = 6
NORM_EPS = 1e-6
GN_EPS = 64e-5
DECAY_SCALE = float(np.exp(-0.5))

F32 = jnp.float32
BF16 = jnp.bfloat16

VMEM_LIMIT_BYTES = 56 * 1024 * 1024
PROJ_ROWS = 256
FFN_ROWS = 1024
FFN_TILE = 256
FFN_PAD_ROWS = 8
FFN_CONV_ROWS = 1024
Q_ROWS = 128
STEP_ROWS = 256
GROUP = 128
PAIRS_PER_STEP = 2
CHUNK = 64

_NN = (((1,), (0,)), ((), ()))
_NT = (((1,), (1,)), ((), ()))
_TN = (((0,), (0,)), ((), ()))


def _dg(a, b, dims):
    return lax.dot_general(a, b, dims, preferred_element_type=F32)


def _dot(a, b):
    return _dg(a, b, _NN)


def _mm(a, b, dims=_NN):
    return _dg(a.astype(BF16), b.astype(BF16), dims)


def _mm_exact_lhs(m, x):
    hi = x.astype(BF16)
    r1 = x - hi.astype(F32)
    mid = r1.astype(BF16)
    lo = (r1 - mid.astype(F32)).astype(BF16)
    return _dot(m, hi) + (_dot(m, mid) + _dot(m, lo))


def _sigmoid(x):
    return 1.0 / (1.0 + jnp.exp(-x))


def _rms(x):
    return x * lax.rsqrt(jnp.mean(x * x, axis=-1, keepdims=True) + NORM_EPS)


def _params(*semantics):
    return pltpu.CompilerParams(dimension_semantics=semantics,
                                vmem_limit_bytes=VMEM_LIMIT_BYTES)


def _mod_kernel(c_ref, w_ref, b_ref, o_ref):
    c = c_ref[...]
    s = (c * _sigmoid(c)).astype(BF16)
    o_ref[...] = _dot(s, w_ref[...].astype(BF16)) + b_ref[...]


def _modulation(cc, w_ada, b_ada):
    rows = cc.shape[0]
    n = w_ada.shape[1]
    tile = 1024
    return pl.pallas_call(
        _mod_kernel,
        out_shape=jax.ShapeDtypeStruct((rows, n), F32),
        grid=(n // tile,),
        in_specs=[
            pl.BlockSpec((rows, D_MODEL), lambda j: (0, 0)),
            pl.BlockSpec((D_MODEL, tile), lambda j: (0, j)),
            pl.BlockSpec((1, tile), lambda j: (0, j)),
        ],
        out_specs=pl.BlockSpec((rows, tile), lambda j: (0, j)),
        compiler_params=_params("arbitrary"),
        name="adaln_mod",
    )(cc, w_ada, b_ada)


def _inproj_kernel(x_ref, mod_ref, g_ref, w_ref, qkv_ref, rkv_ref, lora_ref, sga_ref, sgr_ref):
    x = x_ref[...]
    h = _rms(x) * g_ref[...]
    h = h * (1.0 + mod_ref[1:2, :]) + mod_ref[0:1, :]
    hb = h.astype(BF16)
    c0 = ATTN_WIDTH + 2 * KV_WIDTH
    c1 = c0 + 3 * R_WIDTH
    c2 = c1 + LORA_WIDTH
    c3 = c2 + D_MODEL
    qkv_ref[...] = _dot(hb, w_ref[:, 0:c0])
    rkv_ref[...] = _dot(hb, w_ref[:, c0:c1])
    lora_ref[...] = _dot(hb, w_ref[:, c1:c2])
    sga_ref[...] = _sigmoid(_dot(hb, w_ref[:, c2:c3])).astype(BF16)
    sgr_ref[...] = _sigmoid(_dot(hb, w_ref[:, c3:c3 + D_MODEL])).astype(BF16)


def _in_projection(x2d, mod, norm1_g, w_in_bf16, seq_len):
    n_tok = x2d.shape[0]
    tm = PROJ_ROWS
    in_width = w_in_bf16.shape[1]
    widths = (ATTN_WIDTH + 2 * KV_WIDTH, 3 * R_WIDTH, LORA_WIDTH, D_MODEL, D_MODEL)
    dtypes = (F32, F32, F32, BF16, BF16)
    row = lambda i: (i, 0)
    return pl.pallas_call(
        _inproj_kernel,
        out_shape=tuple(jax.ShapeDtypeStruct((n_tok, w), dt) for w, dt in zip(widths, dtypes)),
        grid=(n_tok // tm,),
        in_specs=[
            pl.BlockSpec((tm, D_MODEL), row),
            pl.BlockSpec((None, N_MOD, D_MODEL), lambda i: ((i * tm) // seq_len, 0, 0)),
            pl.BlockSpec((1, D_MODEL), lambda i: (0, 0)),
            pl.BlockSpec((D_MODEL, in_width), lambda i: (0, 0)),
        ],
        out_specs=tuple(pl.BlockSpec((tm, w), row) for w in widths),
        compiler_params=_params("arbitrary"),
        name="in_proj",
    )(x2d, mod, norm1_g, w_in_bf16)


def _rope(x, cos2, sin2):
    half = HEAD_DIM // 2
    partner = jnp.concatenate([x[:, half:], x[:, :half]], axis=-1)
    return x * cos2 + partner * sin2


def _attn_kernel(*refs, seq_len, past_len, latent):
    if latent:
        (q_ref, k_ref, v_ref, ck_ref, cv_ref, cos_ref, sin_ref, qg_ref, kg_ref,
         o_ref, qs_ref, ks_ref, vs_ref) = refs
    else:
        (q_ref, k_ref, v_ref, qg_ref, kg_ref, o_ref, kn_ref, vn_ref, qs_ref, ks_ref, vs_ref) = refs
    t = seq_len
    d = HEAD_DIM
    group = pl.program_id(1)
    k = jnp.where(group == 0, k_ref[:, 0:d], k_ref[:, d:2 * d])
    v = jnp.where(group == 0, v_ref[:, 0:d], v_ref[:, d:2 * d])
    kn = _rms(k) * kg_ref[...]
    if latent:
        cos2 = cos_ref[...]
        sin2 = sin_ref[...]
        ks_ref[0:t, :] = _rope(kn, cos2, sin2).astype(BF16)
        ks_ref[t:t + past_len, :] = ck_ref[...].astype(BF16)
        vs_ref[0:t, :] = v.astype(BF16)
        vs_ref[t:t + past_len, :] = cv_ref[...].astype(BF16)
    else:
        kn_ref[...] = kn
        vn_ref[...] = v
        ks_ref[...] = kn.astype(BF16)
        vs_ref[...] = v.astype(BF16)
    for hq in range(Q_PER_KV):
        qn = _rms(q_ref[:, hq * d:(hq + 1) * d]) * qg_ref[...]
        if latent:
            qn = _rope(qn, cos2, sin2)
        qs_ref[hq] = qn.astype(BF16)
    scale = HEAD_DIM ** -0.5

    def body(qb, carry):
        r0 = pl.multiple_of(qb * Q_ROWS, Q_ROWS)
        outs = []
        for hq in range(Q_PER_KV):
            q = qs_ref[hq, pl.ds(r0, Q_ROWS), :]
            s = _dg(q, ks_ref[...], _NT) * scale
            m = jnp.max(s, axis=-1, keepdims=True)
            p = jnp.exp(s - m)
            l = jnp.sum(p, axis=-1, keepdims=True)
            outs.append(_dot(p.astype(BF16), vs_ref[...]) / l)
        o_ref[pl.ds(r0, Q_ROWS), :] = jnp.concatenate(outs, axis=-1)
        return carry

    lax.fori_loop(0, t // Q_ROWS, body, 0)


def _attention(qkv, qg, kg, cache=None, rope=None):
    b, t, _ = qkv.shape
    d = HEAD_DIM
    latent = cache is not None
    past_len = cache[0].shape[2] if latent else 0
    total = t + past_len
    gw = Q_PER_KV * d
    q_spec = pl.BlockSpec((None, t, gw), lambda i, g: (i, 0, g))
    k_spec = pl.BlockSpec((None, t, KV_WIDTH), lambda i, g: (i, 0, ATTN_WIDTH // KV_WIDTH))
    v_spec = pl.BlockSpec((None, t, KV_WIDTH), lambda i, g: (i, 0, ATTN_WIDTH // KV_WIDTH + 1))
    hm_spec = pl.BlockSpec((None, None, t, d), lambda i, g: (i, g, 0, 0))
    vec_spec = pl.BlockSpec((1, d), lambda i, g: (0, 0))
    out_shape = jax.ShapeDtypeStruct((b, t, ATTN_WIDTH), F32)
    scratch = [pltpu.VMEM((Q_PER_KV, t, d), BF16), pltpu.VMEM((total, d), BF16),
               pltpu.VMEM((total, d), BF16)]
    kern = functools.partial(_attn_kernel, seq_len=t, past_len=past_len, latent=latent)
    if latent:
        ck, cv = cache
        cos2, sin2 = rope
        c_spec = pl.BlockSpec((None, None, past_len, d), lambda i, g: (i, g, 0, 0))
        tab_spec = pl.BlockSpec((t, d), lambda i, g: (0, 0))
        return pl.pallas_call(
            kern,
            out_shape=out_shape,
            grid=(b, KV_HEADS),
            in_specs=[q_spec, k_spec, v_spec, c_spec, c_spec, tab_spec, tab_spec,
                      vec_spec, vec_spec],
            out_specs=q_spec,
            scratch_shapes=scratch,
            compiler_params=_params("arbitrary", "arbitrary"),
            name="attn_latent",
        )(qkv, qkv, qkv, ck, cv, cos2, sin2, qg, kg)
    hm_shape = jax.ShapeDtypeStruct((b, KV_HEADS, t, d), F32)
    return pl.pallas_call(
        kern,
        out_shape=(out_shape, hm_shape, hm_shape),
        grid=(b, KV_HEADS),
        in_specs=[q_spec, k_spec, v_spec, vec_spec, vec_spec],
        out_specs=(q_spec, hm_spec, hm_spec),
        scratch_shapes=scratch,
        compiler_params=_params("arbitrary", "arbitrary"),
        name="attn_context",
    )(qkv, qkv, qkv, qg, kg)


def _conv_rows(ref_rows, taps, start, n, total):
    cur = ref_rows(start, n)
    width = cur.shape[-1]
    zero = jnp.zeros((1, width), F32)
    prev_row = ref_rows(start - 1, 1) if start > 0 else zero
    next_row = ref_rows(start + n, 1) if start + n < total else zero
    ridx = lax.broadcasted_iota(jnp.int32, (n, 1), 0)
    prev = jnp.where(ridx == 0, prev_row, pltpu.roll(cur, 1, 0))
    nxt = jnp.where(ridx == n - 1, next_row, pltpu.roll(cur, n - 1, 0))
    return prev * taps[0] + cur * taps[1] + nxt * taps[2]


def _round_robin(chains):
    results = [None] * len(chains)
    live = list(range(len(chains)))
    while live:
        for i in list(live):
            try:
                next(chains[i])
            except StopIteration as done:
                results[i] = done.value
                live.remove(i)
    return results


def _rwkv_kernel(*refs, seq_len, has_state):
    n_in = 19 if has_state else 18
    (r_ref, k_ref, v_ref, lora_ref, cr_ref, ck_ref, cv_ref, cl_ref, w0_ref, w2_ref, a0_ref, a2_ref,
     g2_ref, kk_ref, ka_ref, rk_ref, lng_ref, lnb_ref) = refs[:18]
    s0_ref = refs[18] if has_state else None
    y_ref, sfin_ref, rs_ref, ks_ref, vs_ref, ls_ref, yacc_ref, bacc_ref, st_ref = refs[n_in:]
    t = seq_len
    n_steps = t // STEP_ROWS
    n_sub = STEP_ROWS // GROUP
    n_chunks = GROUP // CHUNK
    n = R_HEAD_DIM
    width = 2 * n
    step = pl.program_id(2)
    pair_lanes = [slice(q * width, (q + 1) * width) for q in range(PAIRS_PER_STEP)]

    @pl.when(step == 0)
    def _():
        for blk in range(n_steps):
            start = blk * STEP_ROWS
            rows = slice(start, start + STEP_ROWS)
            for src, cw, dst in ((r_ref, cr_ref, rs_ref), (k_ref, ck_ref, ks_ref),
                                 (v_ref, cv_ref, vs_ref), (lora_ref, cl_ref, ls_ref)):
                taps = [cw[j:j + 1, :] for j in range(3)]
                dst[rows, :] = _conv_rows(lambda s, m, src=src: src[s:s + m, :],
                                          taps, start, STEP_ROWS, t)
        zero = jnp.zeros((n, n), F32)
        for d in range(2):
            for q in range(PAIRS_PER_STEP):
                if has_state:
                    s_a, s_b = s0_ref[d, 2 * q], s0_ref[d, 2 * q + 1]
                    bd = jnp.concatenate([jnp.concatenate([s_a, zero], axis=1),
                                          jnp.concatenate([zero, s_b], axis=1)], axis=0)
                    st_ref[d, q] = bd.T
                else:
                    st_ref[d, q] = jnp.zeros((width, width), F32)

    lane = lax.broadcasted_iota(jnp.int32, (1, width), 1)
    head0 = lane < R_HEAD_DIM

    def per_head(x):
        x0 = jnp.where(head0, x, 0.0)
        return (x0, x - x0)

    def pick(x0, x1):
        return jnp.where(head0, x0, x1)

    def head_sum(x):
        s0 = jnp.sum(jnp.where(head0, x, 0.0), axis=-1, keepdims=True)
        s1 = jnp.sum(jnp.where(head0, 0.0, x), axis=-1, keepdims=True)
        return jnp.where(head0, s0, s1)

    def scan_rows(d, q, blk, state):
        rev = d == 1
        lanes = pair_lanes[q]
        base = blk * STEP_ROWS
        ri = lax.broadcasted_iota(jnp.int32, (GROUP, GROUP), 0)
        ci = lax.broadcasted_iota(jnp.int32, (GROUP, GROUP), 1)
        same = (ri // CHUNK) == (ci // CHUNK)
        eye = (ri == ci).astype(F32)
        blk_ones = same.astype(BF16)
        strict = same & ((ci > ri) if rev else (ci < ri))
        incl = same & ((ci >= ri) if rev else (ci <= ri))
        tri = incl.astype(BF16)

        def half_mask(m):
            return ((ri // (2 * m)) == (ci // (2 * m))) & ((ri // m) != (ci // m))

        subs = []
        for s in (range(n_sub - 1, -1, -1) if rev else range(n_sub)):
            rows = pl.ds(pl.multiple_of(base + s * GROUP, GROUP), GROUP)
            lo = ls_ref[rows, :]
            wd_t = jnp.tanh(lo[:, 0:DECAY_RANK])
            z = w0_ref[d:d + 1, lanes] + _mm(wd_t, w2_ref[d, :, lanes])
            subs.append({"rows": rows, "r": rs_ref[rows, lanes], "k": ks_ref[rows, lanes],
                         "v": vs_ref[rows, lanes], "ad": lo[:, DECAY_RANK:DECAY_RANK + ICLR_RANK],
                         "lw": -DECAY_SCALE * _sigmoid(z)})
        yield
        for p in subs:
            p["cum"] = _mm_exact_lhs(tri, p["lw"])
            p["cum_c"] = _mm_exact_lhs(blk_ones, p["lw"])
            a = _sigmoid(a0_ref[d:d + 1, lanes] + _mm(p["ad"], a2_ref[d, :, lanes]))
            p["kd"] = p["k"] * (1.0 + (a - 1.0) * ka_ref[:, lanes])
            kk = p["k"] * kk_ref[:, lanes]
            p["kk"] = kk * lax.rsqrt(head_sum(kk * kk) + 1e-12)
            p["b"] = p["kk"] * a
            bacc_ref[d, p["rows"], lanes] = head_sum(p["r"] * p["kd"] * rk_ref[:, lanes]) * p["v"]
        yield
        for p in subs:
            cum, cum_c = p["cum"], p["cum_c"]
            p["a_t"] = -p["kk"] * jnp.exp(cum - p["lw"])
            p["r_t"] = p["r"] * jnp.exp(cum)
            inv = jnp.exp(-cum)
            p["b_t"] = p["b"] * inv
            p["k_t"] = p["kd"] * inv
            tail = jnp.exp(cum_c - cum)
            p["b_h"] = p["b"] * tail
            p["k_h"] = p["kd"] * tail
            p["p_c"] = jnp.exp(cum_c)
        yield
        for p in subs:
            a_th = per_head(p["a_t"])
            p["a_ab"] = [jnp.where(strict, _mm(x, p["b_t"], _NT), 0.0) for x in a_th]
            p["a_ak"] = [jnp.where(strict, _mm(x, p["k_t"], _NT), 0.0) for x in a_th]
            r_th = per_head(p["r_t"])
            p["a_rb"] = [jnp.where(incl, _mm(x, p["b_t"], _NT), 0.0) for x in r_th]
            p["a_rk"] = [jnp.where(incl, _mm(x, p["k_t"], _NT), 0.0) for x in r_th]
        yield
        for p in subs:
            p["akv"] = pick(*[_mm(m, p["v"]) for m in p["a_ak"]])
            p["rkv"] = pick(*[_mm(m, p["v"]) for m in p["a_rk"]])
            p["tm"] = [eye + jnp.where(half_mask(1), m, 0.0) for m in p["a_ab"]]
        m_size = 2
        while m_size < CHUNK:
            yield
            off = half_mask(m_size)
            for p in subs:
                p["ot"] = [_mm(jnp.where(off, n_mat, 0.0), tm)
                           for n_mat, tm in zip(p["a_ab"], p["tm"])]
            yield
            for p in subs:
                p["tm"] = [tm + _mm(tm, ot) for tm, ot in zip(p["tm"], p["ot"])]
            m_size *= 2
        yield
        for p in subs:
            p["ta"] = pick(*[_mm(tm, p["a_t"]) for tm in p["tm"]])
            p["tk"] = pick(*[_mm(tm, p["akv"]) for tm in p["tm"]])
        yield
        cr = [slice(c * CHUNK, (c + 1) * CHUNK) for c in range(n_chunks)]
        for p in subs:
            p["r2"] = p["r_t"] + pick(*[_mm(m, p["ta"]) for m in p["a_rb"]])
            p["y2"] = p["rkv"] + pick(*[_mm(m, p["tk"]) for m in p["a_rb"]])
            p["m_t"] = [eye * p["p_c"][c * CHUNK:c * CHUNK + 1, :]
                        + jnp.where(same, _mm(p["b_h"][cr[c]], p["ta"][cr[c]], _TN), 0.0)
                        for c in range(n_chunks)]
            p["g_t"] = [jnp.where(same, _mm(p["b_h"][cr[c]], p["tk"][cr[c]], _TN)
                                  + _mm(p["k_h"][cr[c]], p["v"][cr[c]], _TN), 0.0)
                        for c in range(n_chunks)]
        for p in subs:
            ys = [None] * n_chunks
            for c in (range(n_chunks - 1, -1, -1) if rev else range(n_chunks)):
                yield
                ys[c] = _mm(p["r2"][cr[c]], state) + p["y2"][cr[c]]
                state = _mm(p["m_t"][c], state) + p["g_t"][c]
            yacc_ref[d, p["rows"], lanes] = jnp.concatenate(ys, axis=0)
        return state

    chains = [(d, q) for q in range(PAIRS_PER_STEP) for d in range(2)]
    blocks = (step, n_steps - 1 - step)
    new_state = _round_robin([scan_rows(d, q, blocks[d], st_ref[d, q]) for d, q in chains])
    for (d, q), s_new in zip(chains, new_state):
        st_ref[d, q] = s_new

    @pl.when(step == n_steps - 1)
    def _():
        for d, q in chains:
            s_vk = st_ref[d, q].T
            sfin_ref[d, 2 * q] = s_vk[0:n, 0:n]
            sfin_ref[d, 2 * q + 1] = s_vk[n:width, n:width]
        inv_n = 1.0 / n
        for blk in range(n_steps):
            rows = slice(blk * STEP_ROWS, (blk + 1) * STEP_ROWS)
            gd = _sigmoid(ls_ref[rows, DECAY_RANK + ICLR_RANK:LORA_WIDTH]).astype(BF16)
            for lanes in pair_lanes:
                y = yacc_ref[0, rows, lanes] + yacc_ref[1, rows, lanes]
                yc = y - head_sum(y) * inv_n
                yn = yc * lax.rsqrt(head_sum(yc * yc) * inv_n + GN_EPS)
                out = (yn * lng_ref[:, lanes] + lnb_ref[:, lanes]
                       + (bacc_ref[0, rows, lanes] + bacc_ref[1, rows, lanes]))
                y_ref[rows, lanes] = out * _dot(gd, g2_ref[:, lanes].astype(BF16))


def _rwkv(rkv, lora, p, s0):
    n_seq, t, _ = rkv.shape
    has_state = s0 is not None
    n = R_HEAD_DIM
    heads = 2 * PAIRS_PER_STEP
    width = heads * n
    blocks = R_WIDTH // width
    col = lambda off: (lambda s, i, j: (s, 0, off + i))
    vec = lambda off: (lambda s, i, j: (0, off + i))
    dir_mat = lambda s, i, j: (0, 0, i)
    in_specs = [
        pl.BlockSpec((None, t, width), col(0)),
        pl.BlockSpec((None, t, width), col(blocks)),
        pl.BlockSpec((None, t, width), col(2 * blocks)),
        pl.BlockSpec((None, t, LORA_WIDTH), lambda s, i, j: (s, 0, 0)),
        pl.BlockSpec((3, width), vec(0)),
        pl.BlockSpec((3, width), vec(blocks)),
        pl.BlockSpec((3, width), vec(2 * blocks)),
        pl.BlockSpec((3, LORA_WIDTH), lambda s, i, j: (0, 3 * R_WIDTH // LORA_WIDTH)),
        pl.BlockSpec((2, width), vec(0)),
        pl.BlockSpec((2, DECAY_RANK, width), dir_mat),
        pl.BlockSpec((2, width), vec(0)),
        pl.BlockSpec((2, ICLR_RANK, width), dir_mat),
        pl.BlockSpec((GATE_RANK, width), vec(0)),
        pl.BlockSpec((1, width), vec(0)),
        pl.BlockSpec((1, width), vec(0)),
        pl.BlockSpec((1, width), vec(0)),
        pl.BlockSpec((1, width), vec(0)),
        pl.BlockSpec((1, width), vec(0)),
    ]
    conv = p["conv"]
    args = [rkv, rkv, rkv, lora, conv, conv, conv, conv, p["w0"], p["w2"], p["a0"], p["a2"],
            p["g2"], p["k_k"], p["k_a"], p["r_k"], p["ln_g"], p["ln_b"]]
    state_spec = pl.BlockSpec((None, 2, heads, n, n), lambda s, i, j: (s, 0, i, 0, 0))
    if has_state:
        in_specs.append(state_spec)
        args.append(s0)
    seq_buf = pltpu.VMEM((t, width), F32)
    dir_buf = pltpu.VMEM((2, t, width), F32)
    return pl.pallas_call(
        functools.partial(_rwkv_kernel, seq_len=t, has_state=has_state),
        out_shape=(jax.ShapeDtypeStruct((n_seq, t, R_WIDTH), F32),
                   jax.ShapeDtypeStruct((n_seq, 2, R_HEADS, n, n), F32)),
        grid=(n_seq, blocks, t // STEP_ROWS),
        in_specs=in_specs,
        out_specs=(pl.BlockSpec((None, t, width), col(0)), state_spec),
        scratch_shapes=[seq_buf, seq_buf, seq_buf, pltpu.VMEM((t, LORA_WIDTH), F32),
                        dir_buf, dir_buf, pltpu.VMEM((2, PAIRS_PER_STEP, 2 * n, 2 * n), F32)],
        compiler_params=_params("arbitrary", "arbitrary", "arbitrary"),
        name="rwkv7_latent" if has_state else "rwkv7_context",
    )(*args)


def _merge_kernel(x_ref, ao_ref, ro_ref, sga_ref, sgr_ref, mod_ref, g_ref, wab_ref, wrb_ref,
                  wo_ref, x1_ref, h2_ref):
    merged = (sga_ref[...] * _dot(ao_ref[...].astype(BF16), wab_ref[...])
              + sgr_ref[...] * _dot(ro_ref[...].astype(BF16), wrb_ref[...]))
    x1 = x_ref[...] + mod_ref[2:3, :] * _dot(merged.astype(BF16), wo_ref[...])
    x1_ref[...] = x1
    h2 = _rms(x1) * g_ref[...]
    h2_ref[...] = (h2 * (1.0 + mod_ref[4:5, :]) + mod_ref[3:4, :]).astype(BF16)


def _merge(x2d, attn_o, rwkv_o, sga, sgr, mod, norm2_g, wab, wrb, wo, seq_len):
    n_tok = x2d.shape[0]
    tm = PROJ_ROWS
    row = lambda i: (i, 0)
    const = lambda i: (0, 0)
    return pl.pallas_call(
        _merge_kernel,
        out_shape=(jax.ShapeDtypeStruct((n_tok, D_MODEL), F32),
                   jax.ShapeDtypeStruct((n_tok, D_MODEL), BF16)),
        grid=(n_tok // tm,),
        in_specs=[
            pl.BlockSpec((tm, D_MODEL), row),
            pl.BlockSpec((tm, ATTN_WIDTH), row),
            pl.BlockSpec((tm, R_WIDTH), row),
            pl.BlockSpec((tm, D_MODEL), row),
            pl.BlockSpec((tm, D_MODEL), row),
            pl.BlockSpec((None, N_MOD, D_MODEL), lambda i: ((i * tm) // seq_len, 0, 0)),
            pl.BlockSpec((1, D_MODEL), const),
            pl.BlockSpec((ATTN_WIDTH, D_MODEL), const),
            pl.BlockSpec((R_WIDTH, D_MODEL), const),
            pl.BlockSpec((D_MODEL, D_MODEL), const),
        ],
        out_specs=(pl.BlockSpec((tm, D_MODEL), row), pl.BlockSpec((tm, D_MODEL), row)),
        compiler_params=_params("arbitrary"),
        name="merge_out_proj",
    )(x2d, attn_o, rwkv_o, sga, sgr, mod, norm2_g, wab, wrb, wo)


def _ffn_kernel(h2_ref, x1_ref, mod_ref, wu_ref, wg_ref, cu_ref, cg_ref, wd_ref, fg_ref,
                y_ref, acc_ref, upu_ref, upg_ref, *, seq_len):
    i = pl.program_id(0)
    j = pl.program_id(1)
    rows = h2_ref.shape[0]
    pad = FFN_PAD_ROWS
    rb = FFN_CONV_ROWS
    tile = 8

    @pl.when((i == 0) & (j == 0))
    def _():
        for ref in (upu_ref, upg_ref):
            ref[0:pad, :] = jnp.zeros((pad, ref.shape[1]), F32)
            ref[pad + rows:pad + rows + pad, :] = jnp.zeros((pad, ref.shape[1]), F32)

    @pl.when(j == 0)
    def _():
        acc_ref[...] = jnp.zeros(acc_ref.shape, F32)

    h2 = h2_ref[...]
    upu_ref[pad:pad + rows, :] = _dot(h2, wu_ref[...])
    upg_ref[pad:pad + rows, :] = _dot(h2, wg_ref[...])
    row0 = lax.broadcasted_iota(jnp.int32, (tile, 1), 0) == 0
    row7 = lax.broadcasted_iota(jnp.int32, (tile, 1), 0) == tile - 1

    taps_u = [cu_ref[k:k + 1, :] for k in range(3)]
    taps_g = [cg_ref[k:k + 1, :] for k in range(3)]

    def zero_rows(x, tile_offsets, row_mask):
        parts, done = [], 0
        for off in tile_offsets:
            parts += [x[done:off], jnp.where(row_mask, 0.0, x[off:off + tile])]
            done = off + tile
        parts.append(x[done:])
        return jnp.concatenate([p for p in parts if p.shape[0]], axis=0)

    def conv(up_ref, taps, r0):
        base = pad + r0
        cur = up_ref[base:base + rb, :]
        prev = up_ref[base - 1:base - 1 + rb, :]
        nxt = up_ref[base + 1:base + 1 + rb, :]
        first_seq = -(-r0 // seq_len) * seq_len
        starts = [s - r0 for s in range(first_seq, r0 + rb, seq_len)]
        ends = [e - r0 for e in range(first_seq if first_seq > r0 else first_seq + seq_len,
                                      r0 + rb + 1, seq_len)]
        prev = zero_rows(prev, starts, row0)
        nxt = zero_rows(nxt, [e - tile for e in ends], row7)
        return prev * taps[0] + cur * taps[1] + nxt * taps[2]

    acts = []
    for r0 in range(0, rows, rb):
        up_u = conv(upu_ref, taps_u, r0)
        up_g = conv(upg_ref, taps_g, r0)
        acts.append((up_g * _sigmoid(up_g) * up_u).astype(BF16))
    acc_ref[...] += _dot(jnp.concatenate(acts, axis=0), wd_ref[...])

    @pl.when(j == pl.num_programs(1) - 1)
    def _():
        x2 = x1_ref[...] + mod_ref[...] * acc_ref[...]
        y_ref[...] = _rms(x2) * fg_ref[...]


def _ffn(h2, x1, gate2, ffn_up, ffn_conv, ffn_down, final_g, seq_len):
    n_tok = h2.shape[0]
    tm = FFN_ROWS
    tn = FFN_TILE
    nt = D_FF // tn
    assert tm % seq_len == 0 or seq_len % tm == 0
    gate_map = lambda i, j: ((i * tm) // seq_len, 0, 0)
    return pl.pallas_call(
        functools.partial(_ffn_kernel, seq_len=seq_len),
        out_shape=jax.ShapeDtypeStruct((n_tok, D_MODEL), F32),
        grid=(n_tok // tm, nt),
        in_specs=[
            pl.BlockSpec((tm, D_MODEL), lambda i, j: (i, 0)),
            pl.BlockSpec((tm, D_MODEL), lambda i, j: (i, 0)),
            pl.BlockSpec((None, 1, D_MODEL), gate_map),
            pl.BlockSpec((D_MODEL, tn), lambda i, j: (0, j)),
            pl.BlockSpec((D_MODEL, tn), lambda i, j: (0, nt + j)),
            pl.BlockSpec((3, tn), lambda i, j: (0, j)),
            pl.BlockSpec((3, tn), lambda i, j: (0, nt + j)),
            pl.BlockSpec((tn, D_MODEL), lambda i, j: (j, 0)),
            pl.BlockSpec((1, D_MODEL), lambda i, j: (0, 0)),
        ],
        out_specs=pl.BlockSpec((tm, D_MODEL), lambda i, j: (i, 0)),
        scratch_shapes=[pltpu.VMEM((tm, D_MODEL), F32),
                        pltpu.VMEM((tm + 2 * FFN_PAD_ROWS, tn), F32),
                        pltpu.VMEM((tm + 2 * FFN_PAD_ROWS, tn), F32)],
        compiler_params=_params("arbitrary", "arbitrary"),
        name="conv_ffn",
    )(h2, x1, gate2, ffn_up, ffn_up, ffn_conv, ffn_conv, ffn_down, final_g)


def _layer_path(x, mod, w, rope, cache, s0):
    b, t, _ = x.shape
    x2d = x.reshape(b * t, D_MODEL)
    qkv, rkv, lora, sga, sgr = _in_projection(x2d, mod, w["norm1_g"], w["w_in"], t)
    qkv = qkv.reshape(b, t, ATTN_WIDTH + 2 * KV_WIDTH)
    if rope is None:
        attn_o, k_norm, v_hm = _attention(qkv, w["q_norm_g"], w["k_norm_g"])
    else:
        attn_o = _attention(qkv, w["q_norm_g"], w["k_norm_g"], cache=cache, rope=rope)
        k_norm = v_hm = None
    rwkv_o, s_fin = _rwkv(rkv.reshape(b, t, 3 * R_WIDTH), lora.reshape(b, t, LORA_WIDTH),
                          w["rwkv"], s0)
    x1, h2 = _merge(x2d, attn_o.reshape(b * t, ATTN_WIDTH), rwkv_o.reshape(b * t, R_WIDTH),
                    sga, sgr, mod, w["norm2_g"], w["w_attn_br"], w["w_rwkv_br"], w["w_out"], t)
    gate2 = mod[:, 5:6, :]
    y = _ffn(h2, x1, gate2, w["ffn_up"], w["ffn_conv"], w["ffn_down"], w["final_norm_g"], t)
    return y.reshape(b, t, D_MODEL), k_norm, v_hm, s_fin


def _rope_tables(rows):
    row = jnp.repeat(jnp.arange(rows), GRID_W).astype(F32)
    col = jnp.tile(jnp.arange(GRID_W), rows).astype(F32)
    inv = 1.0 / (ROPE_THETA ** (jnp.arange(ROPE_PAIRS, dtype=F32) / ROPE_PAIRS))
    ang = jnp.concatenate([row[:, None] * inv, col[:, None] * inv], axis=-1)
    cos, sin = jnp.cos(ang), jnp.sin(ang)
    return jnp.concatenate([cos, cos], axis=-1), jnp.concatenate([-sin, sin], axis=-1)


def kernel(x_prompt, x_sample, cache_k, cache_v, state_rwkv, c, c_ctx, w_ada, b_ada, norm1_g, w_in, q_norm_g, k_norm_g, rwkv_conv, rwkv_w0, rwkv_w2, rwkv_a0, rwkv_a2, rwkv_g2, rwkv_k_k, rwkv_k_a, rwkv_r_k, rwkv_ln_g, rwkv_ln_b, w_attn_br, w_rwkv_br, w_out, norm2_g, ffn_up, ffn_conv, ffn_down, final_norm_g):
    depth = w_in.shape[0]
    assert depth == 1, "single trunk layer"
    l = 0
    n_ctx = x_prompt.shape[0]
    n_lat = x_sample.shape[0]

    cc = jnp.concatenate([c_ctx[None, :], c, jnp.zeros((8 - 1 - n_lat, D_MODEL), F32)], axis=0)
    mod_all = _modulation(cc, w_ada[l], b_ada[l][None, :])
    mod_ctx = jnp.broadcast_to(mod_all[0:1].reshape(1, N_MOD, D_MODEL), (n_ctx, N_MOD, D_MODEL))
    mod_lat = mod_all[1:1 + n_lat].reshape(n_lat, N_MOD, D_MODEL)

    rwkv_p = {
        "conv": rwkv_conv[l], "w0": rwkv_w0[l], "w2": rwkv_w2[l], "a0": rwkv_a0[l],
        "a2": rwkv_a2[l], "g2": rwkv_g2[l], "k_k": rwkv_k_k[l][None, :],
        "k_a": rwkv_k_a[l][None, :], "r_k": rwkv_r_k[l].reshape(1, R_WIDTH),
        "ln_g": rwkv_ln_g[l][None, :], "ln_b": rwkv_ln_b[l][None, :],
    }
    w = {
        "norm1_g": norm1_g[l][None, :], "w_in": w_in[l].astype(BF16),
        "q_norm_g": q_norm_g[l][None, :], "k_norm_g": k_norm_g[l][None, :],
        "rwkv": rwkv_p,
        "w_attn_br": w_attn_br[l].astype(BF16), "w_rwkv_br": w_rwkv_br[l].astype(BF16),
        "w_out": w_out[l].astype(BF16), "norm2_g": norm2_g[l][None, :],
        "ffn_up": ffn_up[l].astype(BF16), "ffn_conv": ffn_conv[l],
        "ffn_down": ffn_down[l].astype(BF16), "final_norm_g": final_norm_g[None, :],
    }

    y_prompt, k_ctx, v_ctx, s_ctx = _layer_path(x_prompt, mod_ctx, w, None, None, None)
    rope = _rope_tables(x_sample.shape[1] // GRID_W)
    y_sample, _, _, _ = _layer_path(x_sample, mod_lat, w, rope, (cache_k[:, l], cache_v[:, l]),
                                    state_rwkv[:, l])
    return (y_prompt, y_sample, k_ctx[:, None], v_ctx[:, None], s_ctx[:, None])
```

```python
import functools

import jax
import jax.numpy as jnp
import numpy as np
from jax import lax
from jax.experimental import pallas as pl
from jax.experimental.pallas import tpu as pltpu

D_MODEL = 1024
GRID_W = 64
N_HEADS = 8
KV_HEADS = 2
Q_PER_KV = N_HEADS // KV_HEADS
HEAD_DIM = 64
ATTN_WIDTH = N_HEADS * HEAD_DIM
KV_WIDTH = KV_HEADS * HEAD_DIM
ROPE_THETA = 10000.0
ROPE_PAIRS = HEAD_DIM // 4
R_HEADS = 8
R_HEAD_DIM = 64
R_WIDTH = R_HEADS * R_HEAD_DIM
DECAY_RANK = 64
ICLR_RANK = 64
GATE_RANK = 128
LORA_WIDTH = DECAY_RANK + ICLR_RANK + GATE_RANK
R_IN_WIDTH = 3 * R_WIDTH + LORA_WIDTH
D_FF = 2816
N_MOD = 6
NORM_EPS = 1e-6
GN_EPS = 64e-5
DECAY_SCALE = float(np.exp(-0.5))

F32 = jnp.float32
BF16 = jnp.bfloat16

VMEM_LIMIT_BYTES = 56 * 1024 * 1024
PROJ_ROWS = 256
FFN_ROWS = 1024
FFN_TILE = 256
FFN_PAD_ROWS = 8
FFN_CONV_ROWS = 1024
Q_ROWS = 128
KEY_CHUNK = 256
STEP_ROWS = 256
GROUP = 128
PAIRS_PER_STEP = 2
CHUNK = 64

_NN = (((1,), (0,)), ((), ()))
_NT = (((1,), (1,)), ((), ()))
_TN = (((0,), (0,)), ((), ()))


def _dg(a, b, dims):
    return lax.dot_general(a, b, dims, preferred_element_type=F32)


def _dot(a, b):
    return _dg(a, b, _NN)


def _mm(a, b, dims=_NN):
    return _dg(a.astype(BF16), b.astype(BF16), dims)


def _mm_exact_lhs(m, x):
    hi = x.astype(BF16)
    r1 = x - hi.astype(F32)
    mid = r1.astype(BF16)
    lo = (r1 - mid.astype(F32)).astype(BF16)
    return _dot(m, hi) + (_dot(m, mid) + _dot(m, lo))


def _sigmoid(x):
    return 1.0 / (1.0 + jnp.exp(-x))


def _rms(x):
    return x * lax.rsqrt(jnp.mean(x * x, axis=-1, keepdims=True) + NORM_EPS)


def _params(*semantics):
    return pltpu.CompilerParams(dimension_semantics=semantics,
                                vmem_limit_bytes=VMEM_LIMIT_BYTES)


def _mod_kernel(c_ref, w_ref, b_ref, o_ref):
    c = c_ref[...]
    s = (c * _sigmoid(c)).astype(BF16)
    o_ref[...] = _dot(s, w_ref[...].astype(BF16)) + b_ref[...]


def _modulation(cc, w_ada, b_ada):
    rows = cc.shape[0]
    n = w_ada.shape[1]
    tile = 1024
    return pl.pallas_call(
        _mod_kernel,
        out_shape=jax.ShapeDtypeStruct((rows, n), F32),
        grid=(n // tile,),
        in_specs=[
            pl.BlockSpec((rows, D_MODEL), lambda j: (0, 0)),
            pl.BlockSpec((D_MODEL, tile), lambda j: (0, j)),
            pl.BlockSpec((1, tile), lambda j: (0, j)),
        ],
        out_specs=pl.BlockSpec((rows, tile), lambda j: (0, j)),
        compiler_params=_params("arbitrary"),
        name="adaln_mod",
    )(cc, w_ada, b_ada)


def _inproj_kernel(x_ref, mod_ref, g_ref, w_ref, qkv_ref, rkv_ref, lora_ref, sga_ref, sgr_ref):
    x = x_ref[...]
    h = _rms(x) * g_ref[...]
    h = h * (1.0 + mod_ref[1:2, :]) + mod_ref[0:1, :]
    hb = h.astype(BF16)
    c0 = ATTN_WIDTH + 2 * KV_WIDTH
    c1 = c0 + 3 * R_WIDTH
    c2 = c1 + LORA_WIDTH
    c3 = c2 + D_MODEL
    qkv_ref[...] = _dot(hb, w_ref[:, 0:c0])
    rkv_ref[...] = _dot(hb, w_ref[:, c0:c1])
    lora_ref[...] = _dot(hb, w_ref[:, c1:c2])
    sga_ref[...] = _sigmoid(_dot(hb, w_ref[:, c2:c3])).astype(BF16)
    sgr_ref[...] = _sigmoid(_dot(hb, w_ref[:, c3:c3 + D_MODEL])).astype(BF16)


def _in_projection(x2d, mod, norm1_g, w_in_bf16, seq_len):
    n_tok = x2d.shape[0]
    tm = PROJ_ROWS
    in_width = w_in_bf16.shape[1]
    widths = (ATTN_WIDTH + 2 * KV_WIDTH, 3 * R_WIDTH, LORA_WIDTH, D_MODEL, D_MODEL)
    dtypes = (F32, F32, F32, BF16, BF16)
    row = lambda i: (i, 0)
    return pl.pallas_call(
        _inproj_kernel,
        out_shape=tuple(jax.ShapeDtypeStruct((n_tok, w), dt) for w, dt in zip(widths, dtypes)),
        grid=(n_tok // tm,),
        in_specs=[
            pl.BlockSpec((tm, D_MODEL), row),
            pl.BlockSpec((None, N_MOD, D_MODEL), lambda i: ((i * tm) // seq_len, 0, 0)),
            pl.BlockSpec((1, D_MODEL), lambda i: (0, 0)),
            pl.BlockSpec((D_MODEL, in_width), lambda i: (0, 0)),
        ],
        out_specs=tuple(pl.BlockSpec((tm, w), row) for w in widths),
        compiler_params=_params("arbitrary"),
        name="in_proj",
    )(x2d, mod, norm1_g, w_in_bf16)


def _attn_kernel(*refs, seq_len, past_len, latent):
    if latent:
        (q_ref, k_ref, v_ref, ck_ref, cv_ref, cos_ref, sin_ref, qg_ref, kg_ref,
         o_ref, qs_ref, kd_ref, vd_ref) = refs
    else:
        (q_ref, k_ref, v_ref, qg_ref, kg_ref, o_ref, kn_ref, vn_ref, qs_ref, kd_ref, vd_ref) = refs
    t = seq_len
    d = HEAD_DIM
    width = 2 * d
    total = t + past_len
    group = pl.program_id(1)
    lane = lax.broadcasted_iota(jnp.int32, (1, width), 1)
    head_of_lane = lane // d
    low_half = (lane % d) < d // 2
    inv_d = 1.0 / d

    def pair_norm(x, gain):
        sq = x * x
        s0 = jnp.sum(jnp.where(head_of_lane == 0, sq, 0.0), axis=-1, keepdims=True)
        s1 = jnp.sum(jnp.where(head_of_lane == 0, 0.0, sq), axis=-1, keepdims=True)
        ms = jnp.where(head_of_lane == 0, s0, s1) * inv_d
        y = x * lax.rsqrt(ms + NORM_EPS) * gain
        if latent:
            partner = jnp.where(low_half, pltpu.roll(y, width - d // 2, 1), pltpu.roll(y, d // 2, 1))
            y = y * cos_ref[...] + partner * sin_ref[...]
        return y

    def this_group(x):
        return jnp.where(head_of_lane == group, x, pltpu.roll(x, d, 1))

    kg2 = jnp.concatenate([kg_ref[...], kg_ref[...]], axis=-1)
    qg2 = jnp.concatenate([qg_ref[...], qg_ref[...]], axis=-1)
    v_dup = this_group(v_ref[...])
    if latent:
        kd_ref[0:t, :] = this_group(pair_norm(k_ref[...], kg2)).astype(BF16)
        ck = ck_ref[...].astype(BF16)
        kd_ref[t:total, :] = jnp.concatenate([ck, ck], axis=-1)
        vd_ref[0:t, :] = v_dup.astype(BF16)
        cv = cv_ref[...].astype(BF16)
        vd_ref[t:total, :] = jnp.concatenate([cv, cv], axis=-1)
    else:
        k_dup = this_group(pair_norm(k_ref[...], kg2))
        kn_ref[...] = k_dup[:, 0:d]
        vn_ref[...] = v_dup[:, 0:d]
        kd_ref[...] = k_dup.astype(BF16)
        vd_ref[...] = v_dup.astype(BF16)
    scale = d ** -0.5
    for j in range(Q_PER_KV // 2):
        cols = slice(j * width, (j + 1) * width)
        qs_ref[:, cols] = (pair_norm(q_ref[:, cols], qg2) * scale).astype(BF16)
    n_chunks = total // KEY_CHUNK

    def body(qb, carry):
        r0 = pl.multiple_of(qb * Q_ROWS, Q_ROWS)
        outs = []
        for j in range(Q_PER_KV // 2):
            qp = qs_ref[pl.ds(r0, Q_ROWS), j * width:(j + 1) * width]
            heads = []
            for h in range(2):
                qm = jnp.where(head_of_lane == h, qp, jnp.zeros_like(qp))
                m = l = acc = None
                for c in range(n_chunks):
                    keys = slice(c * KEY_CHUNK, (c + 1) * KEY_CHUNK)
                    s = _dg(qm, kd_ref[keys, :], _NT)
                    m_c = jnp.max(s, axis=-1, keepdims=True)
                    if c == 0:
                        m = m_c
                        p = jnp.exp(s - m)
                        l = jnp.sum(p, axis=-1, keepdims=True)
                        acc = _dot(p.astype(BF16), vd_ref[keys, :])
                    else:
                        m_new = jnp.maximum(m, m_c)
                        alpha = jnp.exp(m - m_new)
                        p = jnp.exp(s - m_new)
                        l = alpha * l + jnp.sum(p, axis=-1, keepdims=True)
                        acc = alpha * acc + _dot(p.astype(BF16), vd_ref[keys, :])
                        m = m_new
                heads.append(acc / l)
            outs.append(jnp.where(head_of_lane == 0, heads[0], heads[1]))
        o_ref[pl.ds(r0, Q_ROWS), :] = jnp.concatenate(outs, axis=-1)
        return carry

    lax.fori_loop(0, t // Q_ROWS, body, 0)


def _attention(qkv, qg, kg, cache=None, rope=None):
    b, t, _ = qkv.shape
    d = HEAD_DIM
    latent = cache is not None
    past_len = cache[0].shape[2] if latent else 0
    total = t + past_len
    assert total % KEY_CHUNK == 0 and KV_WIDTH == 2 * d
    gw = Q_PER_KV * d
    q_spec = pl.BlockSpec((None, t, gw), lambda i, g: (i, 0, g))
    k_spec = pl.BlockSpec((None, t, KV_WIDTH), lambda i, g: (i, 0, ATTN_WIDTH // KV_WIDTH))
    v_spec = pl.BlockSpec((None, t, KV_WIDTH), lambda i, g: (i, 0, ATTN_WIDTH // KV_WIDTH + 1))
    hm_spec = pl.BlockSpec((None, None, t, d), lambda i, g: (i, g, 0, 0))
    vec_spec = pl.BlockSpec((1, d), lambda i, g: (0, 0))
    out_shape = jax.ShapeDtypeStruct((b, t, ATTN_WIDTH), F32)
    scratch = [pltpu.VMEM((t, gw), BF16), pltpu.VMEM((total, 2 * d), BF16),
               pltpu.VMEM((total, 2 * d), BF16)]
    kern = functools.partial(_attn_kernel, seq_len=t, past_len=past_len, latent=latent)
    if latent:
        ck, cv = cache
        cos4, sin4 = rope
        c_spec = pl.BlockSpec((None, None, past_len, d), lambda i, g: (i, g, 0, 0))
        tab_spec = pl.BlockSpec((t, 2 * d), lambda i, g: (0, 0))
        return pl.pallas_call(
            kern,
            out_shape=out_shape,
            grid=(b, KV_HEADS),
            in_specs=[q_spec, k_spec, v_spec, c_spec, c_spec, tab_spec, tab_spec,
                      vec_spec, vec_spec],
            out_specs=q_spec,
            scratch_shapes=scratch,
            compiler_params=_params("arbitrary", "arbitrary"),
            name="attn_latent",
        )(qkv, qkv, qkv, ck, cv, cos4, sin4, qg, kg)
    hm_shape = jax.ShapeDtypeStruct((b, KV_HEADS, t, d), F32)
    return pl.pallas_call(
        kern,
        out_shape=(out_shape, hm_shape, hm_shape),
        grid=(b, KV_HEADS),
        in_specs=[q_spec, k_spec, v_spec, vec_spec, vec_spec],
        out_specs=(q_spec, hm_spec, hm_spec),
        scratch_shapes=scratch,
        compiler_params=_params("arbitrary", "arbitrary"),
        name="attn_context",
    )(qkv, qkv, qkv, qg, kg)


def _conv_rows(ref_rows, taps, start, n, total):
    cur = ref_rows(start, n)
    width = cur.shape[-1]
    zero = jnp.zeros((1, width), F32)
    prev_row = ref_rows(start - 1, 1) if start > 0 else zero
    next_row = ref_rows(start + n, 1) if start + n < total else zero
    ridx = lax.broadcasted_iota(jnp.int32, (n, 1), 0)
    prev = jnp.where(ridx == 0, prev_row, pltpu.roll(cur, 1, 0))
    nxt = jnp.where(ridx == n - 1, next_row, pltpu.roll(cur, n - 1, 0))
    return prev * taps[0] + cur * taps[1] + nxt * taps[2]


def _round_robin(chains):
    results = [None] * len(chains)
    live = list(range(len(chains)))
    while live:
        for i in list(live):
            try:
                next(chains[i])
            except StopIteration as done:
                results[i] = done.value
                live.remove(i)
    return results


def _rwkv_kernel(*refs, seq_len, has_state):
    n_in = 19 if has_state else 18
    (r_ref, k_ref, v_ref, lora_ref, cr_ref, ck_ref, cv_ref, cl_ref, w0_ref, w2_ref, a0_ref, a2_ref,
     g2_ref, kk_ref, ka_ref, rk_ref, lng_ref, lnb_ref) = refs[:18]
    s0_ref = refs[18] if has_state else None
    y_ref, sfin_ref, rs_ref, ks_ref, vs_ref, ls_ref, yacc_ref, bacc_ref, st_ref = refs[n_in:]
    t = seq_len
    n_steps = t // STEP_ROWS
    n_sub = STEP_ROWS // GROUP
    n_chunks = GROUP // CHUNK
    n = R_HEAD_DIM
    width = 2 * n
    step = pl.program_id(2)
    pair_lanes = [slice(q * width, (q + 1) * width) for q in range(PAIRS_PER_STEP)]

    @pl.when(step == 0)
    def _():
        for blk in range(n_steps):
            start = blk * STEP_ROWS
            rows = slice(start, start + STEP_ROWS)
            for src, cw, dst in ((r_ref, cr_ref, rs_ref), (k_ref, ck_ref, ks_ref),
                                 (v_ref, cv_ref, vs_ref), (lora_ref, cl_ref, ls_ref)):
                taps = [cw[j:j + 1, :] for j in range(3)]
                dst[rows, :] = _conv_rows(lambda s, m, src=src: src[s:s + m, :],
                                          taps, start, STEP_ROWS, t)
        zero = jnp.zeros((n, n), F32)
        for d in range(2):
            for q in range(PAIRS_PER_STEP):
                if has_state:
                    s_a, s_b = s0_ref[d, 2 * q], s0_ref[d, 2 * q + 1]
                    bd = jnp.concatenate([jnp.concatenate([s_a, zero], axis=1),
                                          jnp.concatenate([zero, s_b], axis=1)], axis=0)
                    st_ref[d, q] = bd.T
                else:
                    st_ref[d, q] = jnp.zeros((width, width), F32)

    lane = lax.broadcasted_iota(jnp.int32, (1, width), 1)
    head0 = lane < R_HEAD_DIM

    def per_head(x):
        x0 = jnp.where(head0, x, 0.0)
        return (x0, x - x0)

    def pick(x0, x1):
        return jnp.where(head0, x0, x1)

    def pair_mm(mats, x):
        return pick(*[_mm(m, x) for m in mats])

    def pair_mm2(mats, x, y):
        xy = jnp.concatenate([x, y], axis=1)
        prods = [_mm(m, xy) for m in mats]
        return (pick(*[p[:, :width] for p in prods]), pick(*[p[:, width:] for p in prods]))

    def head_sum(x):
        s0 = jnp.sum(jnp.where(head0, x, 0.0), axis=-1, keepdims=True)
        s1 = jnp.sum(jnp.where(head0, 0.0, x), axis=-1, keepdims=True)
        return jnp.where(head0, s0, s1)

    def scan_rows(d, q, blk, state):
        rev = d == 1
        lanes = pair_lanes[q]
        base = blk * STEP_ROWS
        ri = lax.broadcasted_iota(jnp.int32, (GROUP, GROUP), 0)
        ci = lax.broadcasted_iota(jnp.int32, (GROUP, GROUP), 1)
        same = (ri // CHUNK) == (ci // CHUNK)
        eye = (ri == ci).astype(F32)
        strict = same & ((ci > ri) if rev else (ci < ri))
        incl = same & ((ci >= ri) if rev else (ci <= ri))
        tri = incl.astype(BF16)

        def half_mask(m):
            return ((ri // (2 * m)) == (ci // (2 * m))) & ((ri // m) != (ci // m))

        subs = []
        for s in (range(n_sub - 1, -1, -1) if rev else range(n_sub)):
            rows = pl.ds(pl.multiple_of(base + s * GROUP, GROUP), GROUP)
            lo = ls_ref[rows, :]
            wd_t = jnp.tanh(lo[:, 0:DECAY_RANK])
            z = w0_ref[d:d + 1, lanes] + _mm(wd_t, w2_ref[d, :, lanes])
            subs.append({"rows": rows, "r": rs_ref[rows, lanes], "k": ks_ref[rows, lanes],
                         "v": vs_ref[rows, lanes], "ad": lo[:, DECAY_RANK:DECAY_RANK + ICLR_RANK],
                         "lw": -DECAY_SCALE * _sigmoid(z)})
        yield
        for p in subs:
            p["cum"] = _mm_exact_lhs(tri, p["lw"])
            p["cum_c"] = jnp.concatenate(
                [jnp.broadcast_to(p["cum"][r:r + 1, :], (CHUNK, width))
                 for r in (range(0, GROUP, CHUNK) if rev else range(CHUNK - 1, GROUP, CHUNK))], axis=0)
            a = _sigmoid(a0_ref[d:d + 1, lanes] + _mm(p["ad"], a2_ref[d, :, lanes]))
            p["kd"] = p["k"] * (1.0 + (a - 1.0) * ka_ref[:, lanes])
            kk = p["k"] * kk_ref[:, lanes]
            p["kk"] = kk * lax.rsqrt(head_sum(kk * kk) + 1e-12)
            p["b"] = p["kk"] * a
            bacc_ref[d, p["rows"], lanes] = head_sum(p["r"] * p["kd"] * rk_ref[:, lanes]) * p["v"]
        yield
        for p in subs:
            cum, cum_c = p["cum"], p["cum_c"]
            p["a_t"] = -p["kk"] * jnp.exp(cum - p["lw"])
            p["r_t"] = p["r"] * jnp.exp(cum)
            inv = jnp.exp(-cum)
            p["b_t"] = p["b"] * inv
            p["k_t"] = p["kd"] * inv
            tail = jnp.exp(cum_c - cum)
            p["b_h"] = p["b"] * tail
            p["k_h"] = p["kd"] * tail
            p["p_c"] = jnp.exp(cum_c)
        yield
        for p in subs:
            bk = jnp.concatenate([p["b_t"], p["k_t"]], axis=0)
            xa = [_mm(x, bk, _NT) for x in per_head(p["a_t"])]
            xr = [_mm(x, bk, _NT) for x in per_head(p["r_t"])]
            p["a_ab"] = [jnp.where(strict, m[:, :GROUP], 0.0) for m in xa]
            p["a_ak"] = [jnp.where(strict, m[:, GROUP:], 0.0) for m in xa]
            p["a_rb"] = [jnp.where(incl, m[:, :GROUP], 0.0) for m in xr]
            p["a_rk"] = [jnp.where(incl, m[:, GROUP:], 0.0) for m in xr]
        yield
        for p in subs:
            p["akv"] = pair_mm(p["a_ak"], p["v"])
            p["rkv"] = pair_mm(p["a_rk"], p["v"])
            p["tm"] = [eye + jnp.where(half_mask(1), m, 0.0) for m in p["a_ab"]]
        m_size = 2
        while m_size < CHUNK:
            yield
            off = half_mask(m_size)
            for p in subs:
                p["ot"] = [_mm(jnp.where(off, n_mat, 0.0), tm)
                           for n_mat, tm in zip(p["a_ab"], p["tm"])]
            yield
            for p in subs:
                p["tm"] = [tm + _mm(tm, ot) for tm, ot in zip(p["tm"], p["ot"])]
            m_size *= 2
        yield
        for p in subs:
            p["ta"], p["tk"] = pair_mm2(p["tm"], p["a_t"], p["akv"])
        yield
        cr = [slice(c * CHUNK, (c + 1) * CHUNK) for c in range(n_chunks)]
        for p in subs:
            ra, rk = pair_mm2(p["a_rb"], p["ta"], p["tk"])
            p["r2"] = p["r_t"] + ra
            p["y2"] = p["rkv"] + rk
            p["m_t"] = [eye * p["p_c"][c * CHUNK:c * CHUNK + 1, :]
                        + jnp.where(same, _mm(p["b_h"][cr[c]], p["ta"][cr[c]], _TN), 0.0)
                        for c in range(n_chunks)]
            p["g_t"] = [jnp.where(same, _mm(jnp.concatenate([p["b_h"][cr[c]], p["k_h"][cr[c]]], axis=0),
                                            jnp.concatenate([p["tk"][cr[c]], p["v"][cr[c]]], axis=0),
                                            _TN), 0.0)
                        for c in range(n_chunks)]
        for p in subs:
            ys = [None] * n_chunks
            for c in (range(n_chunks - 1, -1, -1) if rev else range(n_chunks)):
                yield
                ys[c] = _mm(p["r2"][cr[c]], state) + p["y2"][cr[c]]
                state = _mm(p["m_t"][c], state) + p["g_t"][c]
            yacc_ref[d, p["rows"], lanes] = jnp.concatenate(ys, axis=0)
        return state

    chains = [(d, q) for q in range(PAIRS_PER_STEP) for d in range(2)]
    blocks = (step, n_steps - 1 - step)
    new_state = _round_robin([scan_rows(d, q, blocks[d], st_ref[d, q]) for d, q in chains])
    for (d, q), s_new in zip(chains, new_state):
        st_ref[d, q] = s_new

    @pl.when(step == n_steps - 1)
    def _():
        for d, q in chains:
            s_vk = st_ref[d, q].T
            sfin_ref[d, 2 * q] = s_vk[0:n, 0:n]
            sfin_ref[d, 2 * q + 1] = s_vk[n:width, n:width]
        inv_n = 1.0 / n
        for blk in range(n_steps):
            rows = slice(blk * STEP_ROWS, (blk + 1) * STEP_ROWS)
            gd = _sigmoid(ls_ref[rows, DECAY_RANK + ICLR_RANK:LORA_WIDTH]).astype(BF16)
            for lanes in pair_lanes:
                y = yacc_ref[0, rows, lanes] + yacc_ref[1, rows, lanes]
                yc = y - head_sum(y) * inv_n
                yn = yc * lax.rsqrt(head_sum(yc * yc) * inv_n + GN_EPS)
                out = (yn * lng_ref[:, lanes] + lnb_ref[:, lanes]
                       + (bacc_ref[0, rows, lanes] + bacc_ref[1, rows, lanes]))
                y_ref[rows, lanes] = out * _dot(gd, g2_ref[:, lanes].astype(BF16))


def _rwkv(rkv, lora, p, s0):
    n_seq, t, _ = rkv.shape
    has_state = s0 is not None
    n = R_HEAD_DIM
    heads = 2 * PAIRS_PER_STEP
    width = heads * n
    blocks = R_WIDTH // width
    col = lambda off: (lambda s, i, j: (s, 0, off + i))
    vec = lambda off: (lambda s, i, j: (0, off + i))
    dir_mat = lambda s, i, j: (0, 0, i)
    in_specs = [
        pl.BlockSpec((None, t, width), col(0)),
        pl.BlockSpec((None, t, width), col(blocks)),
        pl.BlockSpec((None, t, width), col(2 * blocks)),
        pl.BlockSpec((None, t, LORA_WIDTH), lambda s, i, j: (s, 0, 0)),
        pl.BlockSpec((3, width), vec(0)),
        pl.BlockSpec((3, width), vec(blocks)),
        pl.BlockSpec((3, width), vec(2 * blocks)),
        pl.BlockSpec((3, LORA_WIDTH), lambda s, i, j: (0, 3 * R_WIDTH // LORA_WIDTH)),
        pl.BlockSpec((2, width), vec(0)),
        pl.BlockSpec((2, DECAY_RANK, width), dir_mat),
        pl.BlockSpec((2, width), vec(0)),
        pl.BlockSpec((2, ICLR_RANK, width), dir_mat),
        pl.BlockSpec((GATE_RANK, width), vec(0)),
        pl.BlockSpec((1, width), vec(0)),
        pl.BlockSpec((1, width), vec(0)),
        pl.BlockSpec((1, width), vec(0)),
        pl.BlockSpec((1, width), vec(0)),
        pl.BlockSpec((1, width), vec(0)),
    ]
    conv = p["conv"]
    args = [rkv, rkv, rkv, lora, conv, conv, conv, conv, p["w0"], p["w2"], p["a0"], p["a2"],
            p["g2"], p["k_k"], p["k_a"], p["r_k"], p["ln_g"], p["ln_b"]]
    state_spec = pl.BlockSpec((None, 2, heads, n, n), lambda s, i, j: (s, 0, i, 0, 0))
    if has_state:
        in_specs.append(state_spec)
        args.append(s0)
    seq_buf = pltpu.VMEM((t, width), F32)
    dir_buf = pltpu.VMEM((2, t, width), F32)
    return pl.pallas_call(
        functools.partial(_rwkv_kernel, seq_len=t, has_state=has_state),
        out_shape=(jax.ShapeDtypeStruct((n_seq, t, R_WIDTH), F32),
                   jax.ShapeDtypeStruct((n_seq, 2, R_HEADS, n, n), F32)),
        grid=(n_seq, blocks, t // STEP_ROWS),
        in_specs=in_specs,
        out_specs=(pl.BlockSpec((None, t, width), col(0)), state_spec),
        scratch_shapes=[seq_buf, seq_buf, seq_buf, pltpu.VMEM((t, LORA_WIDTH), F32),
                        dir_buf, dir_buf, pltpu.VMEM((2, PAIRS_PER_STEP, 2 * n, 2 * n), F32)],
        compiler_params=_params("arbitrary", "arbitrary", "arbitrary"),
        name="rwkv7_latent" if has_state else "rwkv7_context",
    )(*args)


def _merge_kernel(x_ref, ao_ref, ro_ref, sga_ref, sgr_ref, mod_ref, g_ref, wab_ref, wrb_ref,
                  wo_ref, x1_ref, h2_ref):
    merged = (sga_ref[...] * _dot(ao_ref[...].astype(BF16), wab_ref[...])
              + sgr_ref[...] * _dot(ro_ref[...].astype(BF16), wrb_ref[...]))
    x1 = x_ref[...] + mod_ref[2:3, :] * _dot(merged.astype(BF16), wo_ref[...])
    x1_ref[...] = x1
    h2 = _rms(x1) * g_ref[...]
    h2_ref[...] = (h2 * (1.0 + mod_ref[4:5, :]) + mod_ref[3:4, :]).astype(BF16)


def _merge(x2d, attn_o, rwkv_o, sga, sgr, mod, norm2_g, wab, wrb, wo, seq_len):
    n_tok = x2d.shape[0]
    tm = PROJ_ROWS
    row = lambda i: (i, 0)
    const = lambda i: (0, 0)
    return pl.pallas_call(
        _merge_kernel,
        out_shape=(jax.ShapeDtypeStruct((n_tok, D_MODEL), F32),
                   jax.ShapeDtypeStruct((n_tok, D_MODEL), BF16)),
        grid=(n_tok // tm,),
        in_specs=[
            pl.BlockSpec((tm, D_MODEL), row),
            pl.BlockSpec((tm, ATTN_WIDTH), row),
            pl.BlockSpec((tm, R_WIDTH), row),
            pl.BlockSpec((tm, D_MODEL), row),
            pl.BlockSpec((tm, D_MODEL), row),
            pl.BlockSpec((None, N_MOD, D_MODEL), lambda i: ((i * tm) // seq_len, 0, 0)),
            pl.BlockSpec((1, D_MODEL), const),
            pl.BlockSpec((ATTN_WIDTH, D_MODEL), const),
            pl.BlockSpec((R_WIDTH, D_MODEL), const),
            pl.BlockSpec((D_MODEL, D_MODEL), const),
        ],
        out_specs=(pl.BlockSpec((tm, D_MODEL), row), pl.BlockSpec((tm, D_MODEL), row)),
        compiler_params=_params("arbitrary"),
        name="merge_out_proj",
    )(x2d, attn_o, rwkv_o, sga, sgr, mod, norm2_g, wab, wrb, wo)


def _ffn_kernel(h2_ref, x1_ref, mod_ref, wu_ref, wg_ref, cu_ref, cg_ref, wd_ref, fg_ref,
                y_ref, acc_ref, upu_ref, upg_ref, *, seq_len):
    i = pl.program_id(0)
    j = pl.program_id(1)
    rows = h2_ref.shape[0]
    pad = FFN_PAD_ROWS
    rb = FFN_CONV_ROWS
    tile = 8

    @pl.when((i == 0) & (j == 0))
    def _():
        for ref in (upu_ref, upg_ref):
            ref[0:pad, :] = jnp.zeros((pad, ref.shape[1]), F32)
            ref[pad + rows:pad + rows + pad, :] = jnp.zeros((pad, ref.shape[1]), F32)

    @pl.when(j == 0)
    def _():
        acc_ref[...] = jnp.zeros(acc_ref.shape, F32)

    h2 = h2_ref[...]
    upu_ref[pad:pad + rows, :] = _dot(h2, wu_ref[...])
    upg_ref[pad:pad + rows, :] = _dot(h2, wg_ref[...])
    row0 = lax.broadcasted_iota(jnp.int32, (tile, 1), 0) == 0
    row7 = lax.broadcasted_iota(jnp.int32, (tile, 1), 0) == tile - 1

    taps_u = [cu_ref[k:k + 1, :] for k in range(3)]
    taps_g = [cg_ref[k:k + 1, :] for k in range(3)]

    def zero_rows(x, tile_offsets, row_mask):
        parts, done = [], 0
        for off in tile_offsets:
            parts += [x[done:off], jnp.where(row_mask, 0.0, x[off:off + tile])]
            done = off + tile
        parts.append(x[done:])
        return jnp.concatenate([p for p in parts if p.shape[0]], axis=0)

    def conv(up_ref, taps, r0):
        base = pad + r0
        cur = up_ref[base:base + rb, :]
        prev = up_ref[base - 1:base - 1 + rb, :]
        nxt = up_ref[base + 1:base + 1 + rb, :]
        first_seq = -(-r0 // seq_len) * seq_len
        starts = [s - r0 for s in range(first_seq, r0 + rb, seq_len)]
        ends = [e - r0 for e in range(first_seq if first_seq > r0 else first_seq + seq_len,
                                      r0 + rb + 1, seq_len)]
        prev = zero_rows(prev, starts, row0)
        nxt = zero_rows(nxt, [e - tile for e in ends], row7)
        return prev * taps[0] + cur * taps[1] + nxt * taps[2]

    acts = []
    for r0 in range(0, rows, rb):
        up_u = conv(upu_ref, taps_u, r0)
        up_g = conv(upg_ref, taps_g, r0)
        acts.append((up_g * _sigmoid(up_g) * up_u).astype(BF16))
    acc_ref[...] += _dot(jnp.concatenate(acts, axis=0), wd_ref[...])

    @pl.when(j == pl.num_programs(1) - 1)
    def _():
        x2 = x1_ref[...] + mod_ref[...] * acc_ref[...]
        y_ref[...] = _rms(x2) * fg_ref[...]


def _ffn(h2, x1, gate2, ffn_up, ffn_conv, ffn_down, final_g, seq_len):
    n_tok = h2.shape[0]
    tm = FFN_ROWS
    tn = FFN_TILE
    nt = D_FF // tn
    assert tm % seq_len == 0 or seq_len % tm == 0
    gate_map = lambda i, j: ((i * tm) // seq_len, 0, 0)
    return pl.pallas_call(
        functools.partial(_ffn_kernel, seq_len=seq_len),
        out_shape=jax.ShapeDtypeStruct((n_tok, D_MODEL), F32),
        grid=(n_tok // tm, nt),
        in_specs=[
            pl.BlockSpec((tm, D_MODEL), lambda i, j: (i, 0)),
            pl.BlockSpec((tm, D_MODEL), lambda i, j: (i, 0)),
            pl.BlockSpec((None, 1, D_MODEL), gate_map),
            pl.BlockSpec((D_MODEL, tn), lambda i, j: (0, j)),
            pl.BlockSpec((D_MODEL, tn), lambda i, j: (0, nt + j)),
            pl.BlockSpec((3, tn), lambda i, j: (0, j)),
            pl.BlockSpec((3, tn), lambda i, j: (0, nt + j)),
            pl.BlockSpec((tn, D_MODEL), lambda i, j: (j, 0)),
            pl.BlockSpec((1, D_MODEL), lambda i, j: (0, 0)),
        ],
        out_specs=pl.BlockSpec((tm, D_MODEL), lambda i, j: (i, 0)),
        scratch_shapes=[pltpu.VMEM((tm, D_MODEL), F32),
                        pltpu.VMEM((tm + 2 * FFN_PAD_ROWS, tn), F32),
                        pltpu.VMEM((tm + 2 * FFN_PAD_ROWS, tn), F32)],
        compiler_params=_params("arbitrary", "arbitrary"),
        name="conv_ffn",
    )(h2, x1, gate2, ffn_up, ffn_up, ffn_conv, ffn_conv, ffn_down, final_g)


def _layer_path(x, mod, w, rope, cache, s0):
    b, t, _ = x.shape
    x2d = x.reshape(b * t, D_MODEL)
    qkv, rkv, lora, sga, sgr = _in_projection(x2d, mod, w["norm1_g"], w["w_in"], t)
    qkv = qkv.reshape(b, t, ATTN_WIDTH + 2 * KV_WIDTH)
    if rope is None:
        attn_o, k_norm, v_hm = _attention(qkv, w["q_norm_g"], w["k_norm_g"])
    else:
        attn_o = _attention(qkv, w["q_norm_g"], w["k_norm_g"], cache=cache, rope=rope)
        k_norm = v_hm = None
    rwkv_o, s_fin = _rwkv(rkv.reshape(b, t, 3 * R_WIDTH), lora.reshape(b, t, LORA_WIDTH),
                          w["rwkv"], s0)
    x1, h2 = _merge(x2d, attn_o.reshape(b * t, ATTN_WIDTH), rwkv_o.reshape(b * t, R_WIDTH),
                    sga, sgr, mod, w["norm2_g"], w["w_attn_br"], w["w_rwkv_br"], w["w_out"], t)
    gate2 = mod[:, 5:6, :]
    y = _ffn(h2, x1, gate2, w["ffn_up"], w["ffn_conv"], w["ffn_down"], w["final_norm_g"], t)
    return y.reshape(b, t, D_MODEL), k_norm, v_hm, s_fin


def _rope_tables(rows):
    row = jnp.repeat(jnp.arange(rows), GRID_W).astype(F32)
    col = jnp.tile(jnp.arange(GRID_W), rows).astype(F32)
    inv = 1.0 / (ROPE_THETA ** (jnp.arange(ROPE_PAIRS, dtype=F32) / ROPE_PAIRS))
    ang = jnp.concatenate([row[:, None] * inv, col[:, None] * inv], axis=-1)
    cos, sin = jnp.cos(ang), jnp.sin(ang)
    return jnp.concatenate([cos, cos] * 2, axis=-1), jnp.concatenate([-sin, sin] * 2, axis=-1)


def kernel(x_prompt, x_sample, cache_k, cache_v, state_rwkv, c, c_ctx, w_ada, b_ada, norm1_g, w_in, q_norm_g, k_norm_g, rwkv_conv, rwkv_w0, rwkv_w2, rwkv_a0, rwkv_a2, rwkv_g2, rwkv_k_k, rwkv_k_a, rwkv_r_k, rwkv_ln_g, rwkv_ln_b, w_attn_br, w_rwkv_br, w_out, norm2_g, ffn_up, ffn_conv, ffn_down, final_norm_g):
    depth = w_in.shape[0]
    assert depth == 1, "single trunk layer"
    l = 0
    n_ctx = x_prompt.shape[0]
    n_lat = x_sample.shape[0]

    cc = jnp.concatenate([c_ctx[None, :], c, jnp.zeros((8 - 1 - n_lat, D_MODEL), F32)], axis=0)
    mod_all = _modulation(cc, w_ada[l], b_ada[l][None, :])
    mod_ctx = jnp.broadcast_to(mod_all[0:1].reshape(1, N_MOD, D_MODEL), (n_ctx, N_MOD, D_MODEL))
    mod_lat = mod_all[1:1 + n_lat].reshape(n_lat, N_MOD, D_MODEL)

    rwkv_p = {
        "conv": rwkv_conv[l], "w0": rwkv_w0[l], "w2": rwkv_w2[l], "a0": rwkv_a0[l],
        "a2": rwkv_a2[l], "g2": rwkv_g2[l], "k_k": rwkv_k_k[l][None, :],
        "k_a": rwkv_k_a[l][None, :], "r_k": rwkv_r_k[l].reshape(1, R_WIDTH),
        "ln_g": rwkv_ln_g[l][None, :], "ln_b": rwkv_ln_b[l][None, :],
    }
    w = {
        "norm1_g": norm1_g[l][None, :], "w_in": w_in[l].astype(BF16),
        "q_norm_g": q_norm_g[l][None, :], "k_norm_g": k_norm_g[l][None, :],
        "rwkv": rwkv_p,
        "w_attn_br": w_attn_br[l].astype(BF16), "w_rwkv_br": w_rwkv_br[l].astype(BF16),
        "w_out": w_out[l].astype(BF16), "norm2_g": norm2_g[l][None, :],
        "ffn_up": ffn_up[l].astype(BF16), "ffn_conv": ffn_conv[l],
        "ffn_down": ffn_down[l].astype(BF16), "final_norm_g": final_norm_g[None, :],
    }

    y_prompt, k_ctx, v_ctx, s_ctx = _layer_path(x_prompt, mod_ctx, w, None, None, None)
    rope = _rope_tables(x_sample.shape[1] // GRID_W)
    y_sample, _, _, _ = _layer_path(x_sample, mod_lat, w, rope, (cache_k[:, l], cache_v[:, l]),
                                    state_rwkv[:, l])
    return (y_prompt, y_sample, k_ctx[:, None], v_ctx[:, None], s_ctx[:, None])
```

```python
import functools

import jax
import jax.numpy as jnp
import numpy as np
from jax import lax
from jax.experimental import pallas as pl
from jax.experimental.pallas import tpu as pltpu

D_MODEL = 1024
GRID_W = 64
N_HEADS = 8
KV_HEADS = 2
Q_PER_KV = N_HEADS // KV_HEADS
HEAD_DIM = 64
ATTN_WIDTH = N_HEADS * HEAD_DIM
KV_WIDTH = KV_HEADS * HEAD_DIM
ROPE_THETA = 10000.0
ROPE_PAIRS = HEAD_DIM // 4
R_HEADS = 8
R_HEAD_DIM = 64
R_WIDTH = R_HEADS * R_HEAD_DIM
DECAY_RANK = 64
ICLR_RANK = 64
GATE_RANK = 128
LORA_WIDTH = DECAY_RANK + ICLR_RANK + GATE_RANK
R_IN_WIDTH = 3 * R_WIDTH + LORA_WIDTH
D_FF = 2816
N_MOD = 6
NORM_EPS = 1e-6
GN_EPS = 64e-5
DECAY_SCALE = float(np.exp(-0.5))

F32 = jnp.float32
BF16 = jnp.bfloat16

VMEM_LIMIT_BYTES = 56 * 1024 * 1024
PROJ_ROWS = 256
FFN_ROWS = 1024
FFN_TILE = 256
FFN_PAD_ROWS = 8
FFN_CONV_ROWS = 256
Q_ROWS = 128
KEY_CHUNK = 256
STEP_ROWS = 256
GROUP = 128
PAIRS_PER_STEP = 2
CHUNK = 64

_NN = (((1,), (0,)), ((), ()))
_NT = (((1,), (1,)), ((), ()))
_TN = (((0,), (0,)), ((), ()))


def _dg(a, b, dims):
    return lax.dot_general(a, b, dims, preferred_element_type=F32)


def _dot(a, b):
    return _dg(a, b, _NN)


def _mm(a, b, dims=_NN):
    return _dg(a.astype(BF16), b.astype(BF16), dims)


def _mm_exact_lhs(m, x):
    hi = x.astype(BF16)
    r1 = x - hi.astype(F32)
    mid = r1.astype(BF16)
    lo = (r1 - mid.astype(F32)).astype(BF16)
    return _dot(m, hi) + (_dot(m, mid) + _dot(m, lo))


def _sigmoid(x):
    return 1.0 / (1.0 + jnp.exp(-x))


def _rms(x):
    return x * lax.rsqrt(jnp.mean(x * x, axis=-1, keepdims=True) + NORM_EPS)


def _params(*semantics):
    return pltpu.CompilerParams(dimension_semantics=semantics,
                                vmem_limit_bytes=VMEM_LIMIT_BYTES)


def _mod_kernel(c_ref, w_ref, b_ref, o_ref):
    c = c_ref[...]
    s = (c * _sigmoid(c)).astype(BF16)
    o_ref[...] = _dot(s, w_ref[...].astype(BF16)) + b_ref[...]


def _modulation(cc, w_ada, b_ada):
    rows = cc.shape[0]
    n = w_ada.shape[1]
    tile = 1024
    return pl.pallas_call(
        _mod_kernel,
        out_shape=jax.ShapeDtypeStruct((rows, n), F32),
        grid=(n // tile,),
        in_specs=[
            pl.BlockSpec((rows, D_MODEL), lambda j: (0, 0)),
            pl.BlockSpec((D_MODEL, tile), lambda j: (0, j)),
            pl.BlockSpec((1, tile), lambda j: (0, j)),
        ],
        out_specs=pl.BlockSpec((rows, tile), lambda j: (0, j)),
        compiler_params=_params("arbitrary"),
        name="adaln_mod",
    )(cc, w_ada, b_ada)


def _inproj_kernel(x_ref, mod_ref, g_ref, w_ref, qkv_ref, rkv_ref, lora_ref, sga_ref, sgr_ref):
    x = x_ref[...]
    h = _rms(x) * g_ref[...]
    h = h * (1.0 + mod_ref[1:2, :]) + mod_ref[0:1, :]
    hb = h.astype(BF16)
    c0 = ATTN_WIDTH + 2 * KV_WIDTH
    c1 = c0 + 3 * R_WIDTH
    c2 = c1 + LORA_WIDTH
    c3 = c2 + D_MODEL
    qkv_ref[...] = _dot(hb, w_ref[:, 0:c0])
    rkv_ref[...] = _dot(hb, w_ref[:, c0:c1])
    lora_ref[...] = _dot(hb, w_ref[:, c1:c2])
    sga_ref[...] = _sigmoid(_dot(hb, w_ref[:, c2:c3])).astype(BF16)
    sgr_ref[...] = _sigmoid(_dot(hb, w_ref[:, c3:c3 + D_MODEL])).astype(BF16)


def _in_projection(x2d, mod, norm1_g, w_in_bf16, seq_len):
    n_tok = x2d.shape[0]
    tm = PROJ_ROWS
    in_width = w_in_bf16.shape[1]
    widths = (ATTN_WIDTH + 2 * KV_WIDTH, 3 * R_WIDTH, LORA_WIDTH, D_MODEL, D_MODEL)
    dtypes = (F32, F32, F32, BF16, BF16)
    row = lambda i: (i, 0)
    return pl.pallas_call(
        _inproj_kernel,
        out_shape=tuple(jax.ShapeDtypeStruct((n_tok, w), dt) for w, dt in zip(widths, dtypes)),
        grid=(n_tok // tm,),
        in_specs=[
            pl.BlockSpec((tm, D_MODEL), row),
            pl.BlockSpec((None, N_MOD, D_MODEL), lambda i: ((i * tm) // seq_len, 0, 0)),
            pl.BlockSpec((1, D_MODEL), lambda i: (0, 0)),
            pl.BlockSpec((D_MODEL, in_width), lambda i: (0, 0)),
        ],
        out_specs=tuple(pl.BlockSpec((tm, w), row) for w in widths),
        compiler_params=_params("arbitrary"),
        name="in_proj",
    )(x2d, mod, norm1_g, w_in_bf16)


def _attn_kernel(*refs, seq_len, past_len, latent):
    if latent:
        (q_ref, k_ref, v_ref, ck_ref, cv_ref, cos_ref, sin_ref, qg_ref, kg_ref,
         o_ref, qs_ref, kd_ref, vd_ref) = refs
    else:
        (q_ref, k_ref, v_ref, qg_ref, kg_ref, o_ref, kn_ref, vn_ref, qs_ref, kd_ref, vd_ref) = refs
    t = seq_len
    d = HEAD_DIM
    width = 2 * d
    total = t + past_len
    group = pl.program_id(1)
    lane = lax.broadcasted_iota(jnp.int32, (1, width), 1)
    head_of_lane = lane // d
    low_half = (lane % d) < d // 2
    inv_d = 1.0 / d

    def pair_norm(x, gain):
        sq = x * x
        s0 = jnp.sum(jnp.where(head_of_lane == 0, sq, 0.0), axis=-1, keepdims=True)
        s1 = jnp.sum(jnp.where(head_of_lane == 0, 0.0, sq), axis=-1, keepdims=True)
        ms = jnp.where(head_of_lane == 0, s0, s1) * inv_d
        y = x * lax.rsqrt(ms + NORM_EPS) * gain
        if latent:
            partner = jnp.where(low_half, pltpu.roll(y, width - d // 2, 1), pltpu.roll(y, d // 2, 1))
            y = y * cos_ref[...] + partner * sin_ref[...]
        return y

    def this_group(x):
        return jnp.where(head_of_lane == group, x, pltpu.roll(x, d, 1))

    kg2 = jnp.concatenate([kg_ref[...], kg_ref[...]], axis=-1)
    qg2 = jnp.concatenate([qg_ref[...], qg_ref[...]], axis=-1)
    v_dup = this_group(v_ref[...])
    if latent:
        kd_ref[0:t, :] = this_group(pair_norm(k_ref[...], kg2)).astype(BF16)
        ck = ck_ref[...].astype(BF16)
        kd_ref[t:total, :] = jnp.concatenate([ck, ck], axis=-1)
        vd_ref[0:t, :] = v_dup.astype(BF16)
        cv = cv_ref[...].astype(BF16)
        vd_ref[t:total, :] = jnp.concatenate([cv, cv], axis=-1)
    else:
        k_dup = this_group(pair_norm(k_ref[...], kg2))
        kn_ref[...] = k_dup[:, 0:d]
        vn_ref[...] = v_dup[:, 0:d]
        kd_ref[...] = k_dup.astype(BF16)
        vd_ref[...] = v_dup.astype(BF16)
    scale = d ** -0.5
    for j in range(Q_PER_KV // 2):
        cols = slice(j * width, (j + 1) * width)
        qs_ref[:, cols] = (pair_norm(q_ref[:, cols], qg2) * scale).astype(BF16)
    n_chunks = total // KEY_CHUNK

    def body(qb, carry):
        r0 = pl.multiple_of(qb * Q_ROWS, Q_ROWS)
        outs = []
        for j in range(Q_PER_KV // 2):
            qp = qs_ref[pl.ds(r0, Q_ROWS), j * width:(j + 1) * width]
            heads = []
            for h in range(2):
                qm = jnp.where(head_of_lane == h, qp, jnp.zeros_like(qp))
                m = l = acc = None
                for c in range(n_chunks):
                    keys = slice(c * KEY_CHUNK, (c + 1) * KEY_CHUNK)
                    s = _dg(qm, kd_ref[keys, :], _NT)
                    m_c = jnp.max(s, axis=-1, keepdims=True)
                    if c == 0:
                        m = m_c
                        p = jnp.exp(s - m)
                        l = jnp.sum(p, axis=-1, keepdims=True)
                        acc = _dot(p.astype(BF16), vd_ref[keys, :])
                    else:
                        m_new = jnp.maximum(m, m_c)
                        alpha = jnp.exp(m - m_new)
                        p = jnp.exp(s - m_new)
                        l = alpha * l + jnp.sum(p, axis=-1, keepdims=True)
                        acc = alpha * acc + _dot(p.astype(BF16), vd_ref[keys, :])
                        m = m_new
                heads.append(acc / l)
            outs.append(jnp.where(head_of_lane == 0, heads[0], heads[1]))
        o_ref[pl.ds(r0, Q_ROWS), :] = jnp.concatenate(outs, axis=-1).astype(BF16)
        return carry

    lax.fori_loop(0, t // Q_ROWS, body, 0)


def _attention(qkv, qg, kg, cache=None, rope=None):
    b, t, _ = qkv.shape
    d = HEAD_DIM
    latent = cache is not None
    past_len = cache[0].shape[2] if latent else 0
    total = t + past_len
    assert total % KEY_CHUNK == 0 and KV_WIDTH == 2 * d
    gw = Q_PER_KV * d
    q_spec = pl.BlockSpec((None, t, gw), lambda i, g: (i, 0, g))
    k_spec = pl.BlockSpec((None, t, KV_WIDTH), lambda i, g: (i, 0, ATTN_WIDTH // KV_WIDTH))
    v_spec = pl.BlockSpec((None, t, KV_WIDTH), lambda i, g: (i, 0, ATTN_WIDTH // KV_WIDTH + 1))
    hm_spec = pl.BlockSpec((None, None, t, d), lambda i, g: (i, g, 0, 0))
    vec_spec = pl.BlockSpec((1, d), lambda i, g: (0, 0))
    out_shape = jax.ShapeDtypeStruct((b, t, ATTN_WIDTH), BF16)
    scratch = [pltpu.VMEM((t, gw), BF16), pltpu.VMEM((total, 2 * d), BF16),
               pltpu.VMEM((total, 2 * d), BF16)]
    kern = functools.partial(_attn_kernel, seq_len=t, past_len=past_len, latent=latent)
    if latent:
        ck, cv = cache
        cos4, sin4 = rope
        c_spec = pl.BlockSpec((None, None, past_len, d), lambda i, g: (i, g, 0, 0))
        tab_spec = pl.BlockSpec((t, 2 * d), lambda i, g: (0, 0))
        return pl.pallas_call(
            kern,
            out_shape=out_shape,
            grid=(b, KV_HEADS),
            in_specs=[q_spec, k_spec, v_spec, c_spec, c_spec, tab_spec, tab_spec,
                      vec_spec, vec_spec],
            out_specs=q_spec,
            scratch_shapes=scratch,
            compiler_params=_params("arbitrary", "arbitrary"),
            name="attn_latent",
        )(qkv, qkv, qkv, ck, cv, cos4, sin4, qg, kg)
    hm_shape = jax.ShapeDtypeStruct((b, KV_HEADS, t, d), F32)
    return pl.pallas_call(
        kern,
        out_shape=(out_shape, hm_shape, hm_shape),
        grid=(b, KV_HEADS),
        in_specs=[q_spec, k_spec, v_spec, vec_spec, vec_spec],
        out_specs=(q_spec, hm_spec, hm_spec),
        scratch_shapes=scratch,
        compiler_params=_params("arbitrary", "arbitrary"),
        name="attn_context",
    )(qkv, qkv, qkv, qg, kg)


def _conv_rows(ref_rows, taps, start, n, total):
    cur = ref_rows(start, n)
    width = cur.shape[-1]
    zero = jnp.zeros((1, width), F32)
    prev_row = ref_rows(start - 1, 1) if start > 0 else zero
    next_row = ref_rows(start + n, 1) if start + n < total else zero
    ridx = lax.broadcasted_iota(jnp.int32, (n, 1), 0)
    prev = jnp.where(ridx == 0, prev_row, pltpu.roll(cur, 1, 0))
    nxt = jnp.where(ridx == n - 1, next_row, pltpu.roll(cur, n - 1, 0))
    return prev * taps[0] + cur * taps[1] + nxt * taps[2]


def _round_robin(chains):
    results = [None] * len(chains)
    live = list(range(len(chains)))
    while live:
        for i in list(live):
            try:
                next(chains[i])
            except StopIteration as done:
                results[i] = done.value
                live.remove(i)
    return results


def _rwkv_kernel(*refs, seq_len, has_state):
    n_in = 19 if has_state else 18
    (r_ref, k_ref, v_ref, lora_ref, cr_ref, ck_ref, cv_ref, cl_ref, w0_ref, w2_ref, a0_ref, a2_ref,
     g2_ref, kk_ref, ka_ref, rk_ref, lng_ref, lnb_ref) = refs[:18]
    s0_ref = refs[18] if has_state else None
    y_ref, sfin_ref, rs_ref, ks_ref, vs_ref, ls_ref, yacc_ref, bacc_ref, st_ref = refs[n_in:]
    t = seq_len
    n_steps = t // STEP_ROWS
    n_sub = STEP_ROWS // GROUP
    n_chunks = GROUP // CHUNK
    n = R_HEAD_DIM
    width = 2 * n
    step = pl.program_id(2)
    pair_lanes = [slice(q * width, (q + 1) * width) for q in range(PAIRS_PER_STEP)]

    @pl.when(step == 0)
    def _():
        for blk in range(n_steps):
            start = blk * STEP_ROWS
            rows = slice(start, start + STEP_ROWS)
            for src, cw, dst in ((r_ref, cr_ref, rs_ref), (k_ref, ck_ref, ks_ref),
                                 (v_ref, cv_ref, vs_ref), (lora_ref, cl_ref, ls_ref)):
                taps = [cw[j:j + 1, :] for j in range(3)]
                dst[rows, :] = _conv_rows(lambda s, m, src=src: src[s:s + m, :],
                                          taps, start, STEP_ROWS, t)
        zero = jnp.zeros((n, n), F32)
        for d in range(2):
            for q in range(PAIRS_PER_STEP):
                if has_state:
                    s_a, s_b = s0_ref[d, 2 * q], s0_ref[d, 2 * q + 1]
                    bd = jnp.concatenate([jnp.concatenate([s_a, zero], axis=1),
                                          jnp.concatenate([zero, s_b], axis=1)], axis=0)
                    st_ref[d, q] = bd.T
                else:
                    st_ref[d, q] = jnp.zeros((width, width), F32)

    lane = lax.broadcasted_iota(jnp.int32, (1, width), 1)
    head0 = lane < R_HEAD_DIM

    def per_head(x):
        x0 = jnp.where(head0, x, 0.0)
        return (x0, x - x0)

    def pick(x0, x1):
        return jnp.where(head0, x0, x1)

    def pair_mm(mats, x):
        return pick(*[_mm(m, x) for m in mats])

    def pair_mm2(mats, x, y):
        xy = jnp.concatenate([x, y], axis=1)
        prods = [_mm(m, xy) for m in mats]
        return (pick(*[p[:, :width] for p in prods]), pick(*[p[:, width:] for p in prods]))

    def head_sum(x):
        s0 = jnp.sum(jnp.where(head0, x, 0.0), axis=-1, keepdims=True)
        s1 = jnp.sum(jnp.where(head0, 0.0, x), axis=-1, keepdims=True)
        return jnp.where(head0, s0, s1)

    def scan_rows(d, q, blk, state):
        rev = d == 1
        lanes = pair_lanes[q]
        base = blk * STEP_ROWS
        ri = lax.broadcasted_iota(jnp.int32, (GROUP, GROUP), 0)
        ci = lax.broadcasted_iota(jnp.int32, (GROUP, GROUP), 1)
        same = (ri // CHUNK) == (ci // CHUNK)
        eye = (ri == ci).astype(F32)
        strict = same & ((ci > ri) if rev else (ci < ri))
        incl = same & ((ci >= ri) if rev else (ci <= ri))
        tri = incl.astype(BF16)

        def half_mask(m):
            return ((ri // (2 * m)) == (ci // (2 * m))) & ((ri // m) != (ci // m))

        subs = []
        for s in (range(n_sub - 1, -1, -1) if rev else range(n_sub)):
            rows = pl.ds(pl.multiple_of(base + s * GROUP, GROUP), GROUP)
            lo = ls_ref[rows, :]
            wd_t = jnp.tanh(lo[:, 0:DECAY_RANK])
            z = w0_ref[d:d + 1, lanes] + _mm(wd_t, w2_ref[d, :, lanes])
            subs.append({"rows": rows, "r": rs_ref[rows, lanes], "k": ks_ref[rows, lanes],
                         "v": vs_ref[rows, lanes], "ad": lo[:, DECAY_RANK:DECAY_RANK + ICLR_RANK],
                         "lw": -DECAY_SCALE * _sigmoid(z)})
        yield
        for p in subs:
            p["cum"] = _mm_exact_lhs(tri, p["lw"])
            p["cum_c"] = jnp.concatenate(
                [jnp.broadcast_to(p["cum"][r:r + 1, :], (CHUNK, width))
                 for r in (range(0, GROUP, CHUNK) if rev else range(CHUNK - 1, GROUP, CHUNK))], axis=0)
            a = _sigmoid(a0_ref[d:d + 1, lanes] + _mm(p["ad"], a2_ref[d, :, lanes]))
            p["kd"] = p["k"] * (1.0 + (a - 1.0) * ka_ref[:, lanes])
            kk = p["k"] * kk_ref[:, lanes]
            p["kk"] = kk * lax.rsqrt(head_sum(kk * kk) + 1e-12)
            p["b"] = p["kk"] * a
            bacc_ref[d, p["rows"], lanes] = head_sum(p["r"] * p["kd"] * rk_ref[:, lanes]) * p["v"]
        yield
        for p in subs:
            cum, cum_c = p["cum"], p["cum_c"]
            p["a_t"] = -p["kk"] * jnp.exp(cum - p["lw"])
            p["r_t"] = p["r"] * jnp.exp(cum)
            inv = jnp.exp(-cum)
            p["b_t"] = p["b"] * inv
            p["k_t"] = p["kd"] * inv
            tail = jnp.exp(cum_c - cum)
            p["b_h"] = p["b"] * tail
            p["k_h"] = p["kd"] * tail
            p["p_c"] = jnp.exp(cum_c)
        yield
        for p in subs:
            bk = jnp.concatenate([p["b_t"], p["k_t"]], axis=0)
            xa = [_mm(x, bk, _NT) for x in per_head(p["a_t"])]
            xr = [_mm(x, bk, _NT) for x in per_head(p["r_t"])]
            p["a_ab"] = [jnp.where(strict, m[:, :GROUP], 0.0) for m in xa]
            p["a_ak"] = [jnp.where(strict, m[:, GROUP:], 0.0) for m in xa]
            p["a_rb"] = [jnp.where(incl, m[:, :GROUP], 0.0) for m in xr]
            p["a_rk"] = [jnp.where(incl, m[:, GROUP:], 0.0) for m in xr]
        yield
        for p in subs:
            p["akv"] = pair_mm(p["a_ak"], p["v"])
            p["rkv"] = pair_mm(p["a_rk"], p["v"])
            p["tm"] = [eye + jnp.where(half_mask(1), m, 0.0) for m in p["a_ab"]]
        m_size = 2
        while m_size < CHUNK:
            yield
            off = half_mask(m_size)
            for p in subs:
                p["ot"] = [_mm(jnp.where(off, n_mat, 0.0), tm)
                           for n_mat, tm in zip(p["a_ab"], p["tm"])]
            yield
            for p in subs:
                p["tm"] = [tm + _mm(tm, ot) for tm, ot in zip(p["tm"], p["ot"])]
            m_size *= 2
        yield
        for p in subs:
            p["ta"], p["tk"] = pair_mm2(p["tm"], p["a_t"], p["akv"])
        yield
        cr = [slice(c * CHUNK, (c + 1) * CHUNK) for c in range(n_chunks)]
        for p in subs:
            ra, rk = pair_mm2(p["a_rb"], p["ta"], p["tk"])
            p["r2"] = p["r_t"] + ra
            p["y2"] = p["rkv"] + rk
            p["m_t"] = [eye * p["p_c"][c * CHUNK:c * CHUNK + 1, :]
                        + jnp.where(same, _mm(p["b_h"][cr[c]], p["ta"][cr[c]], _TN), 0.0)
                        for c in range(n_chunks)]
            p["g_t"] = [jnp.where(same, _mm(jnp.concatenate([p["b_h"][cr[c]], p["k_h"][cr[c]]], axis=0),
                                            jnp.concatenate([p["tk"][cr[c]], p["v"][cr[c]]], axis=0),
                                            _TN), 0.0)
                        for c in range(n_chunks)]
        for p in subs:
            ys = [None] * n_chunks
            for c in (range(n_chunks - 1, -1, -1) if rev else range(n_chunks)):
                yield
                ys[c] = _mm(p["r2"][cr[c]], state) + p["y2"][cr[c]]
                state = _mm(p["m_t"][c], state) + p["g_t"][c]
            yacc_ref[d, p["rows"], lanes] = jnp.concatenate(ys, axis=0)
        return state

    chains = [(d, q) for q in range(PAIRS_PER_STEP) for d in range(2)]
    blocks = (step, n_steps - 1 - step)
    new_state = _round_robin([scan_rows(d, q, blocks[d], st_ref[d, q]) for d, q in chains])
    for (d, q), s_new in zip(chains, new_state):
        st_ref[d, q] = s_new

    @pl.when(step == n_steps - 1)
    def _():
        for d, q in chains:
            s_vk = st_ref[d, q].T
            sfin_ref[d, 2 * q] = s_vk[0:n, 0:n]
            sfin_ref[d, 2 * q + 1] = s_vk[n:width, n:width]
        inv_n = 1.0 / n
        for blk in range(n_steps):
            rows = slice(blk * STEP_ROWS, (blk + 1) * STEP_ROWS)
            gd = _sigmoid(ls_ref[rows, DECAY_RANK + ICLR_RANK:LORA_WIDTH]).astype(BF16)
            for lanes in pair_lanes:
                y = yacc_ref[0, rows, lanes] + yacc_ref[1, rows, lanes]
                yc = y - head_sum(y) * inv_n
                yn = yc * lax.rsqrt(head_sum(yc * yc) * inv_n + GN_EPS)
                out = (yn * lng_ref[:, lanes] + lnb_ref[:, lanes]
                       + (bacc_ref[0, rows, lanes] + bacc_ref[1, rows, lanes]))
                y_ref[rows, lanes] = (out * _dot(gd, g2_ref[:, lanes].astype(BF16))).astype(BF16)


def _rwkv(rkv, lora, p, s0):
    n_seq, t, _ = rkv.shape
    has_state = s0 is not None
    n = R_HEAD_DIM
    heads = 2 * PAIRS_PER_STEP
    width = heads * n
    blocks = R_WIDTH // width
    col = lambda off: (lambda s, i, j: (s, 0, off + i))
    vec = lambda off: (lambda s, i, j: (0, off + i))
    dir_mat = lambda s, i, j: (0, 0, i)
    in_specs = [
        pl.BlockSpec((None, t, width), col(0)),
        pl.BlockSpec((None, t, width), col(blocks)),
        pl.BlockSpec((None, t, width), col(2 * blocks)),
        pl.BlockSpec((None, t, LORA_WIDTH), lambda s, i, j: (s, 0, 0)),
        pl.BlockSpec((3, width), vec(0)),
        pl.BlockSpec((3, width), vec(blocks)),
        pl.BlockSpec((3, width), vec(2 * blocks)),
        pl.BlockSpec((3, LORA_WIDTH), lambda s, i, j: (0, 3 * R_WIDTH // LORA_WIDTH)),
        pl.BlockSpec((2, width), vec(0)),
        pl.BlockSpec((2, DECAY_RANK, width), dir_mat),
        pl.BlockSpec((2, width), vec(0)),
        pl.BlockSpec((2, ICLR_RANK, width), dir_mat),
        pl.BlockSpec((GATE_RANK, width), vec(0)),
        pl.BlockSpec((1, width), vec(0)),
        pl.BlockSpec((1, width), vec(0)),
        pl.BlockSpec((1, width), vec(0)),
        pl.BlockSpec((1, width), vec(0)),
        pl.BlockSpec((1, width), vec(0)),
    ]
    conv = p["conv"]
    args = [rkv, rkv, rkv, lora, conv, conv, conv, conv, p["w0"], p["w2"], p["a0"], p["a2"],
            p["g2"], p["k_k"], p["k_a"], p["r_k"], p["ln_g"], p["ln_b"]]
    state_spec = pl.BlockSpec((None, 2, heads, n, n), lambda s, i, j: (s, 0, i, 0, 0))
    if has_state:
        in_specs.append(state_spec)
        args.append(s0)
    seq_buf = pltpu.VMEM((t, width), F32)
    dir_buf = pltpu.VMEM((2, t, width), F32)
    return pl.pallas_call(
        functools.partial(_rwkv_kernel, seq_len=t, has_state=has_state),
        out_shape=(jax.ShapeDtypeStruct((n_seq, t, R_WIDTH), BF16),
                   jax.ShapeDtypeStruct((n_seq, 2, R_HEADS, n, n), F32)),
        grid=(n_seq, blocks, t // STEP_ROWS),
        in_specs=in_specs,
        out_specs=(pl.BlockSpec((None, t, width), col(0)), state_spec),
        scratch_shapes=[seq_buf, seq_buf, seq_buf, pltpu.VMEM((t, LORA_WIDTH), F32),
                        dir_buf, dir_buf, pltpu.VMEM((2, PAIRS_PER_STEP, 2 * n, 2 * n), F32)],
        compiler_params=_params("arbitrary", "arbitrary", "arbitrary"),
        name="rwkv7_latent" if has_state else "rwkv7_context",
    )(*args)


def _merge_kernel(x_ref, ao_ref, ro_ref, sga_ref, sgr_ref, mod_ref, g_ref, wab_ref, wrb_ref,
                  wo_ref, x1_ref, h2_ref):
    merged = (sga_ref[...] * _dot(ao_ref[...], wab_ref[...])
              + sgr_ref[...] * _dot(ro_ref[...], wrb_ref[...]))
    x1 = x_ref[...] + mod_ref[2:3, :] * _dot(merged.astype(BF16), wo_ref[...])
    x1_ref[...] = x1
    h2 = _rms(x1) * g_ref[...]
    h2_ref[...] = (h2 * (1.0 + mod_ref[4:5, :]) + mod_ref[3:4, :]).astype(BF16)


def _merge(x2d, attn_o, rwkv_o, sga, sgr, mod, norm2_g, wab, wrb, wo, seq_len):
    n_tok = x2d.shape[0]
    tm = PROJ_ROWS
    row = lambda i: (i, 0)
    const = lambda i: (0, 0)
    return pl.pallas_call(
        _merge_kernel,
        out_shape=(jax.ShapeDtypeStruct((n_tok, D_MODEL), F32),
                   jax.ShapeDtypeStruct((n_tok, D_MODEL), BF16)),
        grid=(n_tok // tm,),
        in_specs=[
            pl.BlockSpec((tm, D_MODEL), row),
            pl.BlockSpec((tm, ATTN_WIDTH), row),
            pl.BlockSpec((tm, R_WIDTH), row),
            pl.BlockSpec((tm, D_MODEL), row),
            pl.BlockSpec((tm, D_MODEL), row),
            pl.BlockSpec((None, N_MOD, D_MODEL), lambda i: ((i * tm) // seq_len, 0, 0)),
            pl.BlockSpec((1, D_MODEL), const),
            pl.BlockSpec((ATTN_WIDTH, D_MODEL), const),
            pl.BlockSpec((R_WIDTH, D_MODEL), const),
            pl.BlockSpec((D_MODEL, D_MODEL), const),
        ],
        out_specs=(pl.BlockSpec((tm, D_MODEL), row), pl.BlockSpec((tm, D_MODEL), row)),
        compiler_params=_params("arbitrary"),
        name="merge_out_proj",
    )(x2d, attn_o, rwkv_o, sga, sgr, mod, norm2_g, wab, wrb, wo)


def _ffn_kernel(h2_ref, x1_ref, mod_ref, wu_ref, wg_ref, cu_ref, cg_ref, wd_ref, fg_ref,
                y_ref, acc_ref, upu_ref, upg_ref, *, seq_len):
    i = pl.program_id(0)
    j = pl.program_id(1)
    rows = h2_ref.shape[0]
    pad = FFN_PAD_ROWS
    rb = FFN_CONV_ROWS
    tile = 8

    @pl.when((i == 0) & (j == 0))
    def _():
        for ref in (upu_ref, upg_ref):
            ref[0:pad, :] = jnp.zeros((pad, ref.shape[1]), F32)
            ref[pad + rows:pad + rows + pad, :] = jnp.zeros((pad, ref.shape[1]), F32)

    @pl.when(j == 0)
    def _():
        acc_ref[...] = jnp.zeros(acc_ref.shape, F32)

    def up_project(r0):
        h2 = h2_ref[r0:r0 + rb, :]
        upu_ref[pad + r0:pad + r0 + rb, :] = _dot(h2, wu_ref[...])
        upg_ref[pad + r0:pad + r0 + rb, :] = _dot(h2, wg_ref[...])

    row0 = lax.broadcasted_iota(jnp.int32, (tile, 1), 0) == 0
    row7 = lax.broadcasted_iota(jnp.int32, (tile, 1), 0) == tile - 1

    taps_u = [cu_ref[k:k + 1, :] for k in range(3)]
    taps_g = [cg_ref[k:k + 1, :] for k in range(3)]

    def zero_rows(x, tile_offsets, row_mask):
        parts, done = [], 0
        for off in tile_offsets:
            parts += [x[done:off], jnp.where(row_mask, 0.0, x[off:off + tile])]
            done = off + tile
        parts.append(x[done:])
        return jnp.concatenate([p for p in parts if p.shape[0]], axis=0)

    def conv(up_ref, taps, r0):
        base = pad + r0
        cur = up_ref[base:base + rb, :]
        prev = up_ref[base - 1:base - 1 + rb, :]
        nxt = up_ref[base + 1:base + 1 + rb, :]
        first_seq = -(-r0 // seq_len) * seq_len
        starts = [s - r0 for s in range(first_seq, r0 + rb, seq_len)]
        ends = [e - r0 for e in range(first_seq if first_seq > r0 else first_seq + seq_len,
                                      r0 + rb + 1, seq_len)]
        prev = zero_rows(prev, starts, row0)
        nxt = zero_rows(nxt, [e - tile for e in ends], row7)
        return prev * taps[0] + cur * taps[1] + nxt * taps[2]

    starts = list(range(0, rows, rb))
    for r0 in starts[:2]:
        up_project(r0)
    for b, r0 in enumerate(starts):
        up_u = conv(upu_ref, taps_u, r0)
        up_g = conv(upg_ref, taps_g, r0)
        act = (up_g * _sigmoid(up_g) * up_u).astype(BF16)
        if b + 2 < len(starts):
            up_project(starts[b + 2])
        acc_ref[r0:r0 + rb, :] += _dot(act, wd_ref[...])

    @pl.when(j == pl.num_programs(1) - 1)
    def _():
        x2 = x1_ref[...] + mod_ref[...] * acc_ref[...]
        y_ref[...] = _rms(x2) * fg_ref[...]


def _ffn(h2, x1, gate2, ffn_up, ffn_conv, ffn_down, final_g, seq_len):
    n_tok = h2.shape[0]
    tm = FFN_ROWS
    tn = FFN_TILE
    nt = D_FF // tn
    assert tm % seq_len == 0 or seq_len % tm == 0
    gate_map = lambda i, j: ((i * tm) // seq_len, 0, 0)
    return pl.pallas_call(
        functools.partial(_ffn_kernel, seq_len=seq_len),
        out_shape=jax.ShapeDtypeStruct((n_tok, D_MODEL), F32),
        grid=(n_tok // tm, nt),
        in_specs=[
            pl.BlockSpec((tm, D_MODEL), lambda i, j: (i, 0)),
            pl.BlockSpec((tm, D_MODEL), lambda i, j: (i, 0)),
            pl.BlockSpec((None, 1, D_MODEL), gate_map),
            pl.BlockSpec((D_MODEL, tn), lambda i, j: (0, j)),
            pl.BlockSpec((D_MODEL, tn), lambda i, j: (0, nt + j)),
            pl.BlockSpec((3, tn), lambda i, j: (0, j)),
            pl.BlockSpec((3, tn), lambda i, j: (0, nt + j)),
            pl.BlockSpec((tn, D_MODEL), lambda i, j: (j, 0)),
            pl.BlockSpec((1, D_MODEL), lambda i, j: (0, 0)),
        ],
        out_specs=pl.BlockSpec((tm, D_MODEL), lambda i, j: (i, 0)),
        scratch_shapes=[pltpu.VMEM((tm, D_MODEL), F32),
                        pltpu.VMEM((tm + 2 * FFN_PAD_ROWS, tn), F32),
                        pltpu.VMEM((tm + 2 * FFN_PAD_ROWS, tn), F32)],
        compiler_params=_params("arbitrary", "arbitrary"),
        name="conv_ffn",
    )(h2, x1, gate2, ffn_up, ffn_up, ffn_conv, ffn_conv, ffn_down, final_g)


def _layer_path(x, mod, w, rope, cache, s0):
    b, t, _ = x.shape
    x2d = x.reshape(b * t, D_MODEL)
    qkv, rkv, lora, sga, sgr = _in_projection(x2d, mod, w["norm1_g"], w["w_in"], t)
    qkv = qkv.reshape(b, t, ATTN_WIDTH + 2 * KV_WIDTH)
    if rope is None:
        attn_o, k_norm, v_hm = _attention(qkv, w["q_norm_g"], w["k_norm_g"])
    else:
        attn_o = _attention(qkv, w["q_norm_g"], w["k_norm_g"], cache=cache, rope=rope)
        k_norm = v_hm = None
    rwkv_o, s_fin = _rwkv(rkv.reshape(b, t, 3 * R_WIDTH), lora.reshape(b, t, LORA_WIDTH),
                          w["rwkv"], s0)
    x1, h2 = _merge(x2d, attn_o.reshape(b * t, ATTN_WIDTH), rwkv_o.reshape(b * t, R_WIDTH),
                    sga, sgr, mod, w["norm2_g"], w["w_attn_br"], w["w_rwkv_br"], w["w_out"], t)
    gate2 = mod[:, 5:6, :]
    y = _ffn(h2, x1, gate2, w["ffn_up"], w["ffn_conv"], w["ffn_down"], w["final_norm_g"], t)
    return y.reshape(b, t, D_MODEL), k_norm, v_hm, s_fin


def _rope_tables(rows):
    row = jnp.repeat(jnp.arange(rows), GRID_W).astype(F32)
    col = jnp.tile(jnp.arange(GRID_W), rows).astype(F32)
    inv = 1.0 / (ROPE_THETA ** (jnp.arange(ROPE_PAIRS, dtype=F32) / ROPE_PAIRS))
    ang = jnp.concatenate([row[:, None] * inv, col[:, None] * inv], axis=-1)
    cos, sin = jnp.cos(ang), jnp.sin(ang)
    return jnp.concatenate([cos, cos] * 2, axis=-1), jnp.concatenate([-sin, sin] * 2, axis=-1)


def kernel(x_prompt, x_sample, cache_k, cache_v, state_rwkv, c, c_ctx, w_ada, b_ada, norm1_g, w_in, q_norm_g, k_norm_g, rwkv_conv, rwkv_w0, rwkv_w2, rwkv_a0, rwkv_a2, rwkv_g2, rwkv_k_k, rwkv_k_a, rwkv_r_k, rwkv_ln_g, rwkv_ln_b, w_attn_br, w_rwkv_br, w_out, norm2_g, ffn_up, ffn_conv, ffn_down, final_norm_g):
    depth = w_in.shape[0]
    assert depth == 1, "single trunk layer"
    l = 0
    n_ctx = x_prompt.shape[0]
    n_lat = x_sample.shape[0]

    cc = jnp.concatenate([c_ctx[None, :], c, jnp.zeros((8 - 1 - n_lat, D_MODEL), F32)], axis=0)
    mod_all = _modulation(cc, w_ada[l], b_ada[l][None, :])
    mod_ctx = jnp.broadcast_to(mod_all[0:1].reshape(1, N_MOD, D_MODEL), (n_ctx, N_MOD, D_MODEL))
    mod_lat = mod_all[1:1 + n_lat].reshape(n_lat, N_MOD, D_MODEL)

    rwkv_p = {
        "conv": rwkv_conv[l], "w0": rwkv_w0[l], "w2": rwkv_w2[l], "a0": rwkv_a0[l],
        "a2": rwkv_a2[l], "g2": rwkv_g2[l], "k_k": rwkv_k_k[l][None, :],
        "k_a": rwkv_k_a[l][None, :], "r_k": rwkv_r_k[l].reshape(1, R_WIDTH),
        "ln_g": rwkv_ln_g[l][None, :], "ln_b": rwkv_ln_b[l][None, :],
    }
    w = {
        "norm1_g": norm1_g[l][None, :], "w_in": w_in[l].astype(BF16),
        "q_norm_g": q_norm_g[l][None, :], "k_norm_g": k_norm_g[l][None, :],
        "rwkv": rwkv_p,
        "w_attn_br": w_attn_br[l].astype(BF16), "w_rwkv_br": w_rwkv_br[l].astype(BF16),
        "w_out": w_out[l].astype(BF16), "norm2_g": norm2_g[l][None, :],
        "ffn_up": ffn_up[l].astype(BF16), "ffn_conv": ffn_conv[l],
        "ffn_down": ffn_down[l].astype(BF16), "final_norm_g": final_norm_g[None, :],
    }

    y_prompt, k_ctx, v_ctx, s_ctx = _layer_path(x_prompt, mod_ctx, w, None, None, None)
    rope = _rope_tables(x_sample.shape[1] // GRID_W)
    y_sample, _, _, _ = _layer_path(x_sample, mod_lat, w, rope, (cache_k[:, l], cache_v[:, l]),
                                    state_rwkv[:, l])
    return (y_prompt, y_sample, k_ctx[:, None], v_ctx[:, None], s_ctx[:, None])
```

```python
import functools

import jax
import jax.numpy as jnp
import numpy as np
from jax import lax
from jax.experimental import pallas as pl
from jax.experimental.pallas import tpu as pltpu

D_MODEL = 1024
GRID_W = 64
N_HEADS = 8
KV_HEADS = 2
Q_PER_KV = N_HEADS // KV_HEADS
HEAD_DIM = 64
ATTN_WIDTH = N_HEADS * HEAD_DIM
KV_WIDTH = KV_HEADS * HEAD_DIM
ROPE_THETA = 10000.0
ROPE_PAIRS = HEAD_DIM // 4
R_HEADS = 8
R_HEAD_DIM = 64
R_WIDTH = R_HEADS * R_HEAD_DIM
DECAY_RANK = 64
ICLR_RANK = 64
GATE_RANK = 128
LORA_WIDTH = DECAY_RANK + ICLR_RANK + GATE_RANK
R_IN_WIDTH = 3 * R_WIDTH + LORA_WIDTH
D_FF = 2816
N_MOD = 6
NORM_EPS = 1e-6
GN_EPS = 64e-5
DECAY_SCALE = float(np.exp(-0.5))

F32 = jnp.float32
BF16 = jnp.bfloat16

VMEM_LIMIT_BYTES = 56 * 1024 * 1024
PROJ_ROWS = 512
FFN_ROWS = 1024
FFN_TILE = 256
FFN_PAD_ROWS = 8
FFN_CONV_ROWS = 256
Q_ROWS = 128
KEY_CHUNK = 256
STEP_ROWS = 256
GROUP = 128
PAIRS_PER_STEP = 2
CHUNK = 64

_NN = (((1,), (0,)), ((), ()))
_NT = (((1,), (1,)), ((), ()))
_TN = (((0,), (0,)), ((), ()))


def _dg(a, b, dims):
    return lax.dot_general(a, b, dims, preferred_element_type=F32)


def _dot(a, b):
    return _dg(a, b, _NN)


def _mm(a, b, dims=_NN):
    return _dg(a.astype(BF16), b.astype(BF16), dims)


def _mm_exact_lhs(m, x):
    hi = x.astype(BF16)
    r1 = x - hi.astype(F32)
    mid = r1.astype(BF16)
    lo = (r1 - mid.astype(F32)).astype(BF16)
    return _dot(m, hi) + (_dot(m, mid) + _dot(m, lo))


def _sigmoid(x):
    return 1.0 / (1.0 + jnp.exp(-x))


def _rms(x):
    return x * lax.rsqrt(jnp.mean(x * x, axis=-1, keepdims=True) + NORM_EPS)


def _params(*semantics):
    return pltpu.CompilerParams(dimension_semantics=semantics,
                                vmem_limit_bytes=VMEM_LIMIT_BYTES)


def _mod_kernel(c_ref, w_ref, b_ref, o_ref):
    c = c_ref[...]
    s = (c * _sigmoid(c)).astype(BF16)
    o_ref[...] = _dot(s, w_ref[...].astype(BF16)) + b_ref[...]


def _modulation(cc, w_ada, b_ada):
    rows = cc.shape[0]
    n = w_ada.shape[1]
    tile = 1024
    return pl.pallas_call(
        _mod_kernel,
        out_shape=jax.ShapeDtypeStruct((rows, n), F32),
        grid=(n // tile,),
        in_specs=[
            pl.BlockSpec((rows, D_MODEL), lambda j: (0, 0)),
            pl.BlockSpec((D_MODEL, tile), lambda j: (0, j)),
            pl.BlockSpec((1, tile), lambda j: (0, j)),
        ],
        out_specs=pl.BlockSpec((rows, tile), lambda j: (0, j)),
        compiler_params=_params("arbitrary"),
        name="adaln_mod",
    )(cc, w_ada, b_ada)


def _inproj_kernel(x_ref, mod_ref, g_ref, w_ref, qkv_ref, rkv_ref, lora_ref, sga_ref, sgr_ref):
    x = x_ref[...]
    h = _rms(x) * g_ref[...]
    h = h * (1.0 + mod_ref[1:2, :]) + mod_ref[0:1, :]
    hb = h.astype(BF16)
    c0 = ATTN_WIDTH + 2 * KV_WIDTH
    c1 = c0 + 3 * R_WIDTH
    c2 = c1 + LORA_WIDTH
    c3 = c2 + D_MODEL
    qkv_ref[...] = _dot(hb, w_ref[:, 0:c0])
    rkv_ref[...] = _dot(hb, w_ref[:, c0:c1])
    lora_ref[...] = _dot(hb, w_ref[:, c1:c2])
    sga_ref[...] = _sigmoid(_dot(hb, w_ref[:, c2:c3])).astype(BF16)
    sgr_ref[...] = _sigmoid(_dot(hb, w_ref[:, c3:c3 + D_MODEL])).astype(BF16)


def _in_projection(x2d, mod, norm1_g, w_in_bf16, seq_len):
    n_tok = x2d.shape[0]
    tm = PROJ_ROWS
    in_width = w_in_bf16.shape[1]
    widths = (ATTN_WIDTH + 2 * KV_WIDTH, 3 * R_WIDTH, LORA_WIDTH, D_MODEL, D_MODEL)
    dtypes = (F32, F32, F32, BF16, BF16)
    row = lambda i: (i, 0)
    return pl.pallas_call(
        _inproj_kernel,
        out_shape=tuple(jax.ShapeDtypeStruct((n_tok, w), dt) for w, dt in zip(widths, dtypes)),
        grid=(n_tok // tm,),
        in_specs=[
            pl.BlockSpec((tm, D_MODEL), row),
            pl.BlockSpec((None, N_MOD, D_MODEL), lambda i: ((i * tm) // seq_len, 0, 0)),
            pl.BlockSpec((1, D_MODEL), lambda i: (0, 0)),
            pl.BlockSpec((D_MODEL, in_width), lambda i: (0, 0)),
        ],
        out_specs=tuple(pl.BlockSpec((tm, w), row) for w in widths),
        compiler_params=_params("arbitrary"),
        name="in_proj",
    )(x2d, mod, norm1_g, w_in_bf16)


def _attn_kernel(*refs, seq_len, past_len, latent):
    if latent:
        (q_ref, k_ref, v_ref, ck_ref, cv_ref, cos_ref, sin_ref, qg_ref, kg_ref,
         o_ref, qs_ref, kd_ref, vd_ref) = refs
    else:
        (q_ref, k_ref, v_ref, qg_ref, kg_ref, o_ref, kn_ref, vn_ref, qs_ref, kd_ref, vd_ref) = refs
    t = seq_len
    d = HEAD_DIM
    width = 2 * d
    total = t + past_len
    group = pl.program_id(1)
    lane = lax.broadcasted_iota(jnp.int32, (1, width), 1)
    head_of_lane = lane // d
    low_half = (lane % d) < d // 2
    inv_d = 1.0 / d

    def pair_norm(x, gain):
        sq = x * x
        s0 = jnp.sum(jnp.where(head_of_lane == 0, sq, 0.0), axis=-1, keepdims=True)
        s1 = jnp.sum(jnp.where(head_of_lane == 0, 0.0, sq), axis=-1, keepdims=True)
        ms = jnp.where(head_of_lane == 0, s0, s1) * inv_d
        y = x * lax.rsqrt(ms + NORM_EPS) * gain
        if latent:
            partner = jnp.where(low_half, pltpu.roll(y, width - d // 2, 1), pltpu.roll(y, d // 2, 1))
            y = y * cos_ref[...] + partner * sin_ref[...]
        return y

    def this_group(x):
        return jnp.where(head_of_lane == group, x, pltpu.roll(x, d, 1))

    kg2 = jnp.concatenate([kg_ref[...], kg_ref[...]], axis=-1)
    qg2 = jnp.concatenate([qg_ref[...], qg_ref[...]], axis=-1)
    v_dup = this_group(v_ref[...])
    if latent:
        kd_ref[0:t, :] = this_group(pair_norm(k_ref[...], kg2)).astype(BF16)
        ck = ck_ref[...].astype(BF16)
        kd_ref[t:total, :] = jnp.concatenate([ck, ck], axis=-1)
        vd_ref[0:t, :] = v_dup.astype(BF16)
        cv = cv_ref[...].astype(BF16)
        vd_ref[t:total, :] = jnp.concatenate([cv, cv], axis=-1)
    else:
        k_dup = this_group(pair_norm(k_ref[...], kg2))
        kn_ref[...] = k_dup[:, 0:d]
        vn_ref[...] = v_dup[:, 0:d]
        kd_ref[...] = k_dup.astype(BF16)
        vd_ref[...] = v_dup.astype(BF16)
    scale = d ** -0.5
    for j in range(Q_PER_KV // 2):
        cols = slice(j * width, (j + 1) * width)
        qs_ref[:, cols] = (pair_norm(q_ref[:, cols], qg2) * scale).astype(BF16)
    chunk = min(KEY_CHUNK, total)
    n_chunks = total // chunk

    def body(qb, carry):
        r0 = pl.multiple_of(qb * Q_ROWS, Q_ROWS)
        outs = []
        for j in range(Q_PER_KV // 2):
            qp = qs_ref[pl.ds(r0, Q_ROWS), j * width:(j + 1) * width]
            heads = []
            for h in range(2):
                qm = jnp.where(head_of_lane == h, qp, jnp.zeros_like(qp))
                m = l = acc = None
                for c in range(n_chunks):
                    keys = slice(c * KEY_CHUNK, (c + 1) * KEY_CHUNK)
                    s = _dg(qm, kd_ref[keys, :], _NT)
                    m_c = jnp.max(s, axis=-1, keepdims=True)
                    if c == 0:
                        m = m_c
                        p = jnp.exp(s - m)
                        l = jnp.sum(p, axis=-1, keepdims=True)
                        acc = _dot(p.astype(BF16), vd_ref[keys, :])
                    else:
                        m_new = jnp.maximum(m, m_c)
                        alpha = jnp.exp(m - m_new)
                        p = jnp.exp(s - m_new)
                        l = alpha * l + jnp.sum(p, axis=-1, keepdims=True)
                        acc = alpha * acc + _dot(p.astype(BF16), vd_ref[keys, :])
                        m = m_new
                heads.append(acc / l)
            outs.append(jnp.where(head_of_lane == 0, heads[0], heads[1]))
        o_ref[pl.ds(r0, Q_ROWS), :] = jnp.concatenate(outs, axis=-1).astype(BF16)
        return carry

    lax.fori_loop(0, t // Q_ROWS, body, 0)


def _attention(qkv, qg, kg, cache=None, rope=None):
    b, t, _ = qkv.shape
    d = HEAD_DIM
    latent = cache is not None
    past_len = cache[0].shape[2] if latent else 0
    total = t + past_len
    assert total % min(KEY_CHUNK, total) == 0 and KV_WIDTH == 2 * d
    gw = Q_PER_KV * d
    q_spec = pl.BlockSpec((None, t, gw), lambda i, g: (i, 0, g))
    k_spec = pl.BlockSpec((None, t, KV_WIDTH), lambda i, g: (i, 0, ATTN_WIDTH // KV_WIDTH))
    v_spec = pl.BlockSpec((None, t, KV_WIDTH), lambda i, g: (i, 0, ATTN_WIDTH // KV_WIDTH + 1))
    hm_spec = pl.BlockSpec((None, None, t, d), lambda i, g: (i, g, 0, 0))
    vec_spec = pl.BlockSpec((1, d), lambda i, g: (0, 0))
    out_shape = jax.ShapeDtypeStruct((b, t, ATTN_WIDTH), BF16)
    scratch = [pltpu.VMEM((t, gw), BF16), pltpu.VMEM((total, 2 * d), BF16),
               pltpu.VMEM((total, 2 * d), BF16)]
    kern = functools.partial(_attn_kernel, seq_len=t, past_len=past_len, latent=latent)
    if latent:
        ck, cv = cache
        cos4, sin4 = rope
        c_spec = pl.BlockSpec((None, None, past_len, d), lambda i, g: (i, g, 0, 0))
        tab_spec = pl.BlockSpec((t, 2 * d), lambda i, g: (0, 0))
        return pl.pallas_call(
            kern,
            out_shape=out_shape,
            grid=(b, KV_HEADS),
            in_specs=[q_spec, k_spec, v_spec, c_spec, c_spec, tab_spec, tab_spec,
                      vec_spec, vec_spec],
            out_specs=q_spec,
            scratch_shapes=scratch,
            compiler_params=_params("arbitrary", "arbitrary"),
            name="attn_latent",
        )(qkv, qkv, qkv, ck, cv, cos4, sin4, qg, kg)
    hm_shape = jax.ShapeDtypeStruct((b, KV_HEADS, t, d), F32)
    return pl.pallas_call(
        kern,
        out_shape=(out_shape, hm_shape, hm_shape),
        grid=(b, KV_HEADS),
        in_specs=[q_spec, k_spec, v_spec, vec_spec, vec_spec],
        out_specs=(q_spec, hm_spec, hm_spec),
        scratch_shapes=scratch,
        compiler_params=_params("arbitrary", "arbitrary"),
        name="attn_context",
    )(qkv, qkv, qkv, qg, kg)


def _conv_rows(ref_rows, taps, start, n, total):
    cur = ref_rows(start, n)
    width = cur.shape[-1]
    zero = jnp.zeros((1, width), F32)
    prev_row = ref_rows(start - 1, 1) if start > 0 else zero
    next_row = ref_rows(start + n, 1) if start + n < total else zero
    ridx = lax.broadcasted_iota(jnp.int32, (n, 1), 0)
    prev = jnp.where(ridx == 0, prev_row, pltpu.roll(cur, 1, 0))
    nxt = jnp.where(ridx == n - 1, next_row, pltpu.roll(cur, n - 1, 0))
    return prev * taps[0] + cur * taps[1] + nxt * taps[2]


def _round_robin(chains):
    results = [None] * len(chains)
    live = list(range(len(chains)))
    while live:
        for i in list(live):
            try:
                next(chains[i])
            except StopIteration as done:
                results[i] = done.value
                live.remove(i)
    return results


def _rwkv_kernel(*refs, seq_len, has_state):
    n_in = 19 if has_state else 18
    (r_ref, k_ref, v_ref, lora_ref, cr_ref, ck_ref, cv_ref, cl_ref, w0_ref, w2_ref, a0_ref, a2_ref,
     g2_ref, kk_ref, ka_ref, rk_ref, lng_ref, lnb_ref) = refs[:18]
    s0_ref = refs[18] if has_state else None
    y_ref, sfin_ref, rs_ref, ks_ref, vs_ref, ls_ref, yacc_ref, bacc_ref, st_ref = refs[n_in:]
    t = seq_len
    n_steps = t // STEP_ROWS
    n_sub = STEP_ROWS // GROUP
    n_chunks = GROUP // CHUNK
    n = R_HEAD_DIM
    width = 2 * n
    step = pl.program_id(2)
    pair_lanes = [slice(q * width, (q + 1) * width) for q in range(PAIRS_PER_STEP)]

    @pl.when(step == 0)
    def _():
        for blk in range(n_steps):
            start = blk * STEP_ROWS
            rows = slice(start, start + STEP_ROWS)
            for src, cw, dst in ((r_ref, cr_ref, rs_ref), (k_ref, ck_ref, ks_ref),
                                 (v_ref, cv_ref, vs_ref), (lora_ref, cl_ref, ls_ref)):
                taps = [cw[j:j + 1, :] for j in range(3)]
                dst[rows, :] = _conv_rows(lambda s, m, src=src: src[s:s + m, :],
                                          taps, start, STEP_ROWS, t)
        zero = jnp.zeros((n, n), F32)
        for d in range(2):
            for q in range(PAIRS_PER_STEP):
                if has_state:
                    s_a, s_b = s0_ref[d, 2 * q], s0_ref[d, 2 * q + 1]
                    bd = jnp.concatenate([jnp.concatenate([s_a, zero], axis=1),
                                          jnp.concatenate([zero, s_b], axis=1)], axis=0)
                    st_ref[d, q] = bd.T
                else:
                    st_ref[d, q] = jnp.zeros((width, width), F32)

    lane = lax.broadcasted_iota(jnp.int32, (1, width), 1)
    head0 = lane < R_HEAD_DIM

    def per_head(x):
        x0 = jnp.where(head0, x, 0.0)
        return (x0, x - x0)

    def pick(x0, x1):
        return jnp.where(head0, x0, x1)

    def pair_mm(mats, x):
        return pick(*[_mm(m, x) for m in mats])

    def pair_mm2(mats, x, y):
        xy = jnp.concatenate([x, y], axis=1)
        prods = [_mm(m, xy) for m in mats]
        return (pick(*[p[:, :width] for p in prods]), pick(*[p[:, width:] for p in prods]))

    def head_sum(x):
        s0 = jnp.sum(jnp.where(head0, x, 0.0), axis=-1, keepdims=True)
        s1 = jnp.sum(jnp.where(head0, 0.0, x), axis=-1, keepdims=True)
        return jnp.where(head0, s0, s1)

    def scan_rows(d, q, blk, state):
        rev = d == 1
        lanes = pair_lanes[q]
        base = blk * STEP_ROWS
        ri = lax.broadcasted_iota(jnp.int32, (GROUP, GROUP), 0)
        ci = lax.broadcasted_iota(jnp.int32, (GROUP, GROUP), 1)
        same = (ri // CHUNK) == (ci // CHUNK)
        eye = (ri == ci).astype(F32)
        strict = same & ((ci > ri) if rev else (ci < ri))
        incl = same & ((ci >= ri) if rev else (ci <= ri))
        tri = incl.astype(BF16)

        def half_mask(m):
            return ((ri // (2 * m)) == (ci // (2 * m))) & ((ri // m) != (ci // m))

        subs = []
        for s in (range(n_sub - 1, -1, -1) if rev else range(n_sub)):
            rows = pl.ds(pl.multiple_of(base + s * GROUP, GROUP), GROUP)
            lo = ls_ref[rows, :]
            wd_t = jnp.tanh(lo[:, 0:DECAY_RANK])
            z = w0_ref[d:d + 1, lanes] + _mm(wd_t, w2_ref[d, :, lanes])
            subs.append({"rows": rows, "r": rs_ref[rows, lanes], "k": ks_ref[rows, lanes],
                         "v": vs_ref[rows, lanes], "ad": lo[:, DECAY_RANK:DECAY_RANK + ICLR_RANK],
                         "lw": -DECAY_SCALE * _sigmoid(z)})
        yield
        for p in subs:
            p["cum"] = _mm_exact_lhs(tri, p["lw"])
            p["cum_c"] = jnp.concatenate(
                [jnp.broadcast_to(p["cum"][r:r + 1, :], (CHUNK, width))
                 for r in (range(0, GROUP, CHUNK) if rev else range(CHUNK - 1, GROUP, CHUNK))], axis=0)
            a = _sigmoid(a0_ref[d:d + 1, lanes] + _mm(p["ad"], a2_ref[d, :, lanes]))
            p["kd"] = p["k"] * (1.0 + (a - 1.0) * ka_ref[:, lanes])
            kk = p["k"] * kk_ref[:, lanes]
            p["kk"] = kk * lax.rsqrt(head_sum(kk * kk) + 1e-12)
            p["b"] = p["kk"] * a
            bacc_ref[d, p["rows"], lanes] = head_sum(p["r"] * p["kd"] * rk_ref[:, lanes]) * p["v"]
        yield
        for p in subs:
            cum, cum_c = p["cum"], p["cum_c"]
            p["a_t"] = -p["kk"] * jnp.exp(cum - p["lw"])
            p["r_t"] = p["r"] * jnp.exp(cum)
            inv = jnp.exp(-cum)
            p["b_t"] = p["b"] * inv
            p["k_t"] = p["kd"] * inv
            tail = jnp.exp(cum_c - cum)
            p["b_h"] = p["b"] * tail
            p["k_h"] = p["kd"] * tail
            p["p_c"] = jnp.exp(cum_c)
        yield
        for p in subs:
            bk = jnp.concatenate([p["b_t"], p["k_t"]], axis=0)
            xa = [_mm(x, bk, _NT) for x in per_head(p["a_t"])]
            xr = [_mm(x, bk, _NT) for x in per_head(p["r_t"])]
            p["a_ab"] = [jnp.where(strict, m[:, :GROUP], 0.0) for m in xa]
            p["a_ak"] = [jnp.where(strict, m[:, GROUP:], 0.0) for m in xa]
            p["a_rb"] = [jnp.where(incl, m[:, :GROUP], 0.0) for m in xr]
            p["a_rk"] = [jnp.where(incl, m[:, GROUP:], 0.0) for m in xr]
        yield
        for p in subs:
            p["akv"] = pair_mm(p["a_ak"], p["v"])
            p["rkv"] = pair_mm(p["a_rk"], p["v"])
            p["tm"] = [eye + jnp.where(half_mask(1), m, 0.0) for m in p["a_ab"]]
        m_size = 2
        while m_size < CHUNK:
            yield
            off = half_mask(m_size)
            for p in subs:
                p["ot"] = [_mm(jnp.where(off, n_mat, 0.0), tm)
                           for n_mat, tm in zip(p["a_ab"], p["tm"])]
            yield
            for p in subs:
                p["tm"] = [tm + _mm(tm, ot) for tm, ot in zip(p["tm"], p["ot"])]
            m_size *= 2
        yield
        for p in subs:
            p["ta"], p["tk"] = pair_mm2(p["tm"], p["a_t"], p["akv"])
        yield
        cr = [slice(c * CHUNK, (c + 1) * CHUNK) for c in range(n_chunks)]
        for p in subs:
            ra, rk = pair_mm2(p["a_rb"], p["ta"], p["tk"])
            p["r2"] = p["r_t"] + ra
            p["y2"] = p["rkv"] + rk
            p["m_t"] = [eye * p["p_c"][c * CHUNK:c * CHUNK + 1, :]
                        + jnp.where(same, _mm(p["b_h"][cr[c]], p["ta"][cr[c]], _TN), 0.0)
                        for c in range(n_chunks)]
            p["g_t"] = [jnp.where(same, _mm(jnp.concatenate([p["b_h"][cr[c]], p["k_h"][cr[c]]], axis=0),
                                            jnp.concatenate([p["tk"][cr[c]], p["v"][cr[c]]], axis=0),
                                            _TN), 0.0)
                        for c in range(n_chunks)]
        for p in subs:
            ys = [None] * n_chunks
            for c in (range(n_chunks - 1, -1, -1) if rev else range(n_chunks)):
                yield
                ys[c] = _mm(p["r2"][cr[c]], state) + p["y2"][cr[c]]
                state = _mm(p["m_t"][c], state) + p["g_t"][c]
            yacc_ref[d, p["rows"], lanes] = jnp.concatenate(ys, axis=0)
        return state

    chains = [(d, q) for q in range(PAIRS_PER_STEP) for d in range(2)]
    blocks = (step, n_steps - 1 - step)
    new_state = _round_robin([scan_rows(d, q, blocks[d], st_ref[d, q]) for d, q in chains])
    for (d, q), s_new in zip(chains, new_state):
        st_ref[d, q] = s_new

    @pl.when(step == n_steps - 1)
    def _():
        for d, q in chains:
            s_vk = st_ref[d, q].T
            sfin_ref[d, 2 * q] = s_vk[0:n, 0:n]
            sfin_ref[d, 2 * q + 1] = s_vk[n:width, n:width]
        inv_n = 1.0 / n
        for blk in range(n_steps):
            rows = slice(blk * STEP_ROWS, (blk + 1) * STEP_ROWS)
            gd = _sigmoid(ls_ref[rows, DECAY_RANK + ICLR_RANK:LORA_WIDTH]).astype(BF16)
            for lanes in pair_lanes:
                y = yacc_ref[0, rows, lanes] + yacc_ref[1, rows, lanes]
                yc = y - head_sum(y) * inv_n
                yn = yc * lax.rsqrt(head_sum(yc * yc) * inv_n + GN_EPS)
                out = (yn * lng_ref[:, lanes] + lnb_ref[:, lanes]
                       + (bacc_ref[0, rows, lanes] + bacc_ref[1, rows, lanes]))
                y_ref[rows, lanes] = (out * _dot(gd, g2_ref[:, lanes].astype(BF16))).astype(BF16)


def _rwkv(rkv, lora, p, s0):
    n_seq, t, _ = rkv.shape
    has_state = s0 is not None
    n = R_HEAD_DIM
    heads = 2 * PAIRS_PER_STEP
    width = heads * n
    blocks = R_WIDTH // width
    col = lambda off: (lambda s, i, j: (s, 0, off + i))
    vec = lambda off: (lambda s, i, j: (0, off + i))
    dir_mat = lambda s, i, j: (0, 0, i)
    in_specs = [
        pl.BlockSpec((None, t, width), col(0)),
        pl.BlockSpec((None, t, width), col(blocks)),
        pl.BlockSpec((None, t, width), col(2 * blocks)),
        pl.BlockSpec((None, t, LORA_WIDTH), lambda s, i, j: (s, 0, 0)),
        pl.BlockSpec((3, width), vec(0)),
        pl.BlockSpec((3, width), vec(blocks)),
        pl.BlockSpec((3, width), vec(2 * blocks)),
        pl.BlockSpec((3, LORA_WIDTH), lambda s, i, j: (0, 3 * R_WIDTH // LORA_WIDTH)),
        pl.BlockSpec((2, width), vec(0)),
        pl.BlockSpec((2, DECAY_RANK, width), dir_mat),
        pl.BlockSpec((2, width), vec(0)),
        pl.BlockSpec((2, ICLR_RANK, width), dir_mat),
        pl.BlockSpec((GATE_RANK, width), vec(0)),
        pl.BlockSpec((1, width), vec(0)),
        pl.BlockSpec((1, width), vec(0)),
        pl.BlockSpec((1, width), vec(0)),
        pl.BlockSpec((1, width), vec(0)),
        pl.BlockSpec((1, width), vec(0)),
    ]
    conv = p["conv"]
    args = [rkv, rkv, rkv, lora, conv, conv, conv, conv, p["w0"], p["w2"], p["a0"], p["a2"],
            p["g2"], p["k_k"], p["k_a"], p["r_k"], p["ln_g"], p["ln_b"]]
    state_spec = pl.BlockSpec((None, 2, heads, n, n), lambda s, i, j: (s, 0, i, 0, 0))
    if has_state:
        in_specs.append(state_spec)
        args.append(s0)
    seq_buf = pltpu.VMEM((t, width), F32)
    dir_buf = pltpu.VMEM((2, t, width), F32)
    return pl.pallas_call(
        functools.partial(_rwkv_kernel, seq_len=t, has_state=has_state),
        out_shape=(jax.ShapeDtypeStruct((n_seq, t, R_WIDTH), BF16),
                   jax.ShapeDtypeStruct((n_seq, 2, R_HEADS, n, n), F32)),
        grid=(n_seq, blocks, t // STEP_ROWS),
        in_specs=in_specs,
        out_specs=(pl.BlockSpec((None, t, width), col(0)), state_spec),
        scratch_shapes=[seq_buf, seq_buf, seq_buf, pltpu.VMEM((t, LORA_WIDTH), F32),
                        dir_buf, dir_buf, pltpu.VMEM((2, PAIRS_PER_STEP, 2 * n, 2 * n), F32)],
        compiler_params=_params("arbitrary", "arbitrary", "arbitrary"),
        name="rwkv7_latent" if has_state else "rwkv7_context",
    )(*args)


def _merge_kernel(x_ref, ao_ref, ro_ref, sga_ref, sgr_ref, mod_ref, g_ref, wab_ref, wrb_ref,
                  wo_ref, x1_ref, h2_ref):
    merged = (sga_ref[...] * _dot(ao_ref[...], wab_ref[...])
              + sgr_ref[...] * _dot(ro_ref[...], wrb_ref[...]))
    x1 = x_ref[...] + mod_ref[2:3, :] * _dot(merged.astype(BF16), wo_ref[...])
    x1_ref[...] = x1
    h2 = _rms(x1) * g_ref[...]
    h2_ref[...] = (h2 * (1.0 + mod_ref[4:5, :]) + mod_ref[3:4, :]).astype(BF16)


def _merge(x2d, attn_o, rwkv_o, sga, sgr, mod, norm2_g, wab, wrb, wo, seq_len):
    n_tok = x2d.shape[0]
    tm = PROJ_ROWS
    row = lambda i: (i, 0)
    const = lambda i: (0, 0)
    return pl.pallas_call(
        _merge_kernel,
        out_shape=(jax.ShapeDtypeStruct((n_tok, D_MODEL), F32),
                   jax.ShapeDtypeStruct((n_tok, D_MODEL), BF16)),
        grid=(n_tok // tm,),
        in_specs=[
            pl.BlockSpec((tm, D_MODEL), row),
            pl.BlockSpec((tm, ATTN_WIDTH), row),
            pl.BlockSpec((tm, R_WIDTH), row),
            pl.BlockSpec((tm, D_MODEL), row),
            pl.BlockSpec((tm, D_MODEL), row),
            pl.BlockSpec((None, N_MOD, D_MODEL), lambda i: ((i * tm) // seq_len, 0, 0)),
            pl.BlockSpec((1, D_MODEL), const),
            pl.BlockSpec((ATTN_WIDTH, D_MODEL), const),
            pl.BlockSpec((R_WIDTH, D_MODEL), const),
            pl.BlockSpec((D_MODEL, D_MODEL), const),
        ],
        out_specs=(pl.BlockSpec((tm, D_MODEL), row), pl.BlockSpec((tm, D_MODEL), row)),
        compiler_params=_params("arbitrary"),
        name="merge_out_proj",
    )(x2d, attn_o, rwkv_o, sga, sgr, mod, norm2_g, wab, wrb, wo)


def _ffn_kernel(h2_ref, x1_ref, mod_ref, wu_ref, wg_ref, cu_ref, cg_ref, wd_ref, fg_ref,
                y_ref, acc_ref, upu_ref, upg_ref, *, seq_len):
    i = pl.program_id(0)
    j = pl.program_id(1)
    rows = h2_ref.shape[0]
    pad = FFN_PAD_ROWS
    rb = FFN_CONV_ROWS
    tile = 8

    @pl.when((i == 0) & (j == 0))
    def _():
        for ref in (upu_ref, upg_ref):
            ref[0:pad, :] = jnp.zeros((pad, ref.shape[1]), F32)
            ref[pad + rows:pad + rows + pad, :] = jnp.zeros((pad, ref.shape[1]), F32)

    @pl.when(j == 0)
    def _():
        acc_ref[...] = jnp.zeros(acc_ref.shape, F32)

    def up_project(r0):
        h2 = h2_ref[r0:r0 + rb, :]
        upu_ref[pad + r0:pad + r0 + rb, :] = _dot(h2, wu_ref[...])
        upg_ref[pad + r0:pad + r0 + rb, :] = _dot(h2, wg_ref[...])

    row0 = lax.broadcasted_iota(jnp.int32, (tile, 1), 0) == 0
    row7 = lax.broadcasted_iota(jnp.int32, (tile, 1), 0) == tile - 1

    taps_u = [cu_ref[k:k + 1, :] for k in range(3)]
    taps_g = [cg_ref[k:k + 1, :] for k in range(3)]

    def zero_rows(x, tile_offsets, row_mask):
        parts, done = [], 0
        for off in tile_offsets:
            parts += [x[done:off], jnp.where(row_mask, 0.0, x[off:off + tile])]
            done = off + tile
        parts.append(x[done:])
        return jnp.concatenate([p for p in parts if p.shape[0]], axis=0)

    def conv(up_ref, taps, r0):
        base = pad + r0
        cur = up_ref[base:base + rb, :]
        prev = up_ref[base - 1:base - 1 + rb, :]
        nxt = up_ref[base + 1:base + 1 + rb, :]
        first_seq = -(-r0 // seq_len) * seq_len
        starts = [s - r0 for s in range(first_seq, r0 + rb, seq_len)]
        ends = [e - r0 for e in range(first_seq if first_seq > r0 else first_seq + seq_len,
                                      r0 + rb + 1, seq_len)]
        prev = zero_rows(prev, starts, row0)
        nxt = zero_rows(nxt, [e - tile for e in ends], row7)
        return prev * taps[0] + cur * taps[1] + nxt * taps[2]

    starts = list(range(0, rows, rb))
    for r0 in starts[:2]:
        up_project(r0)
    for b, r0 in enumerate(starts):
        up_u = conv(upu_ref, taps_u, r0)
        up_g = conv(upg_ref, taps_g, r0)
        act = (up_g * _sigmoid(up_g) * up_u).astype(BF16)
        if b + 2 < len(starts):
            up_project(starts[b + 2])
        acc_ref[r0:r0 + rb, :] += _dot(act, wd_ref[...])

    @pl.when(j == pl.num_programs(1) - 1)
    def _():
        x2 = x1_ref[...] + mod_ref[...] * acc_ref[...]
        y_ref[...] = _rms(x2) * fg_ref[...]


def _ffn(h2, x1, gate2, ffn_up, ffn_conv, ffn_down, final_g, seq_len):
    n_tok = h2.shape[0]
    tm = FFN_ROWS
    tn = FFN_TILE
    nt = D_FF // tn
    assert tm % seq_len == 0 or seq_len % tm == 0
    gate_map = lambda i, j: ((i * tm) // seq_len, 0, 0)
    return pl.pallas_call(
        functools.partial(_ffn_kernel, seq_len=seq_len),
        out_shape=jax.ShapeDtypeStruct((n_tok, D_MODEL), F32),
        grid=(n_tok // tm, nt),
        in_specs=[
            pl.BlockSpec((tm, D_MODEL), lambda i, j: (i, 0)),
            pl.BlockSpec((tm, D_MODEL), lambda i, j: (i, 0)),
            pl.BlockSpec((None, 1, D_MODEL), gate_map),
            pl.BlockSpec((D_MODEL, tn), lambda i, j: (0, j)),
            pl.BlockSpec((D_MODEL, tn), lambda i, j: (0, nt + j)),
            pl.BlockSpec((3, tn), lambda i, j: (0, j)),
            pl.BlockSpec((3, tn), lambda i, j: (0, nt + j)),
            pl.BlockSpec((tn, D_MODEL), lambda i, j: (j, 0)),
            pl.BlockSpec((1, D_MODEL), lambda i, j: (0, 0)),
        ],
        out_specs=pl.BlockSpec((tm, D_MODEL), lambda i, j: (i, 0)),
        scratch_shapes=[pltpu.VMEM((tm, D_MODEL), F32),
                        pltpu.VMEM((tm + 2 * FFN_PAD_ROWS, tn), F32),
                        pltpu.VMEM((tm + 2 * FFN_PAD_ROWS, tn), F32)],
        compiler_params=_params("arbitrary", "arbitrary"),
        name="conv_ffn",
    )(h2, x1, gate2, ffn_up, ffn_up, ffn_conv, ffn_conv, ffn_down, final_g)


def _layer_path(x, mod, w, rope, cache, s0):
    b, t, _ = x.shape
    x2d = x.reshape(b * t, D_MODEL)
    qkv, rkv, lora, sga, sgr = _in_projection(x2d, mod, w["norm1_g"], w["w_in"], t)
    qkv = qkv.reshape(b, t, ATTN_WIDTH + 2 * KV_WIDTH)
    if rope is None:
        attn_o, k_norm, v_hm = _attention(qkv, w["q_norm_g"], w["k_norm_g"])
    else:
        attn_o = _attention(qkv, w["q_norm_g"], w["k_norm_g"], cache=cache, rope=rope)
        k_norm = v_hm = None
    rwkv_o, s_fin = _rwkv(rkv.reshape(b, t, 3 * R_WIDTH), lora.reshape(b, t, LORA_WIDTH),
                          w["rwkv"], s0)
    x1, h2 = _merge(x2d, attn_o.reshape(b * t, ATTN_WIDTH), rwkv_o.reshape(b * t, R_WIDTH),
                    sga, sgr, mod, w["norm2_g"], w["w_attn_br"], w["w_rwkv_br"], w["w_out"], t)
    gate2 = mod[:, 5:6, :]
    y = _ffn(h2, x1, gate2, w["ffn_up"], w["ffn_conv"], w["ffn_down"], w["final_norm_g"], t)
    return y.reshape(b, t, D_MODEL), k_norm, v_hm, s_fin


def _rope_tables(rows):
    row = jnp.repeat(jnp.arange(rows), GRID_W).astype(F32)
    col = jnp.tile(jnp.arange(GRID_W), rows).astype(F32)
    inv = 1.0 / (ROPE_THETA ** (jnp.arange(ROPE_PAIRS, dtype=F32) / ROPE_PAIRS))
    ang = jnp.concatenate([row[:, None] * inv, col[:, None] * inv], axis=-1)
    cos, sin = jnp.cos(ang), jnp.sin(ang)
    return jnp.concatenate([cos, cos] * 2, axis=-1), jnp.concatenate([-sin, sin] * 2, axis=-1)


def kernel(x_prompt, x_sample, cache_k, cache_v, state_rwkv, c, c_ctx, w_ada, b_ada, norm1_g, w_in, q_norm_g, k_norm_g, rwkv_conv, rwkv_w0, rwkv_w2, rwkv_a0, rwkv_a2, rwkv_g2, rwkv_k_k, rwkv_k_a, rwkv_r_k, rwkv_ln_g, rwkv_ln_b, w_attn_br, w_rwkv_br, w_out, norm2_g, ffn_up, ffn_conv, ffn_down, final_norm_g):
    depth = w_in.shape[0]
    assert depth == 1, "single trunk layer"
    l = 0
    n_ctx = x_prompt.shape[0]
    n_lat = x_sample.shape[0]

    cc = jnp.concatenate([c_ctx[None, :], c, jnp.zeros((8 - 1 - n_lat, D_MODEL), F32)], axis=0)
    mod_all = _modulation(cc, w_ada[l], b_ada[l][None, :])
    mod_ctx = jnp.broadcast_to(mod_all[0:1].reshape(1, N_MOD, D_MODEL), (n_ctx, N_MOD, D_MODEL))
    mod_lat = mod_all[1:1 + n_lat].reshape(n_lat, N_MOD, D_MODEL)

    rwkv_p = {
        "conv": rwkv_conv[l], "w0": rwkv_w0[l], "w2": rwkv_w2[l], "a0": rwkv_a0[l],
        "a2": rwkv_a2[l], "g2": rwkv_g2[l], "k_k": rwkv_k_k[l][None, :],
        "k_a": rwkv_k_a[l][None, :], "r_k": rwkv_r_k[l].reshape(1, R_WIDTH),
        "ln_g": rwkv_ln_g[l][None, :], "ln_b": rwkv_ln_b[l][None, :],
    }
    w = {
        "norm1_g": norm1_g[l][None, :], "w_in": w_in[l].astype(BF16),
        "q_norm_g": q_norm_g[l][None, :], "k_norm_g": k_norm_g[l][None, :],
        "rwkv": rwkv_p,
        "w_attn_br": w_attn_br[l].astype(BF16), "w_rwkv_br": w_rwkv_br[l].astype(BF16),
        "w_out": w_out[l].astype(BF16), "norm2_g": norm2_g[l][None, :],
        "ffn_up": ffn_up[l].astype(BF16), "ffn_conv": ffn_conv[l],
        "ffn_down": ffn_down[l].astype(BF16), "final_norm_g": final_norm_g[None, :],
    }

    y_prompt, k_ctx, v_ctx, s_ctx = _layer_path(x_prompt, mod_ctx, w, None, None, None)
    rope = _rope_tables(x_sample.shape[1] // GRID_W)
    y_sample, _, _, _ = _layer_path(x_sample, mod_lat, w, rope, (cache_k[:, l], cache_v[:, l]),
                                    state_rwkv[:, l])
    return (y_prompt, y_sample, k_ctx[:, None], v_ctx[:, None], s_ctx[:, None])
```

```python
import functools

import jax
import jax.numpy as jnp
import numpy as np
from jax import lax
from jax.experimental import pallas as pl
from jax.experimental.pallas import tpu as pltpu

D_MODEL = 1024
GRID_W = 64
N_HEADS = 8
KV_HEADS = 2
Q_PER_KV = N_HEADS // KV_HEADS
HEAD_DIM = 64
ATTN_WIDTH = N_HEADS * HEAD_DIM
KV_WIDTH = KV_HEADS * HEAD_DIM
ROPE_THETA = 10000.0
ROPE_PAIRS = HEAD_DIM // 4
R_HEADS = 8
R_HEAD_DIM = 64
R_WIDTH = R_HEADS * R_HEAD_DIM
DECAY_RANK = 64
ICLR_RANK = 64
GATE_RANK = 128
LORA_WIDTH = DECAY_RANK + ICLR_RANK + GATE_RANK
R_IN_WIDTH = 3 * R_WIDTH + LORA_WIDTH
D_FF = 2816
N_MOD = 6
NORM_EPS = 1e-6
GN_EPS = 64e-5
DECAY_SCALE = float(np.exp(-0.5))

F32 = jnp.float32
BF16 = jnp.bfloat16

VMEM_LIMIT_BYTES = 56 * 1024 * 1024
PROJ_ROWS = 512
FFN_ROWS = 1024
FFN_TILE = 256
FFN_PAD_ROWS = 8
FFN_CONV_ROWS = 256
Q_ROWS = 512
KEY_CHUNK = 256
STEP_ROWS = 256
GROUP = 128
PAIRS_PER_STEP = 2
CHUNK = 64

_NN = (((1,), (0,)), ((), ()))
_NT = (((1,), (1,)), ((), ()))
_TN = (((0,), (0,)), ((), ()))


def _dg(a, b, dims):
    return lax.dot_general(a, b, dims, preferred_element_type=F32)


def _dot(a, b):
    return _dg(a, b, _NN)


def _mm(a, b, dims=_NN):
    return _dg(a.astype(BF16), b.astype(BF16), dims)


def _mm_exact_lhs(m, x):
    hi = x.astype(BF16)
    r1 = x - hi.astype(F32)
    mid = r1.astype(BF16)
    lo = (r1 - mid.astype(F32)).astype(BF16)
    return _dot(m, hi) + (_dot(m, mid) + _dot(m, lo))


def _sigmoid(x):
    return 1.0 / (1.0 + jnp.exp(-x))


def _rms(x):
    return x * lax.rsqrt(jnp.mean(x * x, axis=-1, keepdims=True) + NORM_EPS)


def _params(*semantics):
    return pltpu.CompilerParams(dimension_semantics=semantics,
                                vmem_limit_bytes=VMEM_LIMIT_BYTES)


def _mod_kernel(c_ref, w_ref, b_ref, o_ref):
    c = c_ref[...]
    s = (c * _sigmoid(c)).astype(BF16)
    o_ref[...] = _dot(s, w_ref[...].astype(BF16)) + b_ref[...]


def _modulation(cc, w_ada, b_ada):
    rows = cc.shape[0]
    n = w_ada.shape[1]
    tile = 1024
    return pl.pallas_call(
        _mod_kernel,
        out_shape=jax.ShapeDtypeStruct((rows, n), F32),
        grid=(n // tile,),
        in_specs=[
            pl.BlockSpec((rows, D_MODEL), lambda j: (0, 0)),
            pl.BlockSpec((D_MODEL, tile), lambda j: (0, j)),
            pl.BlockSpec((1, tile), lambda j: (0, j)),
        ],
        out_specs=pl.BlockSpec((rows, tile), lambda j: (0, j)),
        compiler_params=_params("arbitrary"),
        name="adaln_mod",
    )(cc, w_ada, b_ada)


def _inproj_kernel(x_ref, mod_ref, g_ref, w_ref, qkv_ref, rkv_ref, lora_ref, sga_ref, sgr_ref):
    x = x_ref[...]
    h = _rms(x) * g_ref[...]
    h = h * (1.0 + mod_ref[1:2, :]) + mod_ref[0:1, :]
    hb = h.astype(BF16)
    c0 = ATTN_WIDTH + 2 * KV_WIDTH
    c1 = c0 + 3 * R_WIDTH
    c2 = c1 + LORA_WIDTH
    c3 = c2 + D_MODEL
    qkv_ref[...] = _dot(hb, w_ref[:, 0:c0])
    rkv_ref[...] = _dot(hb, w_ref[:, c0:c1])
    lora_ref[...] = _dot(hb, w_ref[:, c1:c2])
    sga_ref[...] = _sigmoid(_dot(hb, w_ref[:, c2:c3])).astype(BF16)
    sgr_ref[...] = _sigmoid(_dot(hb, w_ref[:, c3:c3 + D_MODEL])).astype(BF16)


def _in_projection(x2d, mod, norm1_g, w_in_bf16, seq_len):
    n_tok = x2d.shape[0]
    tm = PROJ_ROWS
    in_width = w_in_bf16.shape[1]
    widths = (ATTN_WIDTH + 2 * KV_WIDTH, 3 * R_WIDTH, LORA_WIDTH, D_MODEL, D_MODEL)
    dtypes = (F32, F32, F32, BF16, BF16)
    row = lambda i: (i, 0)
    return pl.pallas_call(
        _inproj_kernel,
        out_shape=tuple(jax.ShapeDtypeStruct((n_tok, w), dt) for w, dt in zip(widths, dtypes)),
        grid=(n_tok // tm,),
        in_specs=[
            pl.BlockSpec((tm, D_MODEL), row),
            pl.BlockSpec((None, N_MOD, D_MODEL), lambda i: ((i * tm) // seq_len, 0, 0)),
            pl.BlockSpec((1, D_MODEL), lambda i: (0, 0)),
            pl.BlockSpec((D_MODEL, in_width), lambda i: (0, 0)),
        ],
        out_specs=tuple(pl.BlockSpec((tm, w), row) for w in widths),
        compiler_params=_params("arbitrary"),
        name="in_proj",
    )(x2d, mod, norm1_g, w_in_bf16)


def _attn_kernel(*refs, seq_len, past_len, latent):
    if latent:
        (q_ref, k_ref, v_ref, ck_ref, cv_ref, cos_ref, sin_ref, qg_ref, kg_ref,
         o_ref, qs_ref, kd_ref, vd_ref) = refs
    else:
        (q_ref, k_ref, v_ref, qg_ref, kg_ref, o_ref, kn_ref, vn_ref, qs_ref, kd_ref, vd_ref) = refs
    t = seq_len
    d = HEAD_DIM
    width = 2 * d
    total = t + past_len
    group = pl.program_id(1)
    lane = lax.broadcasted_iota(jnp.int32, (1, width), 1)
    head_of_lane = lane // d
    low_half = (lane % d) < d // 2
    inv_d = 1.0 / d

    def pair_norm(x, gain):
        sq = x * x
        s0 = jnp.sum(jnp.where(head_of_lane == 0, sq, 0.0), axis=-1, keepdims=True)
        s1 = jnp.sum(jnp.where(head_of_lane == 0, 0.0, sq), axis=-1, keepdims=True)
        ms = jnp.where(head_of_lane == 0, s0, s1) * inv_d
        y = x * lax.rsqrt(ms + NORM_EPS) * gain
        if latent:
            partner = jnp.where(low_half, pltpu.roll(y, width - d // 2, 1), pltpu.roll(y, d // 2, 1))
            y = y * cos_ref[...] + partner * sin_ref[...]
        return y

    def this_group(x):
        return jnp.where(head_of_lane == group, x, pltpu.roll(x, d, 1))

    kg2 = jnp.concatenate([kg_ref[...], kg_ref[...]], axis=-1)
    qg2 = jnp.concatenate([qg_ref[...], qg_ref[...]], axis=-1)
    v_dup = this_group(v_ref[...])
    if latent:
        kd_ref[0:t, :] = this_group(pair_norm(k_ref[...], kg2)).astype(BF16)
        ck = ck_ref[...].astype(BF16)
        kd_ref[t:total, :] = jnp.concatenate([ck, ck], axis=-1)
        vd_ref[0:t, :] = v_dup.astype(BF16)
        cv = cv_ref[...].astype(BF16)
        vd_ref[t:total, :] = jnp.concatenate([cv, cv], axis=-1)
    else:
        k_dup = this_group(pair_norm(k_ref[...], kg2))
        kn_ref[...] = k_dup[:, 0:d]
        vn_ref[...] = v_dup[:, 0:d]
        kd_ref[...] = k_dup.astype(BF16)
        vd_ref[...] = v_dup.astype(BF16)
    scale = d ** -0.5
    for j in range(Q_PER_KV // 2):
        cols = slice(j * width, (j + 1) * width)
        qs_ref[:, cols] = (pair_norm(q_ref[:, cols], qg2) * scale).astype(BF16)
    n_chunks = total // KEY_CHUNK
    q_rows = min(Q_ROWS, t)

    def body(qb, carry):
        r0 = pl.multiple_of(qb * q_rows, q_rows)
        outs = []
        for j in range(Q_PER_KV // 2):
            qp = qs_ref[pl.ds(r0, q_rows), j * width:(j + 1) * width]
            heads = []
            for h in range(2):
                qm = jnp.where(head_of_lane == h, qp, jnp.zeros_like(qp))
                m = l = acc = None
                for c in range(n_chunks):
                    keys = slice(c * KEY_CHUNK, (c + 1) * KEY_CHUNK)
                    s = _dg(qm, kd_ref[keys, :], _NT)
                    m_c = jnp.max(s, axis=-1, keepdims=True)
                    if c == 0:
                        m = m_c
                        p = jnp.exp(s - m)
                        l = jnp.sum(p, axis=-1, keepdims=True)
                        acc = _dot(p.astype(BF16), vd_ref[keys, :])
                    else:
                        m_new = jnp.maximum(m, m_c)
                        alpha = jnp.exp(m - m_new)
                        p = jnp.exp(s - m_new)
                        l = alpha * l + jnp.sum(p, axis=-1, keepdims=True)
                        acc = alpha * acc + _dot(p.astype(BF16), vd_ref[keys, :])
                        m = m_new
                heads.append(acc / l)
            outs.append(jnp.where(head_of_lane == 0, heads[0], heads[1]))
        o_ref[pl.ds(r0, q_rows), :] = jnp.concatenate(outs, axis=-1).astype(BF16)
        return carry

    lax.fori_loop(0, t // q_rows, body, 0)


def _attention(qkv, qg, kg, cache=None, rope=None):
    b, t, _ = qkv.shape
    d = HEAD_DIM
    latent = cache is not None
    past_len = cache[0].shape[2] if latent else 0
    total = t + past_len
    assert total % KEY_CHUNK == 0 and t % min(Q_ROWS, t) == 0 and KV_WIDTH == 2 * d
    gw = Q_PER_KV * d
    q_spec = pl.BlockSpec((None, t, gw), lambda i, g: (i, 0, g))
    k_spec = pl.BlockSpec((None, t, KV_WIDTH), lambda i, g: (i, 0, ATTN_WIDTH // KV_WIDTH))
    v_spec = pl.BlockSpec((None, t, KV_WIDTH), lambda i, g: (i, 0, ATTN_WIDTH // KV_WIDTH + 1))
    hm_spec = pl.BlockSpec((None, None, t, d), lambda i, g: (i, g, 0, 0))
    vec_spec = pl.BlockSpec((1, d), lambda i, g: (0, 0))
    out_shape = jax.ShapeDtypeStruct((b, t, ATTN_WIDTH), BF16)
    scratch = [pltpu.VMEM((t, gw), BF16), pltpu.VMEM((total, 2 * d), BF16),
               pltpu.VMEM((total, 2 * d), BF16)]
    kern = functools.partial(_attn_kernel, seq_len=t, past_len=past_len, latent=latent)
    if latent:
        ck, cv = cache
        cos4, sin4 = rope
        c_spec = pl.BlockSpec((None, None, past_len, d), lambda i, g: (i, g, 0, 0))
        tab_spec = pl.BlockSpec((t, 2 * d), lambda i, g: (0, 0))
        return pl.pallas_call(
            kern,
            out_shape=out_shape,
            grid=(b, KV_HEADS),
            in_specs=[q_spec, k_spec, v_spec, c_spec, c_spec, tab_spec, tab_spec,
                      vec_spec, vec_spec],
            out_specs=q_spec,
            scratch_shapes=scratch,
            compiler_params=_params("arbitrary", "arbitrary"),
            name="attn_latent",
        )(qkv, qkv, qkv, ck, cv, cos4, sin4, qg, kg)
    hm_shape = jax.ShapeDtypeStruct((b, KV_HEADS, t, d), F32)
    return pl.pallas_call(
        kern,
        out_shape=(out_shape, hm_shape, hm_shape),
        grid=(b, KV_HEADS),
        in_specs=[q_spec, k_spec, v_spec, vec_spec, vec_spec],
        out_specs=(q_spec, hm_spec, hm_spec),
        scratch_shapes=scratch,
        compiler_params=_params("arbitrary", "arbitrary"),
        name="attn_context",
    )(qkv, qkv, qkv, qg, kg)


def _conv_rows(ref_rows, taps, start, n, total):
    cur = ref_rows(start, n)
    width = cur.shape[-1]
    zero = jnp.zeros((1, width), F32)
    prev_row = ref_rows(start - 1, 1) if start > 0 else zero
    next_row = ref_rows(start + n, 1) if start + n < total else zero
    ridx = lax.broadcasted_iota(jnp.int32, (n, 1), 0)
    prev = jnp.where(ridx == 0, prev_row, pltpu.roll(cur, 1, 0))
    nxt = jnp.where(ridx == n - 1, next_row, pltpu.roll(cur, n - 1, 0))
    return prev * taps[0] + cur * taps[1] + nxt * taps[2]


def _round_robin(chains):
    results = [None] * len(chains)
    live = list(range(len(chains)))
    while live:
        for i in list(live):
            try:
                next(chains[i])
            except StopIteration as done:
                results[i] = done.value
                live.remove(i)
    return results


def _rwkv_kernel(*refs, seq_len, has_state):
    n_in = 19 if has_state else 18
    (r_ref, k_ref, v_ref, lora_ref, cr_ref, ck_ref, cv_ref, cl_ref, w0_ref, w2_ref, a0_ref, a2_ref,
     g2_ref, kk_ref, ka_ref, rk_ref, lng_ref, lnb_ref) = refs[:18]
    s0_ref = refs[18] if has_state else None
    y_ref, sfin_ref, rs_ref, ks_ref, vs_ref, ls_ref, yacc_ref, bacc_ref, st_ref = refs[n_in:]
    t = seq_len
    n_steps = t // STEP_ROWS
    n_sub = STEP_ROWS // GROUP
    n_chunks = GROUP // CHUNK
    n = R_HEAD_DIM
    width = 2 * n
    step = pl.program_id(2)
    pair_lanes = [slice(q * width, (q + 1) * width) for q in range(PAIRS_PER_STEP)]

    @pl.when(step == 0)
    def _():
        for blk in range(n_steps):
            start = blk * STEP_ROWS
            rows = slice(start, start + STEP_ROWS)
            for src, cw, dst in ((r_ref, cr_ref, rs_ref), (k_ref, ck_ref, ks_ref),
                                 (v_ref, cv_ref, vs_ref), (lora_ref, cl_ref, ls_ref)):
                taps = [cw[j:j + 1, :] for j in range(3)]
                dst[rows, :] = _conv_rows(lambda s, m, src=src: src[s:s + m, :],
                                          taps, start, STEP_ROWS, t)
        zero = jnp.zeros((n, n), F32)
        for d in range(2):
            for q in range(PAIRS_PER_STEP):
                if has_state:
                    s_a, s_b = s0_ref[d, 2 * q], s0_ref[d, 2 * q + 1]
                    bd = jnp.concatenate([jnp.concatenate([s_a, zero], axis=1),
                                          jnp.concatenate([zero, s_b], axis=1)], axis=0)
                    st_ref[d, q] = bd.T
                else:
                    st_ref[d, q] = jnp.zeros((width, width), F32)

    lane = lax.broadcasted_iota(jnp.int32, (1, width), 1)
    head0 = lane < R_HEAD_DIM

    def per_head(x):
        x0 = jnp.where(head0, x, 0.0)
        return (x0, x - x0)

    def pick(x0, x1):
        return jnp.where(head0, x0, x1)

    def pair_mm(mats, x):
        return pick(*[_mm(m, x) for m in mats])

    def pair_mm2(mats, x, y):
        xy = jnp.concatenate([x, y], axis=1)
        prods = [_mm(m, xy) for m in mats]
        return (pick(*[p[:, :width] for p in prods]), pick(*[p[:, width:] for p in prods]))

    def head_sum(x):
        s0 = jnp.sum(jnp.where(head0, x, 0.0), axis=-1, keepdims=True)
        s1 = jnp.sum(jnp.where(head0, 0.0, x), axis=-1, keepdims=True)
        return jnp.where(head0, s0, s1)

    def scan_rows(d, q, blk, state):
        rev = d == 1
        lanes = pair_lanes[q]
        base = blk * STEP_ROWS
        ri = lax.broadcasted_iota(jnp.int32, (GROUP, GROUP), 0)
        ci = lax.broadcasted_iota(jnp.int32, (GROUP, GROUP), 1)
        same = (ri // CHUNK) == (ci // CHUNK)
        eye = (ri == ci).astype(F32)
        strict = same & ((ci > ri) if rev else (ci < ri))
        incl = same & ((ci >= ri) if rev else (ci <= ri))
        tri = incl.astype(BF16)

        def half_mask(m):
            return ((ri // (2 * m)) == (ci // (2 * m))) & ((ri // m) != (ci // m))

        subs = []
        for s in (range(n_sub - 1, -1, -1) if rev else range(n_sub)):
            rows = pl.ds(pl.multiple_of(base + s * GROUP, GROUP), GROUP)
            lo = ls_ref[rows, :]
            wd_t = jnp.tanh(lo[:, 0:DECAY_RANK])
            z = w0_ref[d:d + 1, lanes] + _mm(wd_t, w2_ref[d, :, lanes])
            subs.append({"rows": rows, "r": rs_ref[rows, lanes], "k": ks_ref[rows, lanes],
                         "v": vs_ref[rows, lanes], "ad": lo[:, DECAY_RANK:DECAY_RANK + ICLR_RANK],
                         "lw": -DECAY_SCALE * _sigmoid(z)})
        yield
        for p in subs:
            p["cum"] = _mm_exact_lhs(tri, p["lw"])
            p["cum_c"] = jnp.concatenate(
                [jnp.broadcast_to(p["cum"][r:r + 1, :], (CHUNK, width))
                 for r in (range(0, GROUP, CHUNK) if rev else range(CHUNK - 1, GROUP, CHUNK))], axis=0)
            a = _sigmoid(a0_ref[d:d + 1, lanes] + _mm(p["ad"], a2_ref[d, :, lanes]))
            p["kd"] = p["k"] * (1.0 + (a - 1.0) * ka_ref[:, lanes])
            kk = p["k"] * kk_ref[:, lanes]
            p["kk"] = kk * lax.rsqrt(head_sum(kk * kk) + 1e-12)
            p["b"] = p["kk"] * a
            bacc_ref[d, p["rows"], lanes] = head_sum(p["r"] * p["kd"] * rk_ref[:, lanes]) * p["v"]
        yield
        for p in subs:
            cum, cum_c = p["cum"], p["cum_c"]
            p["a_t"] = -p["kk"] * jnp.exp(cum - p["lw"])
            p["r_t"] = p["r"] * jnp.exp(cum)
            inv = jnp.exp(-cum)
            p["b_t"] = p["b"] * inv
            p["k_t"] = p["kd"] * inv
            tail = jnp.exp(cum_c - cum)
            p["b_h"] = p["b"] * tail
            p["k_h"] = p["kd"] * tail
            p["p_c"] = jnp.exp(cum_c)
        yield
        for p in subs:
            bk = jnp.concatenate([p["b_t"], p["k_t"]], axis=0)
            xa = [_mm(x, bk, _NT) for x in per_head(p["a_t"])]
            xr = [_mm(x, bk, _NT) for x in per_head(p["r_t"])]
            p["a_ab"] = [jnp.where(strict, m[:, :GROUP], 0.0) for m in xa]
            p["a_ak"] = [jnp.where(strict, m[:, GROUP:], 0.0) for m in xa]
            p["a_rb"] = [jnp.where(incl, m[:, :GROUP], 0.0) for m in xr]
            p["a_rk"] = [jnp.where(incl, m[:, GROUP:], 0.0) for m in xr]
        yield
        for p in subs:
            p["akv"] = pair_mm(p["a_ak"], p["v"])
            p["rkv"] = pair_mm(p["a_rk"], p["v"])
            p["tm"] = [eye + jnp.where(half_mask(1), m, 0.0) for m in p["a_ab"]]
        m_size = 2
        while m_size < CHUNK:
            yield
            off = half_mask(m_size)
            for p in subs:
                p["ot"] = [_mm(jnp.where(off, n_mat, 0.0), tm)
                           for n_mat, tm in zip(p["a_ab"], p["tm"])]
            yield
            for p in subs:
                p["tm"] = [tm + _mm(tm, ot) for tm, ot in zip(p["tm"], p["ot"])]
            m_size *= 2
        yield
        for p in subs:
            p["ta"], p["tk"] = pair_mm2(p["tm"], p["a_t"], p["akv"])
        yield
        cr = [slice(c * CHUNK, (c + 1) * CHUNK) for c in range(n_chunks)]
        for p in subs:
            ra, rk = pair_mm2(p["a_rb"], p["ta"], p["tk"])
            p["r2"] = p["r_t"] + ra
            p["y2"] = p["rkv"] + rk
            p["m_t"] = [eye * p["p_c"][c * CHUNK:c * CHUNK + 1, :]
                        + jnp.where(same, _mm(p["b_h"][cr[c]], p["ta"][cr[c]], _TN), 0.0)
                        for c in range(n_chunks)]
            p["g_t"] = [jnp.where(same, _mm(jnp.concatenate([p["b_h"][cr[c]], p["k_h"][cr[c]]], axis=0),
                                            jnp.concatenate([p["tk"][cr[c]], p["v"][cr[c]]], axis=0),
                                            _TN), 0.0)
                        for c in range(n_chunks)]
        for p in subs:
            ys = [None] * n_chunks
            for c in (range(n_chunks - 1, -1, -1) if rev else range(n_chunks)):
                yield
                ys[c] = _mm(p["r2"][cr[c]], state) + p["y2"][cr[c]]
                state = _mm(p["m_t"][c], state) + p["g_t"][c]
            yacc_ref[d, p["rows"], lanes] = jnp.concatenate(ys, axis=0)
        return state

    chains = [(d, q) for q in range(PAIRS_PER_STEP) for d in range(2)]
    blocks = (step, n_steps - 1 - step)
    new_state = _round_robin([scan_rows(d, q, blocks[d], st_ref[d, q]) for d, q in chains])
    for (d, q), s_new in zip(chains, new_state):
        st_ref[d, q] = s_new

    @pl.when(step == n_steps - 1)
    def _():
        for d, q in chains:
            s_vk = st_ref[d, q].T
            sfin_ref[d, 2 * q] = s_vk[0:n, 0:n]
            sfin_ref[d, 2 * q + 1] = s_vk[n:width, n:width]
        inv_n = 1.0 / n
        for blk in range(n_steps):
            rows = slice(blk * STEP_ROWS, (blk + 1) * STEP_ROWS)
            gd = _sigmoid(ls_ref[rows, DECAY_RANK + ICLR_RANK:LORA_WIDTH]).astype(BF16)
            for lanes in pair_lanes:
                y = yacc_ref[0, rows, lanes] + yacc_ref[1, rows, lanes]
                yc = y - head_sum(y) * inv_n
                yn = yc * lax.rsqrt(head_sum(yc * yc) * inv_n + GN_EPS)
                out = (yn * lng_ref[:, lanes] + lnb_ref[:, lanes]
                       + (bacc_ref[0, rows, lanes] + bacc_ref[1, rows, lanes]))
                y_ref[rows, lanes] = (out * _dot(gd, g2_ref[:, lanes].astype(BF16))).astype(BF16)


def _rwkv(rkv, lora, p, s0):
    n_seq, t, _ = rkv.shape
    has_state = s0 is not None
    n = R_HEAD_DIM
    heads = 2 * PAIRS_PER_STEP
    width = heads * n
    blocks = R_WIDTH // width
    col = lambda off: (lambda s, i, j: (s, 0, off + i))
    vec = lambda off: (lambda s, i, j: (0, off + i))
    dir_mat = lambda s, i, j: (0, 0, i)
    in_specs = [
        pl.BlockSpec((None, t, width), col(0)),
        pl.BlockSpec((None, t, width), col(blocks)),
        pl.BlockSpec((None, t, width), col(2 * blocks)),
        pl.BlockSpec((None, t, LORA_WIDTH), lambda s, i, j: (s, 0, 0)),
        pl.BlockSpec((3, width), vec(0)),
        pl.BlockSpec((3, width), vec(blocks)),
        pl.BlockSpec((3, width), vec(2 * blocks)),
        pl.BlockSpec((3, LORA_WIDTH), lambda s, i, j: (0, 3 * R_WIDTH // LORA_WIDTH)),
        pl.BlockSpec((2, width), vec(0)),
        pl.BlockSpec((2, DECAY_RANK, width), dir_mat),
        pl.BlockSpec((2, width), vec(0)),
        pl.BlockSpec((2, ICLR_RANK, width), dir_mat),
        pl.BlockSpec((GATE_RANK, width), vec(0)),
        pl.BlockSpec((1, width), vec(0)),
        pl.BlockSpec((1, width), vec(0)),
        pl.BlockSpec((1, width), vec(0)),
        pl.BlockSpec((1, width), vec(0)),
        pl.BlockSpec((1, width), vec(0)),
    ]
    conv = p["conv"]
    args = [rkv, rkv, rkv, lora, conv, conv, conv, conv, p["w0"], p["w2"], p["a0"], p["a2"],
            p["g2"], p["k_k"], p["k_a"], p["r_k"], p["ln_g"], p["ln_b"]]
    state_spec = pl.BlockSpec((None, 2, heads, n, n), lambda s, i, j: (s, 0, i, 0, 0))
    if has_state:
        in_specs.append(state_spec)
        args.append(s0)
    seq_buf = pltpu.VMEM((t, width), F32)
    dir_buf = pltpu.VMEM((2, t, width), F32)
    return pl.pallas_call(
        functools.partial(_rwkv_kernel, seq_len=t, has_state=has_state),
        out_shape=(jax.ShapeDtypeStruct((n_seq, t, R_WIDTH), BF16),
                   jax.ShapeDtypeStruct((n_seq, 2, R_HEADS, n, n), F32)),
        grid=(n_seq, blocks, t // STEP_ROWS),
        in_specs=in_specs,
        out_specs=(pl.BlockSpec((None, t, width), col(0)), state_spec),
        scratch_shapes=[seq_buf, seq_buf, seq_buf, pltpu.VMEM((t, LORA_WIDTH), F32),
                        dir_buf, dir_buf, pltpu.VMEM((2, PAIRS_PER_STEP, 2 * n, 2 * n), F32)],
        compiler_params=_params("arbitrary", "arbitrary", "arbitrary"),
        name="rwkv7_latent" if has_state else "rwkv7_context",
    )(*args)


def _merge_kernel(x_ref, ao_ref, ro_ref, sga_ref, sgr_ref, mod_ref, g_ref, wab_ref, wrb_ref,
                  wo_ref, x1_ref, h2_ref):
    merged = (sga_ref[...] * _dot(ao_ref[...], wab_ref[...])
              + sgr_ref[...] * _dot(ro_ref[...], wrb_ref[...]))
    x1 = x_ref[...] + mod_ref[2:3, :] * _dot(merged.astype(BF16), wo_ref[...])
    x1_ref[...] = x1
    h2 = _rms(x1) * g_ref[...]
    h2_ref[...] = (h2 * (1.0 + mod_ref[4:5, :]) + mod_ref[3:4, :]).astype(BF16)


def _merge(x2d, attn_o, rwkv_o, sga, sgr, mod, norm2_g, wab, wrb, wo, seq_len):
    n_tok = x2d.shape[0]
    tm = PROJ_ROWS
    row = lambda i: (i, 0)
    const = lambda i: (0, 0)
    return pl.pallas_call(
        _merge_kernel,
        out_shape=(jax.ShapeDtypeStruct((n_tok, D_MODEL), F32),
                   jax.ShapeDtypeStruct((n_tok, D_MODEL), BF16)),
        grid=(n_tok // tm,),
        in_specs=[
            pl.BlockSpec((tm, D_MODEL), row),
            pl.BlockSpec((tm, ATTN_WIDTH), row),
            pl.BlockSpec((tm, R_WIDTH), row),
            pl.BlockSpec((tm, D_MODEL), row),
            pl.BlockSpec((tm, D_MODEL), row),
            pl.BlockSpec((None, N_MOD, D_MODEL), lambda i: ((i * tm) // seq_len, 0, 0)),
            pl.BlockSpec((1, D_MODEL), const),
            pl.BlockSpec((ATTN_WIDTH, D_MODEL), const),
            pl.BlockSpec((R_WIDTH, D_MODEL), const),
            pl.BlockSpec((D_MODEL, D_MODEL), const),
        ],
        out_specs=(pl.BlockSpec((tm, D_MODEL), row), pl.BlockSpec((tm, D_MODEL), row)),
        compiler_params=_params("arbitrary"),
        name="merge_out_proj",
    )(x2d, attn_o, rwkv_o, sga, sgr, mod, norm2_g, wab, wrb, wo)


def _ffn_kernel(h2_ref, x1_ref, mod_ref, wu_ref, wg_ref, cu_ref, cg_ref, wd_ref, fg_ref,
                y_ref, acc_ref, upu_ref, upg_ref, *, seq_len):
    i = pl.program_id(0)
    j = pl.program_id(1)
    rows = h2_ref.shape[0]
    pad = FFN_PAD_ROWS
    rb = FFN_CONV_ROWS
    tile = 8

    @pl.when((i == 0) & (j == 0))
    def _():
        for ref in (upu_ref, upg_ref):
            ref[0:pad, :] = jnp.zeros((pad, ref.shape[1]), F32)
            ref[pad + rows:pad + rows + pad, :] = jnp.zeros((pad, ref.shape[1]), F32)

    @pl.when(j == 0)
    def _():
        acc_ref[...] = jnp.zeros(acc_ref.shape, F32)

    def up_project(r0):
        h2 = h2_ref[r0:r0 + rb, :]
        upu_ref[pad + r0:pad + r0 + rb, :] = _dot(h2, wu_ref[...])
        upg_ref[pad + r0:pad + r0 + rb, :] = _dot(h2, wg_ref[...])

    row0 = lax.broadcasted_iota(jnp.int32, (tile, 1), 0) == 0
    row7 = lax.broadcasted_iota(jnp.int32, (tile, 1), 0) == tile - 1

    taps_u = [cu_ref[k:k + 1, :] for k in range(3)]
    taps_g = [cg_ref[k:k + 1, :] for k in range(3)]

    def zero_rows(x, tile_offsets, row_mask):
        parts, done = [], 0
        for off in tile_offsets:
            parts += [x[done:off], jnp.where(row_mask, 0.0, x[off:off + tile])]
            done = off + tile
        parts.append(x[done:])
        return jnp.concatenate([p for p in parts if p.shape[0]], axis=0)

    def conv(up_ref, taps, r0):
        base = pad + r0
        cur = up_ref[base:base + rb, :]
        prev = up_ref[base - 1:base - 1 + rb, :]
        nxt = up_ref[base + 1:base + 1 + rb, :]
        first_seq = -(-r0 // seq_len) * seq_len
        starts = [s - r0 for s in range(first_seq, r0 + rb, seq_len)]
        ends = [e - r0 for e in range(first_seq if first_seq > r0 else first_seq + seq_len,
                                      r0 + rb + 1, seq_len)]
        prev = zero_rows(prev, starts, row0)
        nxt = zero_rows(nxt, [e - tile for e in ends], row7)
        return prev * taps[0] + cur * taps[1] + nxt * taps[2]

    starts = list(range(0, rows, rb))
    for r0 in starts[:2]:
        up_project(r0)
    for b, r0 in enumerate(starts):
        up_u = conv(upu_ref, taps_u, r0)
        up_g = conv(upg_ref, taps_g, r0)
        act = (up_g * _sigmoid(up_g) * up_u).astype(BF16)
        if b + 2 < len(starts):
            up_project(starts[b + 2])
        acc_ref[r0:r0 + rb, :] += _dot(act, wd_ref[...])

    @pl.when(j == pl.num_programs(1) - 1)
    def _():
        x2 = x1_ref[...] + mod_ref[...] * acc_ref[...]
        y_ref[...] = _rms(x2) * fg_ref[...]


def _ffn(h2, x1, gate2, ffn_up, ffn_conv, ffn_down, final_g, seq_len):
    n_tok = h2.shape[0]
    tm = FFN_ROWS
    tn = FFN_TILE
    nt = D_FF // tn
    assert tm % seq_len == 0 or seq_len % tm == 0
    gate_map = lambda i, j: ((i * tm) // seq_len, 0, 0)
    return pl.pallas_call(
        functools.partial(_ffn_kernel, seq_len=seq_len),
        out_shape=jax.ShapeDtypeStruct((n_tok, D_MODEL), F32),
        grid=(n_tok // tm, nt),
        in_specs=[
            pl.BlockSpec((tm, D_MODEL), lambda i, j: (i, 0)),
            pl.BlockSpec((tm, D_MODEL), lambda i, j: (i, 0)),
            pl.BlockSpec((None, 1, D_MODEL), gate_map),
            pl.BlockSpec((D_MODEL, tn), lambda i, j: (0, j)),
            pl.BlockSpec((D_MODEL, tn), lambda i, j: (0, nt + j)),
            pl.BlockSpec((3, tn), lambda i, j: (0, j)),
            pl.BlockSpec((3, tn), lambda i, j: (0, nt + j)),
            pl.BlockSpec((tn, D_MODEL), lambda i, j: (j, 0)),
            pl.BlockSpec((1, D_MODEL), lambda i, j: (0, 0)),
        ],
        out_specs=pl.BlockSpec((tm, D_MODEL), lambda i, j: (i, 0)),
        scratch_shapes=[pltpu.VMEM((tm, D_MODEL), F32),
                        pltpu.VMEM((tm + 2 * FFN_PAD_ROWS, tn), F32),
                        pltpu.VMEM((tm + 2 * FFN_PAD_ROWS, tn), F32)],
        compiler_params=_params("arbitrary", "arbitrary"),
        name="conv_ffn",
    )(h2, x1, gate2, ffn_up, ffn_up, ffn_conv, ffn_conv, ffn_down, final_g)


def _layer_path(x, mod, w, rope, cache, s0):
    b, t, _ = x.shape
    x2d = x.reshape(b * t, D_MODEL)
    qkv, rkv, lora, sga, sgr = _in_projection(x2d, mod, w["norm1_g"], w["w_in"], t)
    qkv = qkv.reshape(b, t, ATTN_WIDTH + 2 * KV_WIDTH)
    if rope is None:
        attn_o, k_norm, v_hm = _attention(qkv, w["q_norm_g"], w["k_norm_g"])
    else:
        attn_o = _attention(qkv, w["q_norm_g"], w["k_norm_g"], cache=cache, rope=rope)
        k_norm = v_hm = None
    rwkv_o, s_fin = _rwkv(rkv.reshape(b, t, 3 * R_WIDTH), lora.reshape(b, t, LORA_WIDTH),
                          w["rwkv"], s0)
    x1, h2 = _merge(x2d, attn_o.reshape(b * t, ATTN_WIDTH), rwkv_o.reshape(b * t, R_WIDTH),
                    sga, sgr, mod, w["norm2_g"], w["w_attn_br"], w["w_rwkv_br"], w["w_out"], t)
    gate2 = mod[:, 5:6, :]
    y = _ffn(h2, x1, gate2, w["ffn_up"], w["ffn_conv"], w["ffn_down"], w["final_norm_g"], t)
    return y.reshape(b, t, D_MODEL), k_norm, v_hm, s_fin


def _rope_tables(rows):
    row = jnp.repeat(jnp.arange(rows), GRID_W).astype(F32)
    col = jnp.tile(jnp.arange(GRID_W), rows).astype(F32)
    inv = 1.0 / (ROPE_THETA ** (jnp.arange(ROPE_PAIRS, dtype=F32) / ROPE_PAIRS))
    ang = jnp.concatenate([row[:, None] * inv, col[:, None] * inv], axis=-1)
    cos, sin = jnp.cos(ang), jnp.sin(ang)
    return jnp.concatenate([cos, cos] * 2, axis=-1), jnp.concatenate([-sin, sin] * 2, axis=-1)


def kernel(x_prompt, x_sample, cache_k, cache_v, state_rwkv, c, c_ctx, w_ada, b_ada, norm1_g, w_in, q_norm_g, k_norm_g, rwkv_conv, rwkv_w0, rwkv_w2, rwkv_a0, rwkv_a2, rwkv_g2, rwkv_k_k, rwkv_k_a, rwkv_r_k, rwkv_ln_g, rwkv_ln_b, w_attn_br, w_rwkv_br, w_out, norm2_g, ffn_up, ffn_conv, ffn_down, final_norm_g):
    depth = w_in.shape[0]
    assert depth == 1, "single trunk layer"
    l = 0
    n_ctx = x_prompt.shape[0]
    n_lat = x_sample.shape[0]

    cc = jnp.concatenate([c_ctx[None, :], c, jnp.zeros((8 - 1 - n_lat, D_MODEL), F32)], axis=0)
    mod_all = _modulation(cc, w_ada[l], b_ada[l][None, :])
    mod_ctx = jnp.broadcast_to(mod_all[0:1].reshape(1, N_MOD, D_MODEL), (n_ctx, N_MOD, D_MODEL))
    mod_lat = mod_all[1:1 + n_lat].reshape(n_lat, N_MOD, D_MODEL)

    rwkv_p = {
        "conv": rwkv_conv[l], "w0": rwkv_w0[l], "w2": rwkv_w2[l], "a0": rwkv_a0[l],
        "a2": rwkv_a2[l], "g2": rwkv_g2[l], "k_k": rwkv_k_k[l][None, :],
        "k_a": rwkv_k_a[l][None, :], "r_k": rwkv_r_k[l].reshape(1, R_WIDTH),
        "ln_g": rwkv_ln_g[l][None, :], "ln_b": rwkv_ln_b[l][None, :],
    }
    w = {
        "norm1_g": norm1_g[l][None, :], "w_in": w_in[l].astype(BF16),
        "q_norm_g": q_norm_g[l][None, :], "k_norm_g": k_norm_g[l][None, :],
        "rwkv": rwkv_p,
        "w_attn_br": w_attn_br[l].astype(BF16), "w_rwkv_br": w_rwkv_br[l].astype(BF16),
        "w_out": w_out[l].astype(BF16), "norm2_g": norm2_g[l][None, :],
        "ffn_up": ffn_up[l].astype(BF16), "ffn_conv": ffn_conv[l],
        "ffn_down": ffn_down[l].astype(BF16), "final_norm_g": final_norm_g[None, :],
    }

    y_prompt, k_ctx, v_ctx, s_ctx = _layer_path(x_prompt, mod_ctx, w, None, None, None)
    rope = _rope_tables(x_sample.shape[1] // GRID_W)
    y_sample, _, _, _ = _layer_path(x_sample, mod_lat, w, rope, (cache_k[:, l], cache_v[:, l]),
                                    state_rwkv[:, l])
    return (y_prompt, y_sample, k_ctx[:, None], v_ctx[:, None], s_ctx[:, None])
```

```python
import functools

import jax
import jax.numpy as jnp
import numpy as np
from jax import lax
from jax.experimental import pallas as pl
from jax.experimental.pallas import tpu as pltpu

D_MODEL = 1024
GRID_W = 64
N_HEADS = 8
KV_HEADS = 2
Q_PER_KV = N_HEADS // KV_HEADS
HEAD_DIM = 64
ATTN_WIDTH = N_HEADS * HEAD_DIM
KV_WIDTH = KV_HEADS * HEAD_DIM
ROPE_THETA = 10000.0
ROPE_PAIRS = HEAD_DIM // 4
R_HEADS = 8
R_HEAD_DIM = 64
R_WIDTH = R_HEADS * R_HEAD_DIM
DECAY_RANK = 64
ICLR_RANK = 64
GATE_RANK = 128
LORA_WIDTH = DECAY_RANK + ICLR_RANK + GATE_RANK
R_IN_WIDTH = 3 * R_WIDTH + LORA_WIDTH
D_FF = 2816
N_MOD = 6
NORM_EPS = 1e-6
GN_EPS = 64e-5
DECAY_SCALE = float(np.exp(-0.5))

F32 = jnp.float32
BF16 = jnp.bfloat16

VMEM_LIMIT_BYTES = 56 * 1024 * 1024
PROJ_ROWS = 512
FFN_ROWS = 1024
FFN_TILE = 256
FFN_PAD_ROWS = 8
FFN_CONV_ROWS = 256
Q_ROWS = 512
KEY_CHUNK = 256
STEP_ROWS = 512
GROUP = 128
PAIRS_PER_STEP = 2
CHUNK = 64

_NN = (((1,), (0,)), ((), ()))
_NT = (((1,), (1,)), ((), ()))
_TN = (((0,), (0,)), ((), ()))


def _dg(a, b, dims):
    return lax.dot_general(a, b, dims, preferred_element_type=F32)


def _dot(a, b):
    return _dg(a, b, _NN)


def _mm(a, b, dims=_NN):
    return _dg(a.astype(BF16), b.astype(BF16), dims)


def _mm_exact_lhs(m, x):
    hi = x.astype(BF16)
    r1 = x - hi.astype(F32)
    mid = r1.astype(BF16)
    lo = (r1 - mid.astype(F32)).astype(BF16)
    return _dot(m, hi) + (_dot(m, mid) + _dot(m, lo))


def _mm_exact_lhs_t(x, m):
    hi = x.astype(BF16)
    r1 = x - hi.astype(F32)
    mid = r1.astype(BF16)
    lo = (r1 - mid.astype(F32)).astype(BF16)
    return _dot(hi, m) + (_dot(mid, m) + _dot(lo, m))


def _sigmoid(x):
    return 1.0 / (1.0 + jnp.exp(-x))


def _rms(x):
    return x * lax.rsqrt(jnp.mean(x * x, axis=-1, keepdims=True) + NORM_EPS)


def _params(*semantics):
    return pltpu.CompilerParams(dimension_semantics=semantics,
                                vmem_limit_bytes=VMEM_LIMIT_BYTES)


def _mod_kernel(c_ref, w_ref, b_ref, o_ref):
    c = c_ref[...]
    s = (c * _sigmoid(c)).astype(BF16)
    o_ref[...] = _dot(s, w_ref[...].astype(BF16)) + b_ref[...]


def _modulation(cc, w_ada, b_ada):
    rows = cc.shape[0]
    n = w_ada.shape[1]
    tile = 1024
    return pl.pallas_call(
        _mod_kernel,
        out_shape=jax.ShapeDtypeStruct((rows, n), F32),
        grid=(n // tile,),
        in_specs=[
            pl.BlockSpec((rows, D_MODEL), lambda j: (0, 0)),
            pl.BlockSpec((D_MODEL, tile), lambda j: (0, j)),
            pl.BlockSpec((1, tile), lambda j: (0, j)),
        ],
        out_specs=pl.BlockSpec((rows, tile), lambda j: (0, j)),
        compiler_params=_params("arbitrary"),
        name="adaln_mod",
    )(cc, w_ada, b_ada)


def _inproj_kernel(x_ref, mod_ref, g_ref, w_ref, qkv_ref, rkv_ref, lora_ref, sga_ref, sgr_ref):
    x = x_ref[...]
    h = _rms(x) * g_ref[...]
    h = h * (1.0 + mod_ref[1:2, :]) + mod_ref[0:1, :]
    hb = h.astype(BF16)
    c0 = ATTN_WIDTH + 2 * KV_WIDTH
    c1 = c0 + 3 * R_WIDTH
    c2 = c1 + LORA_WIDTH
    c3 = c2 + D_MODEL
    qkv_ref[...] = _dot(hb, w_ref[:, 0:c0])
    rkv_ref[...] = _dot(hb, w_ref[:, c0:c1])
    lora_ref[...] = _dot(hb, w_ref[:, c1:c2])
    sga_ref[...] = _sigmoid(_dot(hb, w_ref[:, c2:c3])).astype(BF16)
    sgr_ref[...] = _sigmoid(_dot(hb, w_ref[:, c3:c3 + D_MODEL])).astype(BF16)


def _in_projection(x2d, mod, norm1_g, w_in_bf16, seq_len):
    n_tok = x2d.shape[0]
    tm = PROJ_ROWS
    in_width = w_in_bf16.shape[1]
    widths = (ATTN_WIDTH + 2 * KV_WIDTH, 3 * R_WIDTH, LORA_WIDTH, D_MODEL, D_MODEL)
    dtypes = (F32, F32, F32, BF16, BF16)
    row = lambda i: (i, 0)
    return pl.pallas_call(
        _inproj_kernel,
        out_shape=tuple(jax.ShapeDtypeStruct((n_tok, w), dt) for w, dt in zip(widths, dtypes)),
        grid=(n_tok // tm,),
        in_specs=[
            pl.BlockSpec((tm, D_MODEL), row),
            pl.BlockSpec((None, N_MOD, D_MODEL), lambda i: ((i * tm) // seq_len, 0, 0)),
            pl.BlockSpec((1, D_MODEL), lambda i: (0, 0)),
            pl.BlockSpec((D_MODEL, in_width), lambda i: (0, 0)),
        ],
        out_specs=tuple(pl.BlockSpec((tm, w), row) for w in widths),
        compiler_params=_params("arbitrary"),
        name="in_proj",
    )(x2d, mod, norm1_g, w_in_bf16)


def _attn_kernel(*refs, seq_len, past_len, latent):
    if latent:
        (q_ref, k_ref, v_ref, ck_ref, cv_ref, cos_ref, sin_ref, qg_ref, kg_ref,
         o_ref, qs_ref, kd_ref, vd_ref) = refs
    else:
        (q_ref, k_ref, v_ref, qg_ref, kg_ref, o_ref, kn_ref, vn_ref, qs_ref, kd_ref, vd_ref) = refs
    t = seq_len
    d = HEAD_DIM
    width = 2 * d
    total = t + past_len
    group = pl.program_id(1)
    lane = lax.broadcasted_iota(jnp.int32, (1, width), 1)
    head_of_lane = lane // d
    low_half = (lane % d) < d // 2
    inv_d = 1.0 / d
    same_head = ((lax.broadcasted_iota(jnp.int32, (width, width), 0) // d)
                 == (lax.broadcasted_iota(jnp.int32, (width, width), 1) // d)).astype(BF16)

    def pair_norm(x, gain):
        ms = _mm_exact_lhs_t(x * x, same_head) * inv_d
        y = x * lax.rsqrt(ms + NORM_EPS) * gain
        if latent:
            partner = jnp.where(low_half, pltpu.roll(y, width - d // 2, 1), pltpu.roll(y, d // 2, 1))
            y = y * cos_ref[...] + partner * sin_ref[...]
        return y

    def this_group(x):
        return jnp.where(head_of_lane == group, x, pltpu.roll(x, d, 1))

    kg2 = jnp.concatenate([kg_ref[...], kg_ref[...]], axis=-1)
    qg2 = jnp.concatenate([qg_ref[...], qg_ref[...]], axis=-1)
    v_dup = this_group(v_ref[...])
    if latent:
        kd_ref[0:t, :] = this_group(pair_norm(k_ref[...], kg2)).astype(BF16)
        ck = ck_ref[...].astype(BF16)
        kd_ref[t:total, :] = jnp.concatenate([ck, ck], axis=-1)
        vd_ref[0:t, :] = v_dup.astype(BF16)
        cv = cv_ref[...].astype(BF16)
        vd_ref[t:total, :] = jnp.concatenate([cv, cv], axis=-1)
    else:
        k_dup = this_group(pair_norm(k_ref[...], kg2))
        kn_ref[...] = k_dup[:, 0:d]
        vn_ref[...] = v_dup[:, 0:d]
        kd_ref[...] = k_dup.astype(BF16)
        vd_ref[...] = v_dup.astype(BF16)
    scale = d ** -0.5
    for j in range(Q_PER_KV // 2):
        cols = slice(j * width, (j + 1) * width)
        qs_ref[:, cols] = (pair_norm(q_ref[:, cols], qg2) * scale).astype(BF16)
    n_chunks = total // KEY_CHUNK
    q_rows = min(Q_ROWS, t)

    def body(qb, carry):
        r0 = pl.multiple_of(qb * q_rows, q_rows)
        outs = []
        for j in range(Q_PER_KV // 2):
            qp = qs_ref[pl.ds(r0, q_rows), j * width:(j + 1) * width]
            heads = []
            for h in range(2):
                qm = jnp.where(head_of_lane == h, qp, jnp.zeros_like(qp))
                m = l = acc = None
                for c in range(n_chunks):
                    keys = slice(c * KEY_CHUNK, (c + 1) * KEY_CHUNK)
                    s = _dg(qm, kd_ref[keys, :], _NT)
                    m_c = jnp.max(s, axis=-1, keepdims=True)
                    if c == 0:
                        m = m_c
                        p = jnp.exp(s - m)
                        l = jnp.sum(p, axis=-1, keepdims=True)
                        acc = _dot(p.astype(BF16), vd_ref[keys, :])
                    else:
                        m_new = jnp.maximum(m, m_c)
                        alpha = jnp.exp(m - m_new)
                        p = jnp.exp(s - m_new)
                        l = alpha * l + jnp.sum(p, axis=-1, keepdims=True)
                        acc = alpha * acc + _dot(p.astype(BF16), vd_ref[keys, :])
                        m = m_new
                heads.append(acc / l)
            outs.append(jnp.where(head_of_lane == 0, heads[0], heads[1]))
        o_ref[pl.ds(r0, q_rows), :] = jnp.concatenate(outs, axis=-1).astype(BF16)
        return carry

    lax.fori_loop(0, t // q_rows, body, 0)


def _attention(qkv, qg, kg, cache=None, rope=None):
    b, t, _ = qkv.shape
    d = HEAD_DIM
    latent = cache is not None
    past_len = cache[0].shape[2] if latent else 0
    total = t + past_len
    assert total % KEY_CHUNK == 0 and t % min(Q_ROWS, t) == 0 and KV_WIDTH == 2 * d
    gw = Q_PER_KV * d
    q_spec = pl.BlockSpec((None, t, gw), lambda i, g: (i, 0, g))
    k_spec = pl.BlockSpec((None, t, KV_WIDTH), lambda i, g: (i, 0, ATTN_WIDTH // KV_WIDTH))
    v_spec = pl.BlockSpec((None, t, KV_WIDTH), lambda i, g: (i, 0, ATTN_WIDTH // KV_WIDTH + 1))
    hm_spec = pl.BlockSpec((None, None, t, d), lambda i, g: (i, g, 0, 0))
    vec_spec = pl.BlockSpec((1, d), lambda i, g: (0, 0))
    out_shape = jax.ShapeDtypeStruct((b, t, ATTN_WIDTH), BF16)
    scratch = [pltpu.VMEM((t, gw), BF16), pltpu.VMEM((total, 2 * d), BF16),
               pltpu.VMEM((total, 2 * d), BF16)]
    kern = functools.partial(_attn_kernel, seq_len=t, past_len=past_len, latent=latent)
    if latent:
        ck, cv = cache
        cos4, sin4 = rope
        c_spec = pl.BlockSpec((None, None, past_len, d), lambda i, g: (i, g, 0, 0))
        tab_spec = pl.BlockSpec((t, 2 * d), lambda i, g: (0, 0))
        return pl.pallas_call(
            kern,
            out_shape=out_shape,
            grid=(b, KV_HEADS),
            in_specs=[q_spec, k_spec, v_spec, c_spec, c_spec, tab_spec, tab_spec,
                      vec_spec, vec_spec],
            out_specs=q_spec,
            scratch_shapes=scratch,
            compiler_params=_params("arbitrary", "arbitrary"),
            name="attn_latent",
        )(qkv, qkv, qkv, ck, cv, cos4, sin4, qg, kg)
    hm_shape = jax.ShapeDtypeStruct((b, KV_HEADS, t, d), F32)
    return pl.pallas_call(
        kern,
        out_shape=(out_shape, hm_shape, hm_shape),
        grid=(b, KV_HEADS),
        in_specs=[q_spec, k_spec, v_spec, vec_spec, vec_spec],
        out_specs=(q_spec, hm_spec, hm_spec),
        scratch_shapes=scratch,
        compiler_params=_params("arbitrary", "arbitrary"),
        name="attn_context",
    )(qkv, qkv, qkv, qg, kg)


def _conv_rows(ref_rows, taps, start, n, total):
    cur = ref_rows(start, n)
    width = cur.shape[-1]
    zero = jnp.zeros((1, width), F32)
    prev_row = ref_rows(start - 1, 1) if start > 0 else zero
    next_row = ref_rows(start + n, 1) if start + n < total else zero
    ridx = lax.broadcasted_iota(jnp.int32, (n, 1), 0)
    prev = jnp.where(ridx == 0, prev_row, pltpu.roll(cur, 1, 0))
    nxt = jnp.where(ridx == n - 1, next_row, pltpu.roll(cur, n - 1, 0))
    return prev * taps[0] + cur * taps[1] + nxt * taps[2]


def _round_robin(chains):
    results = [None] * len(chains)
    live = list(range(len(chains)))
    while live:
        for i in list(live):
            try:
                next(chains[i])
            except StopIteration as done:
                results[i] = done.value
                live.remove(i)
    return results


def _rwkv_kernel(*refs, seq_len, has_state):
    n_in = 19 if has_state else 18
    (r_ref, k_ref, v_ref, lora_ref, cr_ref, ck_ref, cv_ref, cl_ref, w0_ref, w2_ref, a0_ref, a2_ref,
     g2_ref, kk_ref, ka_ref, rk_ref, lng_ref, lnb_ref) = refs[:18]
    s0_ref = refs[18] if has_state else None
    y_ref, sfin_ref, rs_ref, ks_ref, vs_ref, ls_ref, yacc_ref, bacc_ref, st_ref = refs[n_in:]
    t = seq_len
    step_rows = min(STEP_ROWS, t)
    n_steps = t // step_rows
    n_sub = step_rows // GROUP
    n_chunks = GROUP // CHUNK
    n = R_HEAD_DIM
    width = 2 * n
    step = pl.program_id(2)
    pair_lanes = [slice(q * width, (q + 1) * width) for q in range(PAIRS_PER_STEP)]

    @pl.when(step == 0)
    def _():
        for blk in range(n_steps):
            start = blk * step_rows
            rows = slice(start, start + step_rows)
            for src, cw, dst in ((r_ref, cr_ref, rs_ref), (k_ref, ck_ref, ks_ref),
                                 (v_ref, cv_ref, vs_ref), (lora_ref, cl_ref, ls_ref)):
                taps = [cw[j:j + 1, :] for j in range(3)]
                dst[rows, :] = _conv_rows(lambda s, m, src=src: src[s:s + m, :],
                                          taps, start, step_rows, t)
        zero = jnp.zeros((n, n), F32)
        for d in range(2):
            for q in range(PAIRS_PER_STEP):
                if has_state:
                    s_a, s_b = s0_ref[d, 2 * q], s0_ref[d, 2 * q + 1]
                    bd = jnp.concatenate([jnp.concatenate([s_a, zero], axis=1),
                                          jnp.concatenate([zero, s_b], axis=1)], axis=0)
                    st_ref[d, q] = bd.T
                else:
                    st_ref[d, q] = jnp.zeros((width, width), F32)

    lane = lax.broadcasted_iota(jnp.int32, (1, width), 1)
    head0 = lane < R_HEAD_DIM

    def per_head(x):
        x0 = jnp.where(head0, x, 0.0)
        return (x0, x - x0)

    def pick(x0, x1):
        return jnp.where(head0, x0, x1)

    def pair_mm(mats, x):
        return pick(*[_mm(m, x) for m in mats])

    def pair_mm2(mats, x, y):
        xy = jnp.concatenate([x, y], axis=1)
        prods = [_mm(m, xy) for m in mats]
        return (pick(*[p[:, :width] for p in prods]), pick(*[p[:, width:] for p in prods]))

    def head_sum(x):
        s0 = jnp.sum(jnp.where(head0, x, 0.0), axis=-1, keepdims=True)
        s1 = jnp.sum(jnp.where(head0, 0.0, x), axis=-1, keepdims=True)
        return jnp.where(head0, s0, s1)

    def scan_rows(d, q, blk, state):
        rev = d == 1
        lanes = pair_lanes[q]
        base = blk * step_rows
        ri = lax.broadcasted_iota(jnp.int32, (GROUP, GROUP), 0)
        ci = lax.broadcasted_iota(jnp.int32, (GROUP, GROUP), 1)
        same = (ri // CHUNK) == (ci // CHUNK)
        eye = (ri == ci).astype(F32)
        strict = same & ((ci > ri) if rev else (ci < ri))
        incl = same & ((ci >= ri) if rev else (ci <= ri))
        tri = incl.astype(BF16)

        def half_mask(m):
            return ((ri // (2 * m)) == (ci // (2 * m))) & ((ri // m) != (ci // m))

        subs = []
        for s in (range(n_sub - 1, -1, -1) if rev else range(n_sub)):
            rows = pl.ds(pl.multiple_of(base + s * GROUP, GROUP), GROUP)
            lo = ls_ref[rows, :]
            wd_t = jnp.tanh(lo[:, 0:DECAY_RANK])
            z = w0_ref[d:d + 1, lanes] + _mm(wd_t, w2_ref[d, :, lanes])
            subs.append({"rows": rows, "r": rs_ref[rows, lanes], "k": ks_ref[rows, lanes],
                         "v": vs_ref[rows, lanes], "ad": lo[:, DECAY_RANK:DECAY_RANK + ICLR_RANK],
                         "lw": -DECAY_SCALE * _sigmoid(z)})
        yield
        for p in subs:
            p["cum"] = _mm_exact_lhs(tri, p["lw"])
            p["cum_c"] = jnp.concatenate(
                [jnp.broadcast_to(p["cum"][r:r + 1, :], (CHUNK, width))
                 for r in (range(0, GROUP, CHUNK) if rev else range(CHUNK - 1, GROUP, CHUNK))], axis=0)
            a = _sigmoid(a0_ref[d:d + 1, lanes] + _mm(p["ad"], a2_ref[d, :, lanes]))
            p["kd"] = p["k"] * (1.0 + (a - 1.0) * ka_ref[:, lanes])
            kk = p["k"] * kk_ref[:, lanes]
            p["kk"] = kk * lax.rsqrt(head_sum(kk * kk) + 1e-12)
            p["b"] = p["kk"] * a
            bacc_ref[d, p["rows"], lanes] = head_sum(p["r"] * p["kd"] * rk_ref[:, lanes]) * p["v"]
        yield
        for p in subs:
            cum, cum_c = p["cum"], p["cum_c"]
            p["a_t"] = -p["kk"] * jnp.exp(cum - p["lw"])
            p["r_t"] = p["r"] * jnp.exp(cum)
            inv = jnp.exp(-cum)
            p["b_t"] = p["b"] * inv
            p["k_t"] = p["kd"] * inv
            tail = jnp.exp(cum_c - cum)
            p["b_h"] = p["b"] * tail
            p["k_h"] = p["kd"] * tail
            p["p_c"] = jnp.exp(cum_c)
        yield
        for p in subs:
            bk = jnp.concatenate([p["b_t"], p["k_t"]], axis=0)
            xa = [_mm(x, bk, _NT) for x in per_head(p["a_t"])]
            xr = [_mm(x, bk, _NT) for x in per_head(p["r_t"])]
            p["a_ab"] = [jnp.where(strict, m[:, :GROUP], 0.0) for m in xa]
            p["a_ak"] = [jnp.where(strict, m[:, GROUP:], 0.0) for m in xa]
            p["a_rb"] = [jnp.where(incl, m[:, :GROUP], 0.0) for m in xr]
            p["a_rk"] = [jnp.where(incl, m[:, GROUP:], 0.0) for m in xr]
        yield
        for p in subs:
            p["akv"] = pair_mm(p["a_ak"], p["v"])
            p["rkv"] = pair_mm(p["a_rk"], p["v"])
            p["tm"] = [eye + jnp.where(half_mask(1), m, 0.0) for m in p["a_ab"]]
        m_size = 2
        while m_size < CHUNK:
            yield
            off = half_mask(m_size)
            for p in subs:
                p["ot"] = [_mm(jnp.where(off, n_mat, 0.0), tm)
                           for n_mat, tm in zip(p["a_ab"], p["tm"])]
            yield
            for p in subs:
                p["tm"] = [tm + _mm(tm, ot) for tm, ot in zip(p["tm"], p["ot"])]
            m_size *= 2
        yield
        for p in subs:
            p["ta"], p["tk"] = pair_mm2(p["tm"], p["a_t"], p["akv"])
        yield
        cr = [slice(c * CHUNK, (c + 1) * CHUNK) for c in range(n_chunks)]
        for p in subs:
            ra, rk = pair_mm2(p["a_rb"], p["ta"], p["tk"])
            p["r2"] = p["r_t"] + ra
            p["y2"] = p["rkv"] + rk
            p["m_t"] = [eye * p["p_c"][c * CHUNK:c * CHUNK + 1, :]
                        + jnp.where(same, _mm(p["b_h"][cr[c]], p["ta"][cr[c]], _TN), 0.0)
                        for c in range(n_chunks)]
            p["g_t"] = [jnp.where(same, _mm(jnp.concatenate([p["b_h"][cr[c]], p["k_h"][cr[c]]], axis=0),
                                            jnp.concatenate([p["tk"][cr[c]], p["v"][cr[c]]], axis=0),
                                            _TN), 0.0)
                        for c in range(n_chunks)]
        for p in subs:
            ys = [None] * n_chunks
            for c in (range(n_chunks - 1, -1, -1) if rev else range(n_chunks)):
                yield
                ys[c] = _mm(p["r2"][cr[c]], state) + p["y2"][cr[c]]
                state = _mm(p["m_t"][c], state) + p["g_t"][c]
            yacc_ref[d, p["rows"], lanes] = jnp.concatenate(ys, axis=0)
        return state

    chains = [(d, q) for q in range(PAIRS_PER_STEP) for d in range(2)]
    blocks = (step, n_steps - 1 - step)
    new_state = _round_robin([scan_rows(d, q, blocks[d], st_ref[d, q]) for d, q in chains])
    for (d, q), s_new in zip(chains, new_state):
        st_ref[d, q] = s_new

    @pl.when(step == n_steps - 1)
    def _():
        for d, q in chains:
            s_vk = st_ref[d, q].T
            sfin_ref[d, 2 * q] = s_vk[0:n, 0:n]
            sfin_ref[d, 2 * q + 1] = s_vk[n:width, n:width]
        inv_n = 1.0 / n
        for blk in range(n_steps):
            rows = slice(blk * step_rows, (blk + 1) * step_rows)
            gd = _sigmoid(ls_ref[rows, DECAY_RANK + ICLR_RANK:LORA_WIDTH]).astype(BF16)
            for lanes in pair_lanes:
                y = yacc_ref[0, rows, lanes] + yacc_ref[1, rows, lanes]
                yc = y - head_sum(y) * inv_n
                yn = yc * lax.rsqrt(head_sum(yc * yc) * inv_n + GN_EPS)
                out = (yn * lng_ref[:, lanes] + lnb_ref[:, lanes]
                       + (bacc_ref[0, rows, lanes] + bacc_ref[1, rows, lanes]))
                y_ref[rows, lanes] = (out * _dot(gd, g2_ref[:, lanes].astype(BF16))).astype(BF16)


def _rwkv(rkv, lora, p, s0):
    n_seq, t, _ = rkv.shape
    has_state = s0 is not None
    n = R_HEAD_DIM
    heads = 2 * PAIRS_PER_STEP
    width = heads * n
    blocks = R_WIDTH // width
    col = lambda off: (lambda s, i, j: (s, 0, off + i))
    vec = lambda off: (lambda s, i, j: (0, off + i))
    dir_mat = lambda s, i, j: (0, 0, i)
    in_specs = [
        pl.BlockSpec((None, t, width), col(0)),
        pl.BlockSpec((None, t, width), col(blocks)),
        pl.BlockSpec((None, t, width), col(2 * blocks)),
        pl.BlockSpec((None, t, LORA_WIDTH), lambda s, i, j: (s, 0, 0)),
        pl.BlockSpec((3, width), vec(0)),
        pl.BlockSpec((3, width), vec(blocks)),
        pl.BlockSpec((3, width), vec(2 * blocks)),
        pl.BlockSpec((3, LORA_WIDTH), lambda s, i, j: (0, 3 * R_WIDTH // LORA_WIDTH)),
        pl.BlockSpec((2, width), vec(0)),
        pl.BlockSpec((2, DECAY_RANK, width), dir_mat),
        pl.BlockSpec((2, width), vec(0)),
        pl.BlockSpec((2, ICLR_RANK, width), dir_mat),
        pl.BlockSpec((GATE_RANK, width), vec(0)),
        pl.BlockSpec((1, width), vec(0)),
        pl.BlockSpec((1, width), vec(0)),
        pl.BlockSpec((1, width), vec(0)),
        pl.BlockSpec((1, width), vec(0)),
        pl.BlockSpec((1, width), vec(0)),
    ]
    conv = p["conv"]
    args = [rkv, rkv, rkv, lora, conv, conv, conv, conv, p["w0"], p["w2"], p["a0"], p["a2"],
            p["g2"], p["k_k"], p["k_a"], p["r_k"], p["ln_g"], p["ln_b"]]
    state_spec = pl.BlockSpec((None, 2, heads, n, n), lambda s, i, j: (s, 0, i, 0, 0))
    if has_state:
        in_specs.append(state_spec)
        args.append(s0)
    seq_buf = pltpu.VMEM((t, width), F32)
    dir_buf = pltpu.VMEM((2, t, width), F32)
    return pl.pallas_call(
        functools.partial(_rwkv_kernel, seq_len=t, has_state=has_state),
        out_shape=(jax.ShapeDtypeStruct((n_seq, t, R_WIDTH), BF16),
                   jax.ShapeDtypeStruct((n_seq, 2, R_HEADS, n, n), F32)),
        grid=(n_seq, blocks, t // min(STEP_ROWS, t)),
        in_specs=in_specs,
        out_specs=(pl.BlockSpec((None, t, width), col(0)), state_spec),
        scratch_shapes=[seq_buf, seq_buf, seq_buf, pltpu.VMEM((t, LORA_WIDTH), F32),
                        dir_buf, dir_buf, pltpu.VMEM((2, PAIRS_PER_STEP, 2 * n, 2 * n), F32)],
        compiler_params=_params("arbitrary", "arbitrary", "arbitrary"),
        name="rwkv7_latent" if has_state else "rwkv7_context",
    )(*args)


def _merge_kernel(x_ref, ao_ref, ro_ref, sga_ref, sgr_ref, mod_ref, g_ref, wab_ref, wrb_ref,
                  wo_ref, x1_ref, h2_ref):
    merged = (sga_ref[...] * _dot(ao_ref[...], wab_ref[...])
              + sgr_ref[...] * _dot(ro_ref[...], wrb_ref[...]))
    x1 = x_ref[...] + mod_ref[2:3, :] * _dot(merged.astype(BF16), wo_ref[...])
    x1_ref[...] = x1
    h2 = _rms(x1) * g_ref[...]
    h2_ref[...] = (h2 * (1.0 + mod_ref[4:5, :]) + mod_ref[3:4, :]).astype(BF16)


def _merge(x2d, attn_o, rwkv_o, sga, sgr, mod, norm2_g, wab, wrb, wo, seq_len):
    n_tok = x2d.shape[0]
    tm = PROJ_ROWS
    row = lambda i: (i, 0)
    const = lambda i: (0, 0)
    return pl.pallas_call(
        _merge_kernel,
        out_shape=(jax.ShapeDtypeStruct((n_tok, D_MODEL), F32),
                   jax.ShapeDtypeStruct((n_tok, D_MODEL), BF16)),
        grid=(n_tok // tm,),
        in_specs=[
            pl.BlockSpec((tm, D_MODEL), row),
            pl.BlockSpec((tm, ATTN_WIDTH), row),
            pl.BlockSpec((tm, R_WIDTH), row),
            pl.BlockSpec((tm, D_MODEL), row),
            pl.BlockSpec((tm, D_MODEL), row),
            pl.BlockSpec((None, N_MOD, D_MODEL), lambda i: ((i * tm) // seq_len, 0, 0)),
            pl.BlockSpec((1, D_MODEL), const),
            pl.BlockSpec((ATTN_WIDTH, D_MODEL), const),
            pl.BlockSpec((R_WIDTH, D_MODEL), const),
            pl.BlockSpec((D_MODEL, D_MODEL), const),
        ],
        out_specs=(pl.BlockSpec((tm, D_MODEL), row), pl.BlockSpec((tm, D_MODEL), row)),
        compiler_params=_params("arbitrary"),
        name="merge_out_proj",
    )(x2d, attn_o, rwkv_o, sga, sgr, mod, norm2_g, wab, wrb, wo)


def _ffn_kernel(h2_ref, x1_ref, mod_ref, wu_ref, wg_ref, cu_ref, cg_ref, wd_ref, fg_ref,
                y_ref, acc_ref, upu_ref, upg_ref, *, seq_len):
    i = pl.program_id(0)
    j = pl.program_id(1)
    rows = h2_ref.shape[0]
    pad = FFN_PAD_ROWS
    rb = FFN_CONV_ROWS
    tile = 8

    @pl.when((i == 0) & (j == 0))
    def _():
        for ref in (upu_ref, upg_ref):
            ref[0:pad, :] = jnp.zeros((pad, ref.shape[1]), F32)
            ref[pad + rows:pad + rows + pad, :] = jnp.zeros((pad, ref.shape[1]), F32)

    @pl.when(j == 0)
    def _():
        acc_ref[...] = jnp.zeros(acc_ref.shape, F32)

    def up_project(r0):
        h2 = h2_ref[r0:r0 + rb, :]
        upu_ref[pad + r0:pad + r0 + rb, :] = _dot(h2, wu_ref[...])
        upg_ref[pad + r0:pad + r0 + rb, :] = _dot(h2, wg_ref[...])

    row0 = lax.broadcasted_iota(jnp.int32, (tile, 1), 0) == 0
    row7 = lax.broadcasted_iota(jnp.int32, (tile, 1), 0) == tile - 1

    taps_u = [cu_ref[k:k + 1, :] for k in range(3)]
    taps_g = [cg_ref[k:k + 1, :] for k in range(3)]

    def zero_rows(x, tile_offsets, row_mask):
        parts, done = [], 0
        for off in tile_offsets:
            parts += [x[done:off], jnp.where(row_mask, 0.0, x[off:off + tile])]
            done = off + tile
        parts.append(x[done:])
        return jnp.concatenate([p for p in parts if p.shape[0]], axis=0)

    def conv(up_ref, taps, r0):
        base = pad + r0
        cur = up_ref[base:base + rb, :]
        prev = up_ref[base - 1:base - 1 + rb, :]
        nxt = up_ref[base + 1:base + 1 + rb, :]
        first_seq = -(-r0 // seq_len) * seq_len
        starts = [s - r0 for s in range(first_seq, r0 + rb, seq_len)]
        ends = [e - r0 for e in range(first_seq if first_seq > r0 else first_seq + seq_len,
                                      r0 + rb + 1, seq_len)]
        prev = zero_rows(prev, starts, row0)
        nxt = zero_rows(nxt, [e - tile for e in ends], row7)
        return prev * taps[0] + cur * taps[1] + nxt * taps[2]

    starts = list(range(0, rows, rb))
    for r0 in starts[:2]:
        up_project(r0)
    for b, r0 in enumerate(starts):
        up_u = conv(upu_ref, taps_u, r0)
        up_g = conv(upg_ref, taps_g, r0)
        act = (up_g * _sigmoid(up_g) * up_u).astype(BF16)
        if b + 2 < len(starts):
            up_project(starts[b + 2])
        acc_ref[r0:r0 + rb, :] += _dot(act, wd_ref[...])

    @pl.when(j == pl.num_programs(1) - 1)
    def _():
        x2 = x1_ref[...] + mod_ref[...] * acc_ref[...]
        y_ref[...] = _rms(x2) * fg_ref[...]


def _ffn(h2, x1, gate2, ffn_up, ffn_conv, ffn_down, final_g, seq_len):
    n_tok = h2.shape[0]
    tm = FFN_ROWS
    tn = FFN_TILE
    nt = D_FF // tn
    assert tm % seq_len == 0 or seq_len % tm == 0
    gate_map = lambda i, j: ((i * tm) // seq_len, 0, 0)
    return pl.pallas_call(
        functools.partial(_ffn_kernel, seq_len=seq_len),
        out_shape=jax.ShapeDtypeStruct((n_tok, D_MODEL), F32),
        grid=(n_tok // tm, nt),
        in_specs=[
            pl.BlockSpec((tm, D_MODEL), lambda i, j: (i, 0)),
            pl.BlockSpec((tm, D_MODEL), lambda i, j: (i, 0)),
            pl.BlockSpec((None, 1, D_MODEL), gate_map),
            pl.BlockSpec((D_MODEL, tn), lambda i, j: (0, j)),
            pl.BlockSpec((D_MODEL, tn), lambda i, j: (0, nt + j)),
            pl.BlockSpec((3, tn), lambda i, j: (0, j)),
            pl.BlockSpec((3, tn), lambda i, j: (0, nt + j)),
            pl.BlockSpec((tn, D_MODEL), lambda i, j: (j, 0)),
            pl.BlockSpec((1, D_MODEL), lambda i, j: (0, 0)),
        ],
        out_specs=pl.BlockSpec((tm, D_MODEL), lambda i, j: (i, 0)),
        scratch_shapes=[pltpu.VMEM((tm, D_MODEL), F32),
                        pltpu.VMEM((tm + 2 * FFN_PAD_ROWS, tn), F32),
                        pltpu.VMEM((tm + 2 * FFN_PAD_ROWS, tn), F32)],
        compiler_params=_params("arbitrary", "arbitrary"),
        name="conv_ffn",
    )(h2, x1, gate2, ffn_up, ffn_up, ffn_conv, ffn_conv, ffn_down, final_g)


def _layer_path(x, mod, w, rope, cache, s0):
    b, t, _ = x.shape
    x2d = x.reshape(b * t, D_MODEL)
    qkv, rkv, lora, sga, sgr = _in_projection(x2d, mod, w["norm1_g"], w["w_in"], t)
    qkv = qkv.reshape(b, t, ATTN_WIDTH + 2 * KV_WIDTH)
    if rope is None:
        attn_o, k_norm, v_hm = _attention(qkv, w["q_norm_g"], w["k_norm_g"])
    else:
        attn_o = _attention(qkv, w["q_norm_g"], w["k_norm_g"], cache=cache, rope=rope)
        k_norm = v_hm = None
    rwkv_o, s_fin = _rwkv(rkv.reshape(b, t, 3 * R_WIDTH), lora.reshape(b, t, LORA_WIDTH),
                          w["rwkv"], s0)
    x1, h2 = _merge(x2d, attn_o.reshape(b * t, ATTN_WIDTH), rwkv_o.reshape(b * t, R_WIDTH),
                    sga, sgr, mod, w["norm2_g"], w["w_attn_br"], w["w_rwkv_br"], w["w_out"], t)
    gate2 = mod[:, 5:6, :]
    y = _ffn(h2, x1, gate2, w["ffn_up"], w["ffn_conv"], w["ffn_down"], w["final_norm_g"], t)
    return y.reshape(b, t, D_MODEL), k_norm, v_hm, s_fin


def _rope_tables(rows):
    row = jnp.repeat(jnp.arange(rows), GRID_W).astype(F32)
    col = jnp.tile(jnp.arange(GRID_W), rows).astype(F32)
    inv = 1.0 / (ROPE_THETA ** (jnp.arange(ROPE_PAIRS, dtype=F32) / ROPE_PAIRS))
    ang = jnp.concatenate([row[:, None] * inv, col[:, None] * inv], axis=-1)
    cos, sin = jnp.cos(ang), jnp.sin(ang)
    return jnp.concatenate([cos, cos] * 2, axis=-1), jnp.concatenate([-sin, sin] * 2, axis=-1)


def kernel(x_prompt, x_sample, cache_k, cache_v, state_rwkv, c, c_ctx, w_ada, b_ada, norm1_g, w_in, q_norm_g, k_norm_g, rwkv_conv, rwkv_w0, rwkv_w2, rwkv_a0, rwkv_a2, rwkv_g2, rwkv_k_k, rwkv_k_a, rwkv_r_k, rwkv_ln_g, rwkv_ln_b, w_attn_br, w_rwkv_br, w_out, norm2_g, ffn_up, ffn_conv, ffn_down, final_norm_g):
    depth = w_in.shape[0]
    assert depth == 1, "single trunk layer"
    l = 0
    n_ctx = x_prompt.shape[0]
    n_lat = x_sample.shape[0]

    cc = jnp.concatenate([c_ctx[None, :], c, jnp.zeros((8 - 1 - n_lat, D_MODEL), F32)], axis=0)
    mod_all = _modulation(cc, w_ada[l], b_ada[l][None, :])
    mod_ctx = jnp.broadcast_to(mod_all[0:1].reshape(1, N_MOD, D_MODEL), (n_ctx, N_MOD, D_MODEL))
    mod_lat = mod_all[1:1 + n_lat].reshape(n_lat, N_MOD, D_MODEL)

    rwkv_p = {
        "conv": rwkv_conv[l], "w0": rwkv_w0[l], "w2": rwkv_w2[l], "a0": rwkv_a0[l],
        "a2": rwkv_a2[l], "g2": rwkv_g2[l], "k_k": rwkv_k_k[l][None, :],
        "k_a": rwkv_k_a[l][None, :], "r_k": rwkv_r_k[l].reshape(1, R_WIDTH),
        "ln_g": rwkv_ln_g[l][None, :], "ln_b": rwkv_ln_b[l][None, :],
    }
    w = {
        "norm1_g": norm1_g[l][None, :], "w_in": w_in[l].astype(BF16),
        "q_norm_g": q_norm_g[l][None, :], "k_norm_g": k_norm_g[l][None, :],
        "rwkv": rwkv_p,
        "w_attn_br": w_attn_br[l].astype(BF16), "w_rwkv_br": w_rwkv_br[l].astype(BF16),
        "w_out": w_out[l].astype(BF16), "norm2_g": norm2_g[l][None, :],
        "ffn_up": ffn_up[l].astype(BF16), "ffn_conv": ffn_conv[l],
        "ffn_down": ffn_down[l].astype(BF16), "final_norm_g": final_norm_g[None, :],
    }

    y_prompt, k_ctx, v_ctx, s_ctx = _layer_path(x_prompt, mod_ctx, w, None, None, None)
    rope = _rope_tables(x_sample.shape[1] // GRID_W)
    y_sample, _, _, _ = _layer_path(x_sample, mod_lat, w, rope, (cache_k[:, l], cache_v[:, l]),
                                    state_rwkv[:, l])
    return (y_prompt, y_sample, k_ctx[:, None], v_ctx[:, None], s_ctx[:, None])
```

```python
import functools

import jax
import jax.numpy as jnp
import numpy as np
from jax import lax
from jax.experimental import pallas as pl
from jax.experimental.pallas import tpu as pltpu

D_MODEL = 1024
GRID_W = 64
N_HEADS = 8
KV_HEADS = 2
Q_PER_KV = N_HEADS // KV_HEADS
HEAD_DIM = 64
ATTN_WIDTH = N_HEADS * HEAD_DIM
KV_WIDTH = KV_HEADS * HEAD_DIM
ROPE_THETA = 10000.0
ROPE_PAIRS = HEAD_DIM // 4
R_HEADS = 8
R_HEAD_DIM = 64
R_WIDTH = R_HEADS * R_HEAD_DIM
DECAY_RANK = 64
ICLR_RANK = 64
GATE_RANK = 128
LORA_WIDTH = DECAY_RANK + ICLR_RANK + GATE_RANK
R_IN_WIDTH = 3 * R_WIDTH + LORA_WIDTH
D_FF = 2816
N_MOD = 6
NORM_EPS = 1e-6
GN_EPS = 64e-5
DECAY_SCALE = float(np.exp(-0.5))

F32 = jnp.float32
BF16 = jnp.bfloat16

VMEM_LIMIT_BYTES = 56 * 1024 * 1024
PROJ_ROWS = 512
FFN_ROWS = 1024
FFN_TILE = 256
FFN_PAD_ROWS = 8
FFN_CONV_ROWS = 256
Q_ROWS = 512
KEY_CHUNK = 256
STEP_ROWS = 256
GROUP = 128
PAIRS_PER_STEP = 2
CHUNK = 64

_NN = (((1,), (0,)), ((), ()))
_NT = (((1,), (1,)), ((), ()))
_TN = (((0,), (0,)), ((), ()))


def _dg(a, b, dims):
    return lax.dot_general(a, b, dims, preferred_element_type=F32)


def _dot(a, b):
    return _dg(a, b, _NN)


def _mm(a, b, dims=_NN):
    return _dg(a.astype(BF16), b.astype(BF16), dims)


def _mm_exact_lhs(m, x):
    hi = x.astype(BF16)
    r1 = x - hi.astype(F32)
    mid = r1.astype(BF16)
    lo = (r1 - mid.astype(F32)).astype(BF16)
    return _dot(m, hi) + (_dot(m, mid) + _dot(m, lo))


def _mm_exact_lhs_t(x, m):
    hi = x.astype(BF16)
    r1 = x - hi.astype(F32)
    mid = r1.astype(BF16)
    lo = (r1 - mid.astype(F32)).astype(BF16)
    return _dot(hi, m) + (_dot(mid, m) + _dot(lo, m))


def _sigmoid(x):
    return 1.0 / (1.0 + jnp.exp(-x))


def _rms(x):
    return x * lax.rsqrt(jnp.mean(x * x, axis=-1, keepdims=True) + NORM_EPS)


def _params(*semantics):
    return pltpu.CompilerParams(dimension_semantics=semantics,
                                vmem_limit_bytes=VMEM_LIMIT_BYTES)


def _mod_kernel(c_ref, w_ref, b_ref, o_ref):
    c = c_ref[...]
    s = (c * _sigmoid(c)).astype(BF16)
    o_ref[...] = _dot(s, w_ref[...].astype(BF16)) + b_ref[...]


def _modulation(cc, w_ada, b_ada):
    rows = cc.shape[0]
    n = w_ada.shape[1]
    tile = 1024
    return pl.pallas_call(
        _mod_kernel,
        out_shape=jax.ShapeDtypeStruct((rows, n), F32),
        grid=(n // tile,),
        in_specs=[
            pl.BlockSpec((rows, D_MODEL), lambda j: (0, 0)),
            pl.BlockSpec((D_MODEL, tile), lambda j: (0, j)),
            pl.BlockSpec((1, tile), lambda j: (0, j)),
        ],
        out_specs=pl.BlockSpec((rows, tile), lambda j: (0, j)),
        compiler_params=_params("arbitrary"),
        name="adaln_mod",
    )(cc, w_ada, b_ada)


def _inproj_kernel(x_ref, mod_ref, g_ref, w_ref, qkv_ref, rkv_ref, lora_ref, sga_ref, sgr_ref):
    x = x_ref[...]
    h = _rms(x) * g_ref[...]
    h = h * (1.0 + mod_ref[1:2, :]) + mod_ref[0:1, :]
    hb = h.astype(BF16)
    c0 = ATTN_WIDTH + 2 * KV_WIDTH
    c1 = c0 + 3 * R_WIDTH
    c2 = c1 + LORA_WIDTH
    c3 = c2 + D_MODEL
    qkv_ref[...] = _dot(hb, w_ref[:, 0:c0])
    rkv_ref[...] = _dot(hb, w_ref[:, c0:c1])
    lora_ref[...] = _dot(hb, w_ref[:, c1:c2])
    sga_ref[...] = _sigmoid(_dot(hb, w_ref[:, c2:c3])).astype(BF16)
    sgr_ref[...] = _sigmoid(_dot(hb, w_ref[:, c3:c3 + D_MODEL])).astype(BF16)


def _in_projection(x2d, mod, norm1_g, w_in_bf16, seq_len):
    n_tok = x2d.shape[0]
    tm = PROJ_ROWS
    in_width = w_in_bf16.shape[1]
    widths = (ATTN_WIDTH + 2 * KV_WIDTH, 3 * R_WIDTH, LORA_WIDTH, D_MODEL, D_MODEL)
    dtypes = (F32, F32, F32, BF16, BF16)
    row = lambda i: (i, 0)
    return pl.pallas_call(
        _inproj_kernel,
        out_shape=tuple(jax.ShapeDtypeStruct((n_tok, w), dt) for w, dt in zip(widths, dtypes)),
        grid=(n_tok // tm,),
        in_specs=[
            pl.BlockSpec((tm, D_MODEL), row),
            pl.BlockSpec((None, N_MOD, D_MODEL), lambda i: ((i * tm) // seq_len, 0, 0)),
            pl.BlockSpec((1, D_MODEL), lambda i: (0, 0)),
            pl.BlockSpec((D_MODEL, in_width), lambda i: (0, 0)),
        ],
        out_specs=tuple(pl.BlockSpec((tm, w), row) for w in widths),
        compiler_params=_params("arbitrary"),
        name="in_proj",
    )(x2d, mod, norm1_g, w_in_bf16)


def _attn_kernel(*refs, seq_len, past_len, latent):
    if latent:
        (q_ref, k_ref, v_ref, ck_ref, cv_ref, cos_ref, sin_ref, qg_ref, kg_ref,
         o_ref, qs_ref, kd_ref, vd_ref) = refs
    else:
        (q_ref, k_ref, v_ref, qg_ref, kg_ref, o_ref, kn_ref, vn_ref, qs_ref, kd_ref, vd_ref) = refs
    t = seq_len
    d = HEAD_DIM
    width = 2 * d
    total = t + past_len
    group = pl.program_id(1)
    lane = lax.broadcasted_iota(jnp.int32, (1, width), 1)
    head_of_lane = lane // d
    low_half = (lane % d) < d // 2
    inv_d = 1.0 / d
    same_head = ((lax.broadcasted_iota(jnp.int32, (width, width), 0) // d)
                 == (lax.broadcasted_iota(jnp.int32, (width, width), 1) // d)).astype(BF16)

    def pair_norm(x, gain):
        ms = _mm_exact_lhs_t(x * x, same_head) * inv_d
        y = x * lax.rsqrt(ms + NORM_EPS) * gain
        if latent:
            partner = jnp.where(low_half, pltpu.roll(y, width - d // 2, 1), pltpu.roll(y, d // 2, 1))
            y = y * cos_ref[...] + partner * sin_ref[...]
        return y

    def this_group(x):
        return jnp.where(head_of_lane == group, x, pltpu.roll(x, d, 1))

    kg2 = jnp.concatenate([kg_ref[...], kg_ref[...]], axis=-1)
    qg2 = jnp.concatenate([qg_ref[...], qg_ref[...]], axis=-1)
    v_dup = this_group(v_ref[...])
    if latent:
        kd_ref[0:t, :] = this_group(pair_norm(k_ref[...], kg2)).astype(BF16)
        ck = ck_ref[...].astype(BF16)
        kd_ref[t:total, :] = jnp.concatenate([ck, ck], axis=-1)
        vd_ref[0:t, :] = v_dup.astype(BF16)
        cv = cv_ref[...].astype(BF16)
        vd_ref[t:total, :] = jnp.concatenate([cv, cv], axis=-1)
    else:
        k_dup = this_group(pair_norm(k_ref[...], kg2))
        kn_ref[...] = k_dup[:, 0:d]
        vn_ref[...] = v_dup[:, 0:d]
        kd_ref[...] = k_dup.astype(BF16)
        vd_ref[...] = v_dup.astype(BF16)
    scale = d ** -0.5
    for j in range(Q_PER_KV // 2):
        cols = slice(j * width, (j + 1) * width)
        qs_ref[:, cols] = (pair_norm(q_ref[:, cols], qg2) * scale).astype(BF16)
    n_chunks = total // KEY_CHUNK
    q_rows = min(Q_ROWS, t)

    def body(qb, carry):
        r0 = pl.multiple_of(qb * q_rows, q_rows)
        outs = []
        for j in range(Q_PER_KV // 2):
            qp = qs_ref[pl.ds(r0, q_rows), j * width:(j + 1) * width]
            heads = []
            for h in range(2):
                qm = jnp.where(head_of_lane == h, qp, jnp.zeros_like(qp))
                m = l = acc = None
                for c in range(n_chunks):
                    keys = slice(c * KEY_CHUNK, (c + 1) * KEY_CHUNK)
                    s = _dg(qm, kd_ref[keys, :], _NT)
                    m_c = jnp.max(s, axis=-1, keepdims=True)
                    if c == 0:
                        m = m_c
                        p = jnp.exp(s - m)
                        l = jnp.sum(p, axis=-1, keepdims=True)
                        acc = _dot(p.astype(BF16), vd_ref[keys, :])
                    else:
                        m_new = jnp.maximum(m, m_c)
                        alpha = jnp.exp(m - m_new)
                        p = jnp.exp(s - m_new)
                        l = alpha * l + jnp.sum(p, axis=-1, keepdims=True)
                        acc = alpha * acc + _dot(p.astype(BF16), vd_ref[keys, :])
                        m = m_new
                heads.append(acc / l)
            outs.append(jnp.where(head_of_lane == 0, heads[0], heads[1]))
        o_ref[pl.ds(r0, q_rows), :] = jnp.concatenate(outs, axis=-1).astype(BF16)
        return carry

    lax.fori_loop(0, t // q_rows, body, 0)


def _attention(qkv, qg, kg, cache=None, rope=None):
    b, t, _ = qkv.shape
    d = HEAD_DIM
    latent = cache is not None
    past_len = cache[0].shape[2] if latent else 0
    total = t + past_len
    assert total % KEY_CHUNK == 0 and t % min(Q_ROWS, t) == 0 and KV_WIDTH == 2 * d
    gw = Q_PER_KV * d
    q_spec = pl.BlockSpec((None, t, gw), lambda i, g: (i, 0, g))
    k_spec = pl.BlockSpec((None, t, KV_WIDTH), lambda i, g: (i, 0, ATTN_WIDTH // KV_WIDTH))
    v_spec = pl.BlockSpec((None, t, KV_WIDTH), lambda i, g: (i, 0, ATTN_WIDTH // KV_WIDTH + 1))
    hm_spec = pl.BlockSpec((None, None, t, d), lambda i, g: (i, g, 0, 0))
    vec_spec = pl.BlockSpec((1, d), lambda i, g: (0, 0))
    out_shape = jax.ShapeDtypeStruct((b, t, ATTN_WIDTH), BF16)
    scratch = [pltpu.VMEM((t, gw), BF16), pltpu.VMEM((total, 2 * d), BF16),
               pltpu.VMEM((total, 2 * d), BF16)]
    kern = functools.partial(_attn_kernel, seq_len=t, past_len=past_len, latent=latent)
    if latent:
        ck, cv = cache
        cos4, sin4 = rope
        c_spec = pl.BlockSpec((None, None, past_len, d), lambda i, g: (i, g, 0, 0))
        tab_spec = pl.BlockSpec((t, 2 * d), lambda i, g: (0, 0))
        return pl.pallas_call(
            kern,
            out_shape=out_shape,
            grid=(b, KV_HEADS),
            in_specs=[q_spec, k_spec, v_spec, c_spec, c_spec, tab_spec, tab_spec,
                      vec_spec, vec_spec],
            out_specs=q_spec,
            scratch_shapes=scratch,
            compiler_params=_params("arbitrary", "arbitrary"),
            name="attn_latent",
        )(qkv, qkv, qkv, ck, cv, cos4, sin4, qg, kg)
    hm_shape = jax.ShapeDtypeStruct((b, KV_HEADS, t, d), F32)
    return pl.pallas_call(
        kern,
        out_shape=(out_shape, hm_shape, hm_shape),
        grid=(b, KV_HEADS),
        in_specs=[q_spec, k_spec, v_spec, vec_spec, vec_spec],
        out_specs=(q_spec, hm_spec, hm_spec),
        scratch_shapes=scratch,
        compiler_params=_params("arbitrary", "arbitrary"),
        name="attn_context",
    )(qkv, qkv, qkv, qg, kg)


def _conv_rows(ref_rows, taps, start, n, total):
    cur = ref_rows(start, n)
    width = cur.shape[-1]
    zero = jnp.zeros((1, width), F32)
    prev_row = ref_rows(start - 1, 1) if start > 0 else zero
    next_row = ref_rows(start + n, 1) if start + n < total else zero
    ridx = lax.broadcasted_iota(jnp.int32, (n, 1), 0)
    prev = jnp.where(ridx == 0, prev_row, pltpu.roll(cur, 1, 0))
    nxt = jnp.where(ridx == n - 1, next_row, pltpu.roll(cur, n - 1, 0))
    return prev * taps[0] + cur * taps[1] + nxt * taps[2]


def _round_robin(chains):
    results = [None] * len(chains)
    live = list(range(len(chains)))
    while live:
        for i in list(live):
            try:
                next(chains[i])
            except StopIteration as done:
                results[i] = done.value
                live.remove(i)
    return results


def _rwkv_kernel(*refs, seq_len, has_state):
    n_in = 19 if has_state else 18
    (r_ref, k_ref, v_ref, lora_ref, cr_ref, ck_ref, cv_ref, cl_ref, w0_ref, w2_ref, a0_ref, a2_ref,
     g2_ref, kk_ref, ka_ref, rk_ref, lng_ref, lnb_ref) = refs[:18]
    s0_ref = refs[18] if has_state else None
    y_ref, sfin_ref, rs_ref, ks_ref, vs_ref, ls_ref, yacc_ref, bacc_ref, st_ref = refs[n_in:]
    t = seq_len
    step_rows = min(STEP_ROWS, t)
    n_steps = t // step_rows
    n_sub = step_rows // GROUP
    n_chunks = GROUP // CHUNK
    n = R_HEAD_DIM
    width = 2 * n
    step = pl.program_id(2)
    pair_lanes = [slice(q * width, (q + 1) * width) for q in range(PAIRS_PER_STEP)]

    @pl.when(step == 0)
    def _():
        for blk in range(n_steps):
            start = blk * step_rows
            rows = slice(start, start + step_rows)
            for src, cw, dst in ((r_ref, cr_ref, rs_ref), (k_ref, ck_ref, ks_ref),
                                 (v_ref, cv_ref, vs_ref), (lora_ref, cl_ref, ls_ref)):
                taps = [cw[j:j + 1, :] for j in range(3)]
                dst[rows, :] = _conv_rows(lambda s, m, src=src: src[s:s + m, :],
                                          taps, start, step_rows, t)
        zero = jnp.zeros((n, n), F32)
        for d in range(2):
            for q in range(PAIRS_PER_STEP):
                if has_state:
                    s_a, s_b = s0_ref[d, 2 * q], s0_ref[d, 2 * q + 1]
                    bd = jnp.concatenate([jnp.concatenate([s_a, zero], axis=1),
                                          jnp.concatenate([zero, s_b], axis=1)], axis=0)
                    st_ref[d, q] = bd.T
                else:
                    st_ref[d, q] = jnp.zeros((width, width), F32)

    lane = lax.broadcasted_iota(jnp.int32, (1, width), 1)
    head0 = lane < R_HEAD_DIM

    def per_head(x):
        x0 = jnp.where(head0, x, 0.0)
        return (x0, x - x0)

    def pick(x0, x1):
        return jnp.where(head0, x0, x1)

    def pair_mm(mats, x):
        return pick(*[_mm(m, x) for m in mats])

    def pair_mm2(mats, x, y):
        xy = jnp.concatenate([x, y], axis=1)
        prods = [_mm(m, xy) for m in mats]
        return (pick(*[p[:, :width] for p in prods]), pick(*[p[:, width:] for p in prods]))

    def head_sum(x):
        s0 = jnp.sum(jnp.where(head0, x, 0.0), axis=-1, keepdims=True)
        s1 = jnp.sum(jnp.where(head0, 0.0, x), axis=-1, keepdims=True)
        return jnp.where(head0, s0, s1)

    def scan_rows(d, q, blk, state):
        rev = d == 1
        lanes = pair_lanes[q]
        base = blk * step_rows
        ri = lax.broadcasted_iota(jnp.int32, (GROUP, GROUP), 0)
        ci = lax.broadcasted_iota(jnp.int32, (GROUP, GROUP), 1)
        same = (ri // CHUNK) == (ci // CHUNK)
        eye = (ri == ci).astype(F32)
        strict = same & ((ci > ri) if rev else (ci < ri))
        incl = same & ((ci >= ri) if rev else (ci <= ri))
        tri = incl.astype(BF16)

        def half_mask(m):
            return ((ri // (2 * m)) == (ci // (2 * m))) & ((ri // m) != (ci // m))

        subs = []
        for s in (range(n_sub - 1, -1, -1) if rev else range(n_sub)):
            rows = pl.ds(pl.multiple_of(base + s * GROUP, GROUP), GROUP)
            lo = ls_ref[rows, :]
            wd_t = jnp.tanh(lo[:, 0:DECAY_RANK])
            z = w0_ref[d:d + 1, lanes] + _mm(wd_t, w2_ref[d, :, lanes])
            subs.append({"rows": rows, "r": rs_ref[rows, lanes], "k": ks_ref[rows, lanes],
                         "v": vs_ref[rows, lanes], "ad": lo[:, DECAY_RANK:DECAY_RANK + ICLR_RANK],
                         "lw": -DECAY_SCALE * _sigmoid(z)})
        yield
        for p in subs:
            p["cum"] = _mm_exact_lhs(tri, p["lw"])
            p["cum_c"] = jnp.concatenate(
                [jnp.broadcast_to(p["cum"][r:r + 1, :], (CHUNK, width))
                 for r in (range(0, GROUP, CHUNK) if rev else range(CHUNK - 1, GROUP, CHUNK))], axis=0)
            a = _sigmoid(a0_ref[d:d + 1, lanes] + _mm(p["ad"], a2_ref[d, :, lanes]))
            p["kd"] = p["k"] * (1.0 + (a - 1.0) * ka_ref[:, lanes])
            kk = p["k"] * kk_ref[:, lanes]
            p["kk"] = kk * lax.rsqrt(head_sum(kk * kk) + 1e-12)
            p["b"] = p["kk"] * a
            bacc_ref[d, p["rows"], lanes] = head_sum(p["r"] * p["kd"] * rk_ref[:, lanes]) * p["v"]
        yield
        for p in subs:
            cum, cum_c = p["cum"], p["cum_c"]
            p["a_t"] = -p["kk"] * jnp.exp(cum - p["lw"])
            p["r_t"] = p["r"] * jnp.exp(cum)
            inv = jnp.exp(-cum)
            p["b_t"] = p["b"] * inv
            p["k_t"] = p["kd"] * inv
            tail = jnp.exp(cum_c - cum)
            p["b_h"] = p["b"] * tail
            p["k_h"] = p["kd"] * tail
            p["p_c"] = jnp.exp(cum_c)
        yield
        for p in subs:
            bk = jnp.concatenate([p["b_t"], p["k_t"]], axis=0)
            xa = [_mm(x, bk, _NT) for x in per_head(p["a_t"])]
            xr = [_mm(x, bk, _NT) for x in per_head(p["r_t"])]
            p["a_ab"] = [jnp.where(strict, m[:, :GROUP], 0.0) for m in xa]
            p["a_ak"] = [jnp.where(strict, m[:, GROUP:], 0.0) for m in xa]
            p["a_rb"] = [jnp.where(incl, m[:, :GROUP], 0.0) for m in xr]
            p["a_rk"] = [jnp.where(incl, m[:, GROUP:], 0.0) for m in xr]
        yield
        for p in subs:
            p["akv"] = pair_mm(p["a_ak"], p["v"])
            p["rkv"] = pair_mm(p["a_rk"], p["v"])
            p["tm"] = [eye + jnp.where(half_mask(1), m, 0.0) for m in p["a_ab"]]
        m_size = 2
        while m_size < CHUNK:
            yield
            off = half_mask(m_size)
            for p in subs:
                p["ot"] = [_mm(jnp.where(off, n_mat, 0.0), tm)
                           for n_mat, tm in zip(p["a_ab"], p["tm"])]
            yield
            for p in subs:
                p["tm"] = [tm + _mm(tm, ot) for tm, ot in zip(p["tm"], p["ot"])]
            m_size *= 2
        yield
        for p in subs:
            p["ta"], p["tk"] = pair_mm2(p["tm"], p["a_t"], p["akv"])
        yield
        cr = [slice(c * CHUNK, (c + 1) * CHUNK) for c in range(n_chunks)]
        for p in subs:
            ra, rk = pair_mm2(p["a_rb"], p["ta"], p["tk"])
            p["r2"] = p["r_t"] + ra
            p["y2"] = p["rkv"] + rk
            p["m_t"] = [eye * p["p_c"][c * CHUNK:c * CHUNK + 1, :]
                        + jnp.where(same, _mm(p["b_h"][cr[c]], p["ta"][cr[c]], _TN), 0.0)
                        for c in range(n_chunks)]
            p["g_t"] = [jnp.where(same, _mm(jnp.concatenate([p["b_h"][cr[c]], p["k_h"][cr[c]]], axis=0),
                                            jnp.concatenate([p["tk"][cr[c]], p["v"][cr[c]]], axis=0),
                                            _TN), 0.0)
                        for c in range(n_chunks)]
        for p in subs:
            ys = [None] * n_chunks
            for c in (range(n_chunks - 1, -1, -1) if rev else range(n_chunks)):
                yield
                ys[c] = _mm(p["r2"][cr[c]], state) + p["y2"][cr[c]]
                state = _mm(p["m_t"][c], state) + p["g_t"][c]
            yacc_ref[d, p["rows"], lanes] = jnp.concatenate(ys, axis=0)
        return state

    chains = [(d, q) for q in range(PAIRS_PER_STEP) for d in range(2)]
    blocks = (step, n_steps - 1 - step)
    new_state = _round_robin([scan_rows(d, q, blocks[d], st_ref[d, q]) for d, q in chains])
    for (d, q), s_new in zip(chains, new_state):
        st_ref[d, q] = s_new

    @pl.when(step == n_steps - 1)
    def _():
        for d, q in chains:
            s_vk = st_ref[d, q].T
            sfin_ref[d, 2 * q] = s_vk[0:n, 0:n]
            sfin_ref[d, 2 * q + 1] = s_vk[n:width, n:width]
        inv_n = 1.0 / n
        for blk in range(n_steps):
            rows = slice(blk * step_rows, (blk + 1) * step_rows)
            gd = _sigmoid(ls_ref[rows, DECAY_RANK + ICLR_RANK:LORA_WIDTH]).astype(BF16)
            for lanes in pair_lanes:
                y = yacc_ref[0, rows, lanes] + yacc_ref[1, rows, lanes]
                yc = y - head_sum(y) * inv_n
                yn = yc * lax.rsqrt(head_sum(yc * yc) * inv_n + GN_EPS)
                out = (yn * lng_ref[:, lanes] + lnb_ref[:, lanes]
                       + (bacc_ref[0, rows, lanes] + bacc_ref[1, rows, lanes]))
                y_ref[rows, lanes] = (out * _dot(gd, g2_ref[:, lanes].astype(BF16))).astype(BF16)


def _rwkv(rkv, lora, p, s0):
    n_seq, t, _ = rkv.shape
    has_state = s0 is not None
    n = R_HEAD_DIM
    heads = 2 * PAIRS_PER_STEP
    width = heads * n
    blocks = R_WIDTH // width
    col = lambda off: (lambda s, i, j: (s, 0, off + i))
    vec = lambda off: (lambda s, i, j: (0, off + i))
    dir_mat = lambda s, i, j: (0, 0, i)
    in_specs = [
        pl.BlockSpec((None, t, width), col(0)),
        pl.BlockSpec((None, t, width), col(blocks)),
        pl.BlockSpec((None, t, width), col(2 * blocks)),
        pl.BlockSpec((None, t, LORA_WIDTH), lambda s, i, j: (s, 0, 0)),
        pl.BlockSpec((3, width), vec(0)),
        pl.BlockSpec((3, width), vec(blocks)),
        pl.BlockSpec((3, width), vec(2 * blocks)),
        pl.BlockSpec((3, LORA_WIDTH), lambda s, i, j: (0, 3 * R_WIDTH // LORA_WIDTH)),
        pl.BlockSpec((2, width), vec(0)),
        pl.BlockSpec((2, DECAY_RANK, width), dir_mat),
        pl.BlockSpec((2, width), vec(0)),
        pl.BlockSpec((2, ICLR_RANK, width), dir_mat),
        pl.BlockSpec((GATE_RANK, width), vec(0)),
        pl.BlockSpec((1, width), vec(0)),
        pl.BlockSpec((1, width), vec(0)),
        pl.BlockSpec((1, width), vec(0)),
        pl.BlockSpec((1, width), vec(0)),
        pl.BlockSpec((1, width), vec(0)),
    ]
    conv = p["conv"]
    args = [rkv, rkv, rkv, lora, conv, conv, conv, conv, p["w0"], p["w2"], p["a0"], p["a2"],
            p["g2"], p["k_k"], p["k_a"], p["r_k"], p["ln_g"], p["ln_b"]]
    state_spec = pl.BlockSpec((None, 2, heads, n, n), lambda s, i, j: (s, 0, i, 0, 0))
    if has_state:
        in_specs.append(state_spec)
        args.append(s0)
    seq_buf = pltpu.VMEM((t, width), F32)
    dir_buf = pltpu.VMEM((2, t, width), F32)
    return pl.pallas_call(
        functools.partial(_rwkv_kernel, seq_len=t, has_state=has_state),
        out_shape=(jax.ShapeDtypeStruct((n_seq, t, R_WIDTH), BF16),
                   jax.ShapeDtypeStruct((n_seq, 2, R_HEADS, n, n), F32)),
        grid=(n_seq, blocks, t // min(STEP_ROWS, t)),
        in_specs=in_specs,
        out_specs=(pl.BlockSpec((None, t, width), col(0)), state_spec),
        scratch_shapes=[seq_buf, seq_buf, seq_buf, pltpu.VMEM((t, LORA_WIDTH), F32),
                        dir_buf, dir_buf, pltpu.VMEM((2, PAIRS_PER_STEP, 2 * n, 2 * n), F32)],
        compiler_params=_params("arbitrary", "arbitrary", "arbitrary"),
        name="rwkv7_latent" if has_state else "rwkv7_context",
    )(*args)


def _merge_kernel(x_ref, ao_ref, ro_ref, sga_ref, sgr_ref, mod_ref, g_ref, wab_ref, wrb_ref,
                  wo_ref, x1_ref, h2_ref):
    merged = (sga_ref[...] * _dot(ao_ref[...], wab_ref[...])
              + sgr_ref[...] * _dot(ro_ref[...], wrb_ref[...]))
    x1 = x_ref[...] + mod_ref[2:3, :] * _dot(merged.astype(BF16), wo_ref[...])
    x1_ref[...] = x1
    h2 = _rms(x1) * g_ref[...]
    h2_ref[...] = (h2 * (1.0 + mod_ref[4:5, :]) + mod_ref[3:4, :]).astype(BF16)


def _merge(x2d, attn_o, rwkv_o, sga, sgr, mod, norm2_g, wab, wrb, wo, seq_len):
    n_tok = x2d.shape[0]
    tm = PROJ_ROWS
    row = lambda i: (i, 0)
    const = lambda i: (0, 0)
    return pl.pallas_call(
        _merge_kernel,
        out_shape=(jax.ShapeDtypeStruct((n_tok, D_MODEL), F32),
                   jax.ShapeDtypeStruct((n_tok, D_MODEL), BF16)),
        grid=(n_tok // tm,),
        in_specs=[
            pl.BlockSpec((tm, D_MODEL), row),
            pl.BlockSpec((tm, ATTN_WIDTH), row),
            pl.BlockSpec((tm, R_WIDTH), row),
            pl.BlockSpec((tm, D_MODEL), row),
            pl.BlockSpec((tm, D_MODEL), row),
            pl.BlockSpec((None, N_MOD, D_MODEL), lambda i: ((i * tm) // seq_len, 0, 0)),
            pl.BlockSpec((1, D_MODEL), const),
            pl.BlockSpec((ATTN_WIDTH, D_MODEL), const),
            pl.BlockSpec((R_WIDTH, D_MODEL), const),
            pl.BlockSpec((D_MODEL, D_MODEL), const),
        ],
        out_specs=(pl.BlockSpec((tm, D_MODEL), row), pl.BlockSpec((tm, D_MODEL), row)),
        compiler_params=_params("arbitrary"),
        name="merge_out_proj",
    )(x2d, attn_o, rwkv_o, sga, sgr, mod, norm2_g, wab, wrb, wo)


def _ffn_kernel(h2_ref, x1_ref, mod_ref, wu_ref, wg_ref, cu_ref, cg_ref, wd_ref, fg_ref,
                y_ref, acc_ref, upu_ref, upg_ref, *, seq_len):
    i = pl.program_id(0)
    j = pl.program_id(1)
    rows = h2_ref.shape[0]
    pad = FFN_PAD_ROWS
    rb = FFN_CONV_ROWS
    tile = 8

    @pl.when((i == 0) & (j == 0))
    def _():
        for ref in (upu_ref, upg_ref):
            ref[0:pad, :] = jnp.zeros((pad, ref.shape[1]), F32)
            ref[pad + rows:pad + rows + pad, :] = jnp.zeros((pad, ref.shape[1]), F32)

    @pl.when(j == 0)
    def _():
        acc_ref[...] = jnp.zeros(acc_ref.shape, F32)

    def up_project(r0):
        h2 = h2_ref[r0:r0 + rb, :]
        upu_ref[pad + r0:pad + r0 + rb, :] = _dot(h2, wu_ref[...])
        upg_ref[pad + r0:pad + r0 + rb, :] = _dot(h2, wg_ref[...])

    row0 = lax.broadcasted_iota(jnp.int32, (tile, 1), 0) == 0
    row7 = lax.broadcasted_iota(jnp.int32, (tile, 1), 0) == tile - 1

    taps_u = [cu_ref[k:k + 1, :] for k in range(3)]
    taps_g = [cg_ref[k:k + 1, :] for k in range(3)]

    def zero_rows(x, tile_offsets, row_mask):
        parts, done = [], 0
        for off in tile_offsets:
            parts += [x[done:off], jnp.where(row_mask, 0.0, x[off:off + tile])]
            done = off + tile
        parts.append(x[done:])
        return jnp.concatenate([p for p in parts if p.shape[0]], axis=0)

    def conv(up_ref, taps, r0):
        base = pad + r0
        cur = up_ref[base:base + rb, :]
        prev = up_ref[base - 1:base - 1 + rb, :]
        nxt = up_ref[base + 1:base + 1 + rb, :]
        first_seq = -(-r0 // seq_len) * seq_len
        starts = [s - r0 for s in range(first_seq, r0 + rb, seq_len)]
        ends = [e - r0 for e in range(first_seq if first_seq > r0 else first_seq + seq_len,
                                      r0 + rb + 1, seq_len)]
        prev = zero_rows(prev, starts, row0)
        nxt = zero_rows(nxt, [e - tile for e in ends], row7)
        return prev * taps[0] + cur * taps[1] + nxt * taps[2]

    starts = list(range(0, rows, rb))
    for r0 in starts[:2]:
        up_project(r0)
    for b, r0 in enumerate(starts):
        up_u = conv(upu_ref, taps_u, r0)
        up_g = conv(upg_ref, taps_g, r0)
        act = (up_g * _sigmoid(up_g) * up_u).astype(BF16)
        if b + 2 < len(starts):
            up_project(starts[b + 2])
        acc_ref[r0:r0 + rb, :] += _dot(act, wd_ref[...])

    @pl.when(j == pl.num_programs(1) - 1)
    def _():
        x2 = x1_ref[...] + mod_ref[...] * acc_ref[...]
        y_ref[...] = _rms(x2) * fg_ref[...]


def _ffn(h2, x1, gate2, ffn_up, ffn_conv, ffn_down, final_g, seq_len):
    n_tok = h2.shape[0]
    tm = FFN_ROWS
    tn = FFN_TILE
    nt = D_FF // tn
    assert tm % seq_len == 0 or seq_len % tm == 0
    gate_map = lambda i, j: ((i * tm) // seq_len, 0, 0)
    return pl.pallas_call(
        functools.partial(_ffn_kernel, seq_len=seq_len),
        out_shape=jax.ShapeDtypeStruct((n_tok, D_MODEL), F32),
        grid=(n_tok // tm, nt),
        in_specs=[
            pl.BlockSpec((tm, D_MODEL), lambda i, j: (i, 0)),
            pl.BlockSpec((tm, D_MODEL), lambda i, j: (i, 0)),
            pl.BlockSpec((None, 1, D_MODEL), gate_map),
            pl.BlockSpec((D_MODEL, tn), lambda i, j: (0, j)),
            pl.BlockSpec((D_MODEL, tn), lambda i, j: (0, nt + j)),
            pl.BlockSpec((3, tn), lambda i, j: (0, j)),
            pl.BlockSpec((3, tn), lambda i, j: (0, nt + j)),
            pl.BlockSpec((tn, D_MODEL), lambda i, j: (j, 0)),
            pl.BlockSpec((1, D_MODEL), lambda i, j: (0, 0)),
        ],
        out_specs=pl.BlockSpec((tm, D_MODEL), lambda i, j: (i, 0)),
        scratch_shapes=[pltpu.VMEM((tm, D_MODEL), F32),
                        pltpu.VMEM((tm + 2 * FFN_PAD_ROWS, tn), F32),
                        pltpu.VMEM((tm + 2 * FFN_PAD_ROWS, tn), F32)],
        compiler_params=_params("arbitrary", "arbitrary"),
        name="conv_ffn",
    )(h2, x1, gate2, ffn_up, ffn_up, ffn_conv, ffn_conv, ffn_down, final_g)


def _layer_path(x, mod, w, rope, cache, s0):
    b, t, _ = x.shape
    x2d = x.reshape(b * t, D_MODEL)
    qkv, rkv, lora, sga, sgr = _in_projection(x2d, mod, w["norm1_g"], w["w_in"], t)
    qkv = qkv.reshape(b, t, ATTN_WIDTH + 2 * KV_WIDTH)
    if rope is None:
        attn_o, k_norm, v_hm = _attention(qkv, w["q_norm_g"], w["k_norm_g"])
    else:
        attn_o = _attention(qkv, w["q_norm_g"], w["k_norm_g"], cache=cache, rope=rope)
        k_norm = v_hm = None
    rwkv_o, s_fin = _rwkv(rkv.reshape(b, t, 3 * R_WIDTH), lora.reshape(b, t, LORA_WIDTH),
                          w["rwkv"], s0)
    x1, h2 = _merge(x2d, attn_o.reshape(b * t, ATTN_WIDTH), rwkv_o.reshape(b * t, R_WIDTH),
                    sga, sgr, mod, w["norm2_g"], w["w_attn_br"], w["w_rwkv_br"], w["w_out"], t)
    gate2 = mod[:, 5:6, :]
    y = _ffn(h2, x1, gate2, w["ffn_up"], w["ffn_conv"], w["ffn_down"], w["final_norm_g"], t)
    return y.reshape(b, t, D_MODEL), k_norm, v_hm, s_fin


def _rope_tables(rows):
    row = jnp.repeat(jnp.arange(rows), GRID_W).astype(F32)
    col = jnp.tile(jnp.arange(GRID_W), rows).astype(F32)
    inv = 1.0 / (ROPE_THETA ** (jnp.arange(ROPE_PAIRS, dtype=F32) / ROPE_PAIRS))
    ang = jnp.concatenate([row[:, None] * inv, col[:, None] * inv], axis=-1)
    cos, sin = jnp.cos(ang), jnp.sin(ang)
    return jnp.concatenate([cos, cos] * 2, axis=-1), jnp.concatenate([-sin, sin] * 2, axis=-1)


def kernel(x_prompt, x_sample, cache_k, cache_v, state_rwkv, c, c_ctx, w_ada, b_ada, norm1_g, w_in, q_norm_g, k_norm_g, rwkv_conv, rwkv_w0, rwkv_w2, rwkv_a0, rwkv_a2, rwkv_g2, rwkv_k_k, rwkv_k_a, rwkv_r_k, rwkv_ln_g, rwkv_ln_b, w_attn_br, w_rwkv_br, w_out, norm2_g, ffn_up, ffn_conv, ffn_down, final_norm_g):
    depth = w_in.shape[0]
    assert depth == 1, "single trunk layer"
    l = 0
    n_ctx = x_prompt.shape[0]
    n_lat = x_sample.shape[0]

    cc = jnp.concatenate([c_ctx[None, :], c, jnp.zeros((8 - 1 - n_lat, D_MODEL), F32)], axis=0)
    mod_all = _modulation(cc, w_ada[l], b_ada[l][None, :])
    mod_ctx = jnp.broadcast_to(mod_all[0:1].reshape(1, N_MOD, D_MODEL), (n_ctx, N_MOD, D_MODEL))
    mod_lat = mod_all[1:1 + n_lat].reshape(n_lat, N_MOD, D_MODEL)

    rwkv_p = {
        "conv": rwkv_conv[l], "w0": rwkv_w0[l], "w2": rwkv_w2[l], "a0": rwkv_a0[l],
        "a2": rwkv_a2[l], "g2": rwkv_g2[l], "k_k": rwkv_k_k[l][None, :],
        "k_a": rwkv_k_a[l][None, :], "r_k": rwkv_r_k[l].reshape(1, R_WIDTH),
        "ln_g": rwkv_ln_g[l][None, :], "ln_b": rwkv_ln_b[l][None, :],
    }
    w = {
        "norm1_g": norm1_g[l][None, :], "w_in": w_in[l].astype(BF16),
        "q_norm_g": q_norm_g[l][None, :], "k_norm_g": k_norm_g[l][None, :],
        "rwkv": rwkv_p,
        "w_attn_br": w_attn_br[l].astype(BF16), "w_rwkv_br": w_rwkv_br[l].astype(BF16),
        "w_out": w_out[l].astype(BF16), "norm2_g": norm2_g[l][None, :],
        "ffn_up": ffn_up[l].astype(BF16), "ffn_conv": ffn_conv[l],
        "ffn_down": ffn_down[l].astype(BF16), "final_norm_g": final_norm_g[None, :],
    }

    y_prompt, k_ctx, v_ctx, s_ctx = _layer_path(x_prompt, mod_ctx, w, None, None, None)
    rope = _rope_tables(x_sample.shape[1] // GRID_W)
    y_sample, _, _, _ = _layer_path(x_sample, mod_lat, w, rope, (cache_k[:, l], cache_v[:, l]),
                                    state_rwkv[:, l])
    return (y_prompt, y_sample, k_ctx[:, None], v_ctx[:, None], s_ctx[:, None])
```

```python
import functools

import jax
import jax.numpy as jnp
import numpy as np
from jax import lax
from jax.experimental import pallas as pl
from jax.experimental.pallas import tpu as pltpu

D_MODEL = 1024
GRID_W = 64
N_HEADS = 8
KV_HEADS = 2
Q_PER_KV = N_HEADS // KV_HEADS
HEAD_DIM = 64
ATTN_WIDTH = N_HEADS * HEAD_DIM
KV_WIDTH = KV_HEADS * HEAD_DIM
ROPE_THETA = 10000.0
ROPE_PAIRS = HEAD_DIM // 4
R_HEADS = 8
R_HEAD_DIM = 64
R_WIDTH = R_HEADS * R_HEAD_DIM
DECAY_RANK = 64
ICLR_RANK = 64
GATE_RANK = 128
LORA_WIDTH = DECAY_RANK + ICLR_RANK + GATE_RANK
R_IN_WIDTH = 3 * R_WIDTH + LORA_WIDTH
D_FF = 2816
N_MOD = 6
NORM_EPS = 1e-6
GN_EPS = 64e-5
DECAY_SCALE = float(np.exp(-0.5))

F32 = jnp.float32
BF16 = jnp.bfloat16

VMEM_LIMIT_BYTES = 56 * 1024 * 1024
PROJ_ROWS = 512
IN_PROJ_ROWS = 1024
FFN_ROWS = 1024
FFN_TILE = 256
FFN_PAD_ROWS = 8
FFN_CONV_ROWS = 256
Q_ROWS = 512
KEY_CHUNK = 256
STEP_ROWS = 256
GROUP = 128
PAIRS_PER_STEP = 2
CHUNK = 64

_NN = (((1,), (0,)), ((), ()))
_NT = (((1,), (1,)), ((), ()))
_TN = (((0,), (0,)), ((), ()))


def _dg(a, b, dims):
    return lax.dot_general(a, b, dims, preferred_element_type=F32)


def _dot(a, b):
    return _dg(a, b, _NN)


def _mm(a, b, dims=_NN):
    return _dg(a.astype(BF16), b.astype(BF16), dims)


def _mm_exact_lhs(m, x):
    hi = x.astype(BF16)
    r1 = x - hi.astype(F32)
    mid = r1.astype(BF16)
    lo = (r1 - mid.astype(F32)).astype(BF16)
    return _dot(m, hi) + (_dot(m, mid) + _dot(m, lo))


def _mm_exact_lhs_t(x, m):
    hi = x.astype(BF16)
    r1 = x - hi.astype(F32)
    mid = r1.astype(BF16)
    lo = (r1 - mid.astype(F32)).astype(BF16)
    return _dot(hi, m) + (_dot(mid, m) + _dot(lo, m))


def _sigmoid(x):
    return 1.0 / (1.0 + jnp.exp(-x))


def _rms(x):
    return x * lax.rsqrt(jnp.mean(x * x, axis=-1, keepdims=True) + NORM_EPS)


def _params(*semantics):
    return pltpu.CompilerParams(dimension_semantics=semantics,
                                vmem_limit_bytes=VMEM_LIMIT_BYTES)


def _mod_kernel(c_ref, w_ref, b_ref, o_ref):
    c = c_ref[...]
    s = (c * _sigmoid(c)).astype(BF16)
    o_ref[...] = _dot(s, w_ref[...].astype(BF16)) + b_ref[...]


def _modulation(cc, w_ada, b_ada):
    rows = cc.shape[0]
    n = w_ada.shape[1]
    tile = 1024
    return pl.pallas_call(
        _mod_kernel,
        out_shape=jax.ShapeDtypeStruct((rows, n), F32),
        grid=(n // tile,),
        in_specs=[
            pl.BlockSpec((rows, D_MODEL), lambda j: (0, 0)),
            pl.BlockSpec((D_MODEL, tile), lambda j: (0, j)),
            pl.BlockSpec((1, tile), lambda j: (0, j)),
        ],
        out_specs=pl.BlockSpec((rows, tile), lambda j: (0, j)),
        compiler_params=_params("arbitrary"),
        name="adaln_mod",
    )(cc, w_ada, b_ada)


def _inproj_kernel(x_ref, mod_ref, g_ref, w_ref, qkv_ref, rkv_ref, lora_ref, sga_ref, sgr_ref):
    x = x_ref[...]
    h = _rms(x) * g_ref[...]
    h = h * (1.0 + mod_ref[1:2, :]) + mod_ref[0:1, :]
    hb = h.astype(BF16)
    c0 = ATTN_WIDTH + 2 * KV_WIDTH
    c1 = c0 + 3 * R_WIDTH
    c2 = c1 + LORA_WIDTH
    c3 = c2 + D_MODEL
    qkv_ref[...] = _dot(hb, w_ref[:, 0:c0])
    rkv_ref[...] = _dot(hb, w_ref[:, c0:c1])
    lora_ref[...] = _dot(hb, w_ref[:, c1:c2])
    sga_ref[...] = _sigmoid(_dot(hb, w_ref[:, c2:c3])).astype(BF16)
    sgr_ref[...] = _sigmoid(_dot(hb, w_ref[:, c3:c3 + D_MODEL])).astype(BF16)


def _in_projection(x2d, mod, norm1_g, w_in_bf16, seq_len):
    n_tok = x2d.shape[0]
    tm = IN_PROJ_ROWS
    in_width = w_in_bf16.shape[1]
    widths = (ATTN_WIDTH + 2 * KV_WIDTH, 3 * R_WIDTH, LORA_WIDTH, D_MODEL, D_MODEL)
    dtypes = (F32, F32, F32, BF16, BF16)
    row = lambda i: (i, 0)
    return pl.pallas_call(
        _inproj_kernel,
        out_shape=tuple(jax.ShapeDtypeStruct((n_tok, w), dt) for w, dt in zip(widths, dtypes)),
        grid=(n_tok // tm,),
        in_specs=[
            pl.BlockSpec((tm, D_MODEL), row),
            pl.BlockSpec((None, N_MOD, D_MODEL), lambda i: ((i * tm) // seq_len, 0, 0)),
            pl.BlockSpec((1, D_MODEL), lambda i: (0, 0)),
            pl.BlockSpec((D_MODEL, in_width), lambda i: (0, 0), pipeline_mode=pl.Buffered(1)),
        ],
        out_specs=tuple(pl.BlockSpec((tm, w), row) for w in widths),
        compiler_params=_params("arbitrary"),
        name="in_proj",
    )(x2d, mod, norm1_g, w_in_bf16)


def _attn_kernel(*refs, seq_len, past_len, latent):
    if latent:
        (q_ref, k_ref, v_ref, ck_ref, cv_ref, cos_ref, sin_ref, qg_ref, kg_ref,
         o_ref, qs_ref, kd_ref, vd_ref) = refs
    else:
        (q_ref, k_ref, v_ref, qg_ref, kg_ref, o_ref, kn_ref, vn_ref, qs_ref, kd_ref, vd_ref) = refs
    t = seq_len
    d = HEAD_DIM
    width = 2 * d
    total = t + past_len
    group = pl.program_id(1)
    lane = lax.broadcasted_iota(jnp.int32, (1, width), 1)
    head_of_lane = lane // d
    low_half = (lane % d) < d // 2
    inv_d = 1.0 / d
    same_head = ((lax.broadcasted_iota(jnp.int32, (width, width), 0) // d)
                 == (lax.broadcasted_iota(jnp.int32, (width, width), 1) // d)).astype(BF16)

    def pair_norm(x, gain):
        ms = _mm_exact_lhs_t(x * x, same_head) * inv_d
        y = x * lax.rsqrt(ms + NORM_EPS) * gain
        if latent:
            partner = jnp.where(low_half, pltpu.roll(y, width - d // 2, 1), pltpu.roll(y, d // 2, 1))
            y = y * cos_ref[...] + partner * sin_ref[...]
        return y

    def this_group(x):
        return jnp.where(head_of_lane == group, x, pltpu.roll(x, d, 1))

    kg2 = jnp.concatenate([kg_ref[...], kg_ref[...]], axis=-1)
    qg2 = jnp.concatenate([qg_ref[...], qg_ref[...]], axis=-1)
    v_dup = this_group(v_ref[...])
    if latent:
        kd_ref[0:t, :] = this_group(pair_norm(k_ref[...], kg2)).astype(BF16)
        ck = ck_ref[...].astype(BF16)
        kd_ref[t:total, :] = jnp.concatenate([ck, ck], axis=-1)
        vd_ref[0:t, :] = v_dup.astype(BF16)
        cv = cv_ref[...].astype(BF16)
        vd_ref[t:total, :] = jnp.concatenate([cv, cv], axis=-1)
    else:
        k_dup = this_group(pair_norm(k_ref[...], kg2))
        kn_ref[...] = k_dup[:, 0:d]
        vn_ref[...] = v_dup[:, 0:d]
        kd_ref[...] = k_dup.astype(BF16)
        vd_ref[...] = v_dup.astype(BF16)
    scale = d ** -0.5
    for j in range(Q_PER_KV // 2):
        cols = slice(j * width, (j + 1) * width)
        qs_ref[:, cols] = (pair_norm(q_ref[:, cols], qg2) * scale).astype(BF16)
    n_chunks = total // KEY_CHUNK
    q_rows = min(Q_ROWS, t)

    def body(qb, carry):
        r0 = pl.multiple_of(qb * q_rows, q_rows)
        outs = []
        for j in range(Q_PER_KV // 2):
            qp = qs_ref[pl.ds(r0, q_rows), j * width:(j + 1) * width]
            heads = []
            for h in range(2):
                qm = jnp.where(head_of_lane == h, qp, jnp.zeros_like(qp))
                m = l = acc = None
                for c in range(n_chunks):
                    keys = slice(c * KEY_CHUNK, (c + 1) * KEY_CHUNK)
                    s = _dg(qm, kd_ref[keys, :], _NT)
                    m_c = jnp.max(s, axis=-1, keepdims=True)
                    if c == 0:
                        m = m_c
                        p = jnp.exp(s - m)
                        l = jnp.sum(p, axis=-1, keepdims=True)
                        acc = _dot(p.astype(BF16), vd_ref[keys, :])
                    else:
                        m_new = jnp.maximum(m, m_c)
                        alpha = jnp.exp(m - m_new)
                        p = jnp.exp(s - m_new)
                        l = alpha * l + jnp.sum(p, axis=-1, keepdims=True)
                        acc = alpha * acc + _dot(p.astype(BF16), vd_ref[keys, :])
                        m = m_new
                heads.append(acc / l)
            outs.append(jnp.where(head_of_lane == 0, heads[0], heads[1]))
        o_ref[pl.ds(r0, q_rows), :] = jnp.concatenate(outs, axis=-1).astype(BF16)
        return carry

    lax.fori_loop(0, t // q_rows, body, 0)


def _attention(qkv, qg, kg, cache=None, rope=None):
    b, t, _ = qkv.shape
    d = HEAD_DIM
    latent = cache is not None
    past_len = cache[0].shape[2] if latent else 0
    total = t + past_len
    assert total % KEY_CHUNK == 0 and t % min(Q_ROWS, t) == 0 and KV_WIDTH == 2 * d
    gw = Q_PER_KV * d
    q_spec = pl.BlockSpec((None, t, gw), lambda i, g: (i, 0, g))
    k_spec = pl.BlockSpec((None, t, KV_WIDTH), lambda i, g: (i, 0, ATTN_WIDTH // KV_WIDTH))
    v_spec = pl.BlockSpec((None, t, KV_WIDTH), lambda i, g: (i, 0, ATTN_WIDTH // KV_WIDTH + 1))
    hm_spec = pl.BlockSpec((None, None, t, d), lambda i, g: (i, g, 0, 0))
    vec_spec = pl.BlockSpec((1, d), lambda i, g: (0, 0))
    out_shape = jax.ShapeDtypeStruct((b, t, ATTN_WIDTH), BF16)
    scratch = [pltpu.VMEM((t, gw), BF16), pltpu.VMEM((total, 2 * d), BF16),
               pltpu.VMEM((total, 2 * d), BF16)]
    kern = functools.partial(_attn_kernel, seq_len=t, past_len=past_len, latent=latent)
    if latent:
        ck, cv = cache
        cos4, sin4 = rope
        c_spec = pl.BlockSpec((None, None, past_len, d), lambda i, g: (i, g, 0, 0))
        tab_spec = pl.BlockSpec((t, 2 * d), lambda i, g: (0, 0))
        return pl.pallas_call(
            kern,
            out_shape=out_shape,
            grid=(b, KV_HEADS),
            in_specs=[q_spec, k_spec, v_spec, c_spec, c_spec, tab_spec, tab_spec,
                      vec_spec, vec_spec],
            out_specs=q_spec,
            scratch_shapes=scratch,
            compiler_params=_params("arbitrary", "arbitrary"),
            name="attn_latent",
        )(qkv, qkv, qkv, ck, cv, cos4, sin4, qg, kg)
    hm_shape = jax.ShapeDtypeStruct((b, KV_HEADS, t, d), F32)
    return pl.pallas_call(
        kern,
        out_shape=(out_shape, hm_shape, hm_shape),
        grid=(b, KV_HEADS),
        in_specs=[q_spec, k_spec, v_spec, vec_spec, vec_spec],
        out_specs=(q_spec, hm_spec, hm_spec),
        scratch_shapes=scratch,
        compiler_params=_params("arbitrary", "arbitrary"),
        name="attn_context",
    )(qkv, qkv, qkv, qg, kg)


def _conv_rows(ref_rows, taps, start, n, total):
    cur = ref_rows(start, n)
    width = cur.shape[-1]
    zero = jnp.zeros((1, width), F32)
    prev_row = ref_rows(start - 1, 1) if start > 0 else zero
    next_row = ref_rows(start + n, 1) if start + n < total else zero
    ridx = lax.broadcasted_iota(jnp.int32, (n, 1), 0)
    prev = jnp.where(ridx == 0, prev_row, pltpu.roll(cur, 1, 0))
    nxt = jnp.where(ridx == n - 1, next_row, pltpu.roll(cur, n - 1, 0))
    return prev * taps[0] + cur * taps[1] + nxt * taps[2]


def _round_robin(chains):
    results = [None] * len(chains)
    live = list(range(len(chains)))
    while live:
        for i in list(live):
            try:
                next(chains[i])
            except StopIteration as done:
                results[i] = done.value
                live.remove(i)
    return results


def _rwkv_kernel(*refs, seq_len, has_state):
    n_in = 19 if has_state else 18
    (r_ref, k_ref, v_ref, lora_ref, cr_ref, ck_ref, cv_ref, cl_ref, w0_ref, w2_ref, a0_ref, a2_ref,
     g2_ref, kk_ref, ka_ref, rk_ref, lng_ref, lnb_ref) = refs[:18]
    s0_ref = refs[18] if has_state else None
    y_ref, sfin_ref, rs_ref, ks_ref, vs_ref, ls_ref, yacc_ref, bacc_ref, st_ref = refs[n_in:]
    t = seq_len
    step_rows = min(STEP_ROWS, t)
    n_steps = t // step_rows
    n_sub = step_rows // GROUP
    n_chunks = GROUP // CHUNK
    n = R_HEAD_DIM
    width = 2 * n
    step = pl.program_id(2)
    pair_lanes = [slice(q * width, (q + 1) * width) for q in range(PAIRS_PER_STEP)]

    @pl.when(step == 0)
    def _():
        for blk in range(n_steps):
            start = blk * step_rows
            rows = slice(start, start + step_rows)
            for src, cw, dst in ((r_ref, cr_ref, rs_ref), (k_ref, ck_ref, ks_ref),
                                 (v_ref, cv_ref, vs_ref), (lora_ref, cl_ref, ls_ref)):
                taps = [cw[j:j + 1, :] for j in range(3)]
                dst[rows, :] = _conv_rows(lambda s, m, src=src: src[s:s + m, :],
                                          taps, start, step_rows, t)
        zero = jnp.zeros((n, n), F32)
        for d in range(2):
            for q in range(PAIRS_PER_STEP):
                if has_state:
                    s_a, s_b = s0_ref[d, 2 * q], s0_ref[d, 2 * q + 1]
                    bd = jnp.concatenate([jnp.concatenate([s_a, zero], axis=1),
                                          jnp.concatenate([zero, s_b], axis=1)], axis=0)
                    st_ref[d, q] = bd.T
                else:
                    st_ref[d, q] = jnp.zeros((width, width), F32)

    lane = lax.broadcasted_iota(jnp.int32, (1, width), 1)
    head0 = lane < R_HEAD_DIM

    def per_head(x):
        x0 = jnp.where(head0, x, 0.0)
        return (x0, x - x0)

    def pick(x0, x1):
        return jnp.where(head0, x0, x1)

    def pair_mm(mats, x):
        return pick(*[_mm(m, x) for m in mats])

    def pair_mm2(mats, x, y):
        xy = jnp.concatenate([x, y], axis=1)
        prods = [_mm(m, xy) for m in mats]
        return (pick(*[p[:, :width] for p in prods]), pick(*[p[:, width:] for p in prods]))

    def head_sum(x):
        s0 = jnp.sum(jnp.where(head0, x, 0.0), axis=-1, keepdims=True)
        s1 = jnp.sum(jnp.where(head0, 0.0, x), axis=-1, keepdims=True)
        return jnp.where(head0, s0, s1)

    def scan_rows(d, q, blk, state):
        rev = d == 1
        lanes = pair_lanes[q]
        base = blk * step_rows
        ri = lax.broadcasted_iota(jnp.int32, (GROUP, GROUP), 0)
        ci = lax.broadcasted_iota(jnp.int32, (GROUP, GROUP), 1)
        same = (ri // CHUNK) == (ci // CHUNK)
        eye = (ri == ci).astype(F32)
        strict = same & ((ci > ri) if rev else (ci < ri))
        incl = same & ((ci >= ri) if rev else (ci <= ri))
        tri = incl.astype(BF16)

        def half_mask(m):
            return ((ri // (2 * m)) == (ci // (2 * m))) & ((ri // m) != (ci // m))

        subs = []
        for s in (range(n_sub - 1, -1, -1) if rev else range(n_sub)):
            rows = pl.ds(pl.multiple_of(base + s * GROUP, GROUP), GROUP)
            lo = ls_ref[rows, :]
            wd_t = jnp.tanh(lo[:, 0:DECAY_RANK])
            z = w0_ref[d:d + 1, lanes] + _mm(wd_t, w2_ref[d, :, lanes])
            subs.append({"rows": rows, "r": rs_ref[rows, lanes], "k": ks_ref[rows, lanes],
                         "v": vs_ref[rows, lanes], "ad": lo[:, DECAY_RANK:DECAY_RANK + ICLR_RANK],
                         "lw": -DECAY_SCALE * _sigmoid(z)})
        yield
        for p in subs:
            p["cum"] = _mm_exact_lhs(tri, p["lw"])
            p["cum_c"] = jnp.concatenate(
                [jnp.broadcast_to(p["cum"][r:r + 1, :], (CHUNK, width))
                 for r in (range(0, GROUP, CHUNK) if rev else range(CHUNK - 1, GROUP, CHUNK))], axis=0)
            a = _sigmoid(a0_ref[d:d + 1, lanes] + _mm(p["ad"], a2_ref[d, :, lanes]))
            p["kd"] = p["k"] * (1.0 + (a - 1.0) * ka_ref[:, lanes])
            kk = p["k"] * kk_ref[:, lanes]
            p["kk"] = kk * lax.rsqrt(head_sum(kk * kk) + 1e-12)
            p["b"] = p["kk"] * a
            bacc_ref[d, p["rows"], lanes] = head_sum(p["r"] * p["kd"] * rk_ref[:, lanes]) * p["v"]
        yield
        for p in subs:
            cum, cum_c = p["cum"], p["cum_c"]
            p["a_t"] = -p["kk"] * jnp.exp(cum - p["lw"])
            p["r_t"] = p["r"] * jnp.exp(cum)
            inv = jnp.exp(-cum)
            p["b_t"] = p["b"] * inv
            p["k_t"] = p["kd"] * inv
            tail = jnp.exp(cum_c - cum)
            p["b_h"] = p["b"] * tail
            p["k_h"] = p["kd"] * tail
            p["p_c"] = jnp.exp(cum_c)
        yield
        for p in subs:
            bk = jnp.concatenate([p["b_t"], p["k_t"]], axis=0)
            xa = [_mm(x, bk, _NT) for x in per_head(p["a_t"])]
            xr = [_mm(x, bk, _NT) for x in per_head(p["r_t"])]
            p["a_ab"] = [jnp.where(strict, m[:, :GROUP], 0.0) for m in xa]
            p["a_ak"] = [jnp.where(strict, m[:, GROUP:], 0.0) for m in xa]
            p["a_rb"] = [jnp.where(incl, m[:, :GROUP], 0.0) for m in xr]
            p["a_rk"] = [jnp.where(incl, m[:, GROUP:], 0.0) for m in xr]
        yield
        for p in subs:
            p["akv"] = pair_mm(p["a_ak"], p["v"])
            p["rkv"] = pair_mm(p["a_rk"], p["v"])
            p["tm"] = [eye + jnp.where(half_mask(1), m, 0.0) for m in p["a_ab"]]
        m_size = 2
        while m_size < CHUNK:
            yield
            off = half_mask(m_size)
            for p in subs:
                p["ot"] = [_mm(jnp.where(off, n_mat, 0.0), tm)
                           for n_mat, tm in zip(p["a_ab"], p["tm"])]
            yield
            for p in subs:
                p["tm"] = [tm + _mm(tm, ot) for tm, ot in zip(p["tm"], p["ot"])]
            m_size *= 2
        yield
        for p in subs:
            p["ta"], p["tk"] = pair_mm2(p["tm"], p["a_t"], p["akv"])
        yield
        cr = [slice(c * CHUNK, (c + 1) * CHUNK) for c in range(n_chunks)]
        for p in subs:
            ra, rk = pair_mm2(p["a_rb"], p["ta"], p["tk"])
            p["r2"] = p["r_t"] + ra
            p["y2"] = p["rkv"] + rk
            p["m_t"] = [eye * p["p_c"][c * CHUNK:c * CHUNK + 1, :]
                        + jnp.where(same, _mm(p["b_h"][cr[c]], p["ta"][cr[c]], _TN), 0.0)
                        for c in range(n_chunks)]
            p["g_t"] = [jnp.where(same, _mm(jnp.concatenate([p["b_h"][cr[c]], p["k_h"][cr[c]]], axis=0),
                                            jnp.concatenate([p["tk"][cr[c]], p["v"][cr[c]]], axis=0),
                                            _TN), 0.0)
                        for c in range(n_chunks)]
        for p in subs:
            ys = [None] * n_chunks
            for c in (range(n_chunks - 1, -1, -1) if rev else range(n_chunks)):
                yield
                ys[c] = _mm(p["r2"][cr[c]], state) + p["y2"][cr[c]]
                state = _mm(p["m_t"][c], state) + p["g_t"][c]
            yacc_ref[d, p["rows"], lanes] = jnp.concatenate(ys, axis=0)
        return state

    chains = [(d, q) for q in range(PAIRS_PER_STEP) for d in range(2)]
    blocks = (step, n_steps - 1 - step)
    new_state = _round_robin([scan_rows(d, q, blocks[d], st_ref[d, q]) for d, q in chains])
    for (d, q), s_new in zip(chains, new_state):
        st_ref[d, q] = s_new

    @pl.when(step == n_steps - 1)
    def _():
        for d, q in chains:
            s_vk = st_ref[d, q].T
            sfin_ref[d, 2 * q] = s_vk[0:n, 0:n]
            sfin_ref[d, 2 * q + 1] = s_vk[n:width, n:width]
        inv_n = 1.0 / n
        for blk in range(n_steps):
            rows = slice(blk * step_rows, (blk + 1) * step_rows)
            gd = _sigmoid(ls_ref[rows, DECAY_RANK + ICLR_RANK:LORA_WIDTH]).astype(BF16)
            for lanes in pair_lanes:
                y = yacc_ref[0, rows, lanes] + yacc_ref[1, rows, lanes]
                yc = y - head_sum(y) * inv_n
                yn = yc * lax.rsqrt(head_sum(yc * yc) * inv_n + GN_EPS)
                out = (yn * lng_ref[:, lanes] + lnb_ref[:, lanes]
                       + (bacc_ref[0, rows, lanes] + bacc_ref[1, rows, lanes]))
                y_ref[rows, lanes] = (out * _dot(gd, g2_ref[:, lanes].astype(BF16))).astype(BF16)


def _rwkv(rkv, lora, p, s0):
    n_seq, t, _ = rkv.shape
    has_state = s0 is not None
    n = R_HEAD_DIM
    heads = 2 * PAIRS_PER_STEP
    width = heads * n
    blocks = R_WIDTH // width
    col = lambda off: (lambda s, i, j: (s, 0, off + i))
    vec = lambda off: (lambda s, i, j: (0, off + i))
    dir_mat = lambda s, i, j: (0, 0, i)
    in_specs = [
        pl.BlockSpec((None, t, width), col(0)),
        pl.BlockSpec((None, t, width), col(blocks)),
        pl.BlockSpec((None, t, width), col(2 * blocks)),
        pl.BlockSpec((None, t, LORA_WIDTH), lambda s, i, j: (s, 0, 0)),
        pl.BlockSpec((3, width), vec(0)),
        pl.BlockSpec((3, width), vec(blocks)),
        pl.BlockSpec((3, width), vec(2 * blocks)),
        pl.BlockSpec((3, LORA_WIDTH), lambda s, i, j: (0, 3 * R_WIDTH // LORA_WIDTH)),
        pl.BlockSpec((2, width), vec(0)),
        pl.BlockSpec((2, DECAY_RANK, width), dir_mat),
        pl.BlockSpec((2, width), vec(0)),
        pl.BlockSpec((2, ICLR_RANK, width), dir_mat),
        pl.BlockSpec((GATE_RANK, width), vec(0)),
        pl.BlockSpec((1, width), vec(0)),
        pl.BlockSpec((1, width), vec(0)),
        pl.BlockSpec((1, width), vec(0)),
        pl.BlockSpec((1, width), vec(0)),
        pl.BlockSpec((1, width), vec(0)),
    ]
    conv = p["conv"]
    args = [rkv, rkv, rkv, lora, conv, conv, conv, conv, p["w0"], p["w2"], p["a0"], p["a2"],
            p["g2"], p["k_k"], p["k_a"], p["r_k"], p["ln_g"], p["ln_b"]]
    state_spec = pl.BlockSpec((None, 2, heads, n, n), lambda s, i, j: (s, 0, i, 0, 0))
    if has_state:
        in_specs.append(state_spec)
        args.append(s0)
    seq_buf = pltpu.VMEM((t, width), F32)
    dir_buf = pltpu.VMEM((2, t, width), F32)
    return pl.pallas_call(
        functools.partial(_rwkv_kernel, seq_len=t, has_state=has_state),
        out_shape=(jax.ShapeDtypeStruct((n_seq, t, R_WIDTH), BF16),
                   jax.ShapeDtypeStruct((n_seq, 2, R_HEADS, n, n), F32)),
        grid=(n_seq, blocks, t // min(STEP_ROWS, t)),
        in_specs=in_specs,
        out_specs=(pl.BlockSpec((None, t, width), col(0)), state_spec),
        scratch_shapes=[seq_buf, seq_buf, seq_buf, pltpu.VMEM((t, LORA_WIDTH), F32),
                        dir_buf, dir_buf, pltpu.VMEM((2, PAIRS_PER_STEP, 2 * n, 2 * n), F32)],
        compiler_params=_params("arbitrary", "arbitrary", "arbitrary"),
        name="rwkv7_latent" if has_state else "rwkv7_context",
    )(*args)


def _merge_kernel(x_ref, ao_ref, ro_ref, sga_ref, sgr_ref, mod_ref, g_ref, wab_ref, wrb_ref,
                  wo_ref, x1_ref, h2_ref):
    merged = (sga_ref[...] * _dot(ao_ref[...], wab_ref[...])
              + sgr_ref[...] * _dot(ro_ref[...], wrb_ref[...]))
    x1 = x_ref[...] + mod_ref[2:3, :] * _dot(merged.astype(BF16), wo_ref[...])
    x1_ref[...] = x1
    h2 = _rms(x1) * g_ref[...]
    h2_ref[...] = (h2 * (1.0 + mod_ref[4:5, :]) + mod_ref[3:4, :]).astype(BF16)


def _merge(x2d, attn_o, rwkv_o, sga, sgr, mod, norm2_g, wab, wrb, wo, seq_len):
    n_tok = x2d.shape[0]
    tm = PROJ_ROWS
    row = lambda i: (i, 0)
    const = lambda i: (0, 0)
    return pl.pallas_call(
        _merge_kernel,
        out_shape=(jax.ShapeDtypeStruct((n_tok, D_MODEL), F32),
                   jax.ShapeDtypeStruct((n_tok, D_MODEL), BF16)),
        grid=(n_tok // tm,),
        in_specs=[
            pl.BlockSpec((tm, D_MODEL), row),
            pl.BlockSpec((tm, ATTN_WIDTH), row),
            pl.BlockSpec((tm, R_WIDTH), row),
            pl.BlockSpec((tm, D_MODEL), row),
            pl.BlockSpec((tm, D_MODEL), row),
            pl.BlockSpec((None, N_MOD, D_MODEL), lambda i: ((i * tm) // seq_len, 0, 0)),
            pl.BlockSpec((1, D_MODEL), const),
            pl.BlockSpec((ATTN_WIDTH, D_MODEL), const),
            pl.BlockSpec((R_WIDTH, D_MODEL), const),
            pl.BlockSpec((D_MODEL, D_MODEL), const),
        ],
        out_specs=(pl.BlockSpec((tm, D_MODEL), row), pl.BlockSpec((tm, D_MODEL), row)),
        compiler_params=_params("arbitrary"),
        name="merge_out_proj",
    )(x2d, attn_o, rwkv_o, sga, sgr, mod, norm2_g, wab, wrb, wo)


def _ffn_kernel(h2_ref, x1_ref, mod_ref, wu_ref, wg_ref, cu_ref, cg_ref, wd_ref, fg_ref,
                y_ref, acc_ref, upu_ref, upg_ref, *, seq_len):
    i = pl.program_id(0)
    j = pl.program_id(1)
    rows = h2_ref.shape[0]
    pad = FFN_PAD_ROWS
    rb = FFN_CONV_ROWS
    tile = 8

    @pl.when((i == 0) & (j == 0))
    def _():
        for ref in (upu_ref, upg_ref):
            ref[0:pad, :] = jnp.zeros((pad, ref.shape[1]), F32)
            ref[pad + rows:pad + rows + pad, :] = jnp.zeros((pad, ref.shape[1]), F32)

    @pl.when(j == 0)
    def _():
        acc_ref[...] = jnp.zeros(acc_ref.shape, F32)

    def up_project(r0):
        h2 = h2_ref[r0:r0 + rb, :]
        upu_ref[pad + r0:pad + r0 + rb, :] = _dot(h2, wu_ref[...])
        upg_ref[pad + r0:pad + r0 + rb, :] = _dot(h2, wg_ref[...])

    row0 = lax.broadcasted_iota(jnp.int32, (tile, 1), 0) == 0
    row7 = lax.broadcasted_iota(jnp.int32, (tile, 1), 0) == tile - 1

    taps_u = [cu_ref[k:k + 1, :] for k in range(3)]
    taps_g = [cg_ref[k:k + 1, :] for k in range(3)]

    def zero_rows(x, tile_offsets, row_mask):
        parts, done = [], 0
        for off in tile_offsets:
            parts += [x[done:off], jnp.where(row_mask, 0.0, x[off:off + tile])]
            done = off + tile
        parts.append(x[done:])
        return jnp.concatenate([p for p in parts if p.shape[0]], axis=0)

    def conv(up_ref, taps, r0):
        base = pad + r0
        cur = up_ref[base:base + rb, :]
        prev = up_ref[base - 1:base - 1 + rb, :]
        nxt = up_ref[base + 1:base + 1 + rb, :]
        first_seq = -(-r0 // seq_len) * seq_len
        starts = [s - r0 for s in range(first_seq, r0 + rb, seq_len)]
        ends = [e - r0 for e in range(first_seq if first_seq > r0 else first_seq + seq_len,
                                      r0 + rb + 1, seq_len)]
        prev = zero_rows(prev, starts, row0)
        nxt = zero_rows(nxt, [e - tile for e in ends], row7)
        return prev * taps[0] + cur * taps[1] + nxt * taps[2]

    starts = list(range(0, rows, rb))
    for r0 in starts[:2]:
        up_project(r0)
    for b, r0 in enumerate(starts):
        up_u = conv(upu_ref, taps_u, r0)
        up_g = conv(upg_ref, taps_g, r0)
        act = (up_g * _sigmoid(up_g) * up_u).astype(BF16)
        if b + 2 < len(starts):
            up_project(starts[b + 2])
        acc_ref[r0:r0 + rb, :] += _dot(act, wd_ref[...])

    @pl.when(j == pl.num_programs(1) - 1)
    def _():
        x2 = x1_ref[...] + mod_ref[...] * acc_ref[...]
        y_ref[...] = _rms(x2) * fg_ref[...]


def _ffn(h2, x1, gate2, ffn_up, ffn_conv, ffn_down, final_g, seq_len):
    n_tok = h2.shape[0]
    tm = FFN_ROWS
    tn = FFN_TILE
    nt = D_FF // tn
    assert tm % seq_len == 0 or seq_len % tm == 0
    gate_map = lambda i, j: ((i * tm) // seq_len, 0, 0)
    return pl.pallas_call(
        functools.partial(_ffn_kernel, seq_len=seq_len),
        out_shape=jax.ShapeDtypeStruct((n_tok, D_MODEL), F32),
        grid=(n_tok // tm, nt),
        in_specs=[
            pl.BlockSpec((tm, D_MODEL), lambda i, j: (i, 0)),
            pl.BlockSpec((tm, D_MODEL), lambda i, j: (i, 0)),
            pl.BlockSpec((None, 1, D_MODEL), gate_map),
            pl.BlockSpec((D_MODEL, tn), lambda i, j: (0, j)),
            pl.BlockSpec((D_MODEL, tn), lambda i, j: (0, nt + j)),
            pl.BlockSpec((3, tn), lambda i, j: (0, j)),
            pl.BlockSpec((3, tn), lambda i, j: (0, nt + j)),
            pl.BlockSpec((tn, D_MODEL), lambda i, j: (j, 0)),
            pl.BlockSpec((1, D_MODEL), lambda i, j: (0, 0)),
        ],
        out_specs=pl.BlockSpec((tm, D_MODEL), lambda i, j: (i, 0)),
        scratch_shapes=[pltpu.VMEM((tm, D_MODEL), F32),
                        pltpu.VMEM((tm + 2 * FFN_PAD_ROWS, tn), F32),
                        pltpu.VMEM((tm + 2 * FFN_PAD_ROWS, tn), F32)],
        compiler_params=_params("arbitrary", "arbitrary"),
        name="conv_ffn",
    )(h2, x1, gate2, ffn_up, ffn_up, ffn_conv, ffn_conv, ffn_down, final_g)


def _layer_path(x, mod, w, rope, cache, s0):
    b, t, _ = x.shape
    x2d = x.reshape(b * t, D_MODEL)
    qkv, rkv, lora, sga, sgr = _in_projection(x2d, mod, w["norm1_g"], w["w_in"], t)
    qkv = qkv.reshape(b, t, ATTN_WIDTH + 2 * KV_WIDTH)
    if rope is None:
        attn_o, k_norm, v_hm = _attention(qkv, w["q_norm_g"], w["k_norm_g"])
    else:
        attn_o = _attention(qkv, w["q_norm_g"], w["k_norm_g"], cache=cache, rope=rope)
        k_norm = v_hm = None
    rwkv_o, s_fin = _rwkv(rkv.reshape(b, t, 3 * R_WIDTH), lora.reshape(b, t, LORA_WIDTH),
                          w["rwkv"], s0)
    x1, h2 = _merge(x2d, attn_o.reshape(b * t, ATTN_WIDTH), rwkv_o.reshape(b * t, R_WIDTH),
                    sga, sgr, mod, w["norm2_g"], w["w_attn_br"], w["w_rwkv_br"], w["w_out"], t)
    gate2 = mod[:, 5:6, :]
    y = _ffn(h2, x1, gate2, w["ffn_up"], w["ffn_conv"], w["ffn_down"], w["final_norm_g"], t)
    return y.reshape(b, t, D_MODEL), k_norm, v_hm, s_fin


def _rope_tables(rows):
    row = jnp.repeat(jnp.arange(rows), GRID_W).astype(F32)
    col = jnp.tile(jnp.arange(GRID_W), rows).astype(F32)
    inv = 1.0 / (ROPE_THETA ** (jnp.arange(ROPE_PAIRS, dtype=F32) / ROPE_PAIRS))
    ang = jnp.concatenate([row[:, None] * inv, col[:, None] * inv], axis=-1)
    cos, sin = jnp.cos(ang), jnp.sin(ang)
    return jnp.concatenate([cos, cos] * 2, axis=-1), jnp.concatenate([-sin, sin] * 2, axis=-1)


def kernel(x_prompt, x_sample, cache_k, cache_v, state_rwkv, c, c_ctx, w_ada, b_ada, norm1_g, w_in, q_norm_g, k_norm_g, rwkv_conv, rwkv_w0, rwkv_w2, rwkv_a0, rwkv_a2, rwkv_g2, rwkv_k_k, rwkv_k_a, rwkv_r_k, rwkv_ln_g, rwkv_ln_b, w_attn_br, w_rwkv_br, w_out, norm2_g, ffn_up, ffn_conv, ffn_down, final_norm_g):
    depth = w_in.shape[0]
    assert depth == 1, "single trunk layer"
    l = 0
    n_ctx = x_prompt.shape[0]
    n_lat = x_sample.shape[0]

    cc = jnp.concatenate([c_ctx[None, :], c, jnp.zeros((8 - 1 - n_lat, D_MODEL), F32)], axis=0)
    mod_all = _modulation(cc, w_ada[l], b_ada[l][None, :])
    mod_ctx = jnp.broadcast_to(mod_all[0:1].reshape(1, N_MOD, D_MODEL), (n_ctx, N_MOD, D_MODEL))
    mod_lat = mod_all[1:1 + n_lat].reshape(n_lat, N_MOD, D_MODEL)

    rwkv_p = {
        "conv": rwkv_conv[l], "w0": rwkv_w0[l], "w2": rwkv_w2[l], "a0": rwkv_a0[l],
        "a2": rwkv_a2[l], "g2": rwkv_g2[l], "k_k": rwkv_k_k[l][None, :],
        "k_a": rwkv_k_a[l][None, :], "r_k": rwkv_r_k[l].reshape(1, R_WIDTH),
        "ln_g": rwkv_ln_g[l][None, :], "ln_b": rwkv_ln_b[l][None, :],
    }
    w = {
        "norm1_g": norm1_g[l][None, :], "w_in": w_in[l].astype(BF16),
        "q_norm_g": q_norm_g[l][None, :], "k_norm_g": k_norm_g[l][None, :],
        "rwkv": rwkv_p,
        "w_attn_br": w_attn_br[l].astype(BF16), "w_rwkv_br": w_rwkv_br[l].astype(BF16),
        "w_out": w_out[l].astype(BF16), "norm2_g": norm2_g[l][None, :],
        "ffn_up": ffn_up[l].astype(BF16), "ffn_conv": ffn_conv[l],
        "ffn_down": ffn_down[l].astype(BF16), "final_norm_g": final_norm_g[None, :],
    }

    y_prompt, k_ctx, v_ctx, s_ctx = _layer_path(x_prompt, mod_ctx, w, None, None, None)
    rope = _rope_tables(x_sample.shape[1] // GRID_W)
    y_sample, _, _, _ = _layer_path(x_sample, mod_lat, w, rope, (cache_k[:, l], cache_v[:, l]),
                                    state_rwkv[:, l])
    return (y_prompt, y_sample, k_ctx[:, None], v_ctx[:, None], s_ctx[:, None])
```
